```python
import jax, jax.numpy as jnp
from jax import lax
import numpy as np

D_MODEL = 1024
BATCH = 2
SEQ = 8192
DEPTH = 4

GLA_HEADS = 4
GLA_HEAD_K = 64
GLA_HEAD_V = 128
GLA_DK = GLA_HEADS * GLA_HEAD_K
GLA_DV = GLA_HEADS * GLA_HEAD_V
GLA_GATE_RANK = 16
GLA_GATE_TAU = 16.0
GLA_CHUNK = 64
GLA_NORM_EPS = 1e-5
RWKV_HEADS = 8
RWKV_HEAD = 64
RWKV_DIM = RWKV_HEADS * RWKV_HEAD
RWKV_DECAY_RANK = 64
RWKV_ICLR_RANK = 64
RWKV_GATE_RANK = 128
RWKV_GN_EPS = 64e-5
ATT_HEADS = 8
ATT_HEAD = 64
ATT_DIM = ATT_HEADS * ATT_HEAD
DILATED_PATTERNS = ((128, 1), (512, 4), (2048, 16))
ATT_BLOCK = 128
N_BRANCH = 3
FFN_HIDDEN = -(-8 * D_MODEL // (3 * 256)) * 256
LN_EPS = 1e-5
DEEPNORM_ALPHA = (2 * DEPTH) ** 0.25
DEEPNORM_BETA = (8 * DEPTH) ** -0.25

GLA_WIDTHS = (GLA_DK, GLA_DK, GLA_DV, GLA_GATE_RANK, GLA_DV)
RWKV_WIDTHS = (RWKV_DIM, RWKV_DIM, RWKV_DIM, RWKV_DECAY_RANK, RWKV_ICLR_RANK, RWKV_GATE_RANK)
ATT_WIDTHS = (ATT_DIM, ATT_DIM, ATT_DIM)
RWKV_IN = sum(RWKV_WIDTHS)
GROUP_WIDTHS = (sum(GLA_WIDTHS), RWKV_IN, sum(ATT_WIDTHS), N_BRANCH * D_MODEL)
D_IN = sum(GROUP_WIDTHS)

kernel_name = "hybrid_gla_rwkv7_dilated_deepnorm_adaln"


def _split(t, widths):
    return jnp.split(t, np.cumsum(widths)[:-1].tolist(), axis=-1)


def _split_heads(t, n_heads):
    b, s, _ = t.shape
    return t.reshape(b, s, n_heads, -1).transpose(0, 2, 1, 3)


def _merge_heads(t):
    b, h, s, d = t.shape
    return t.transpose(0, 2, 1, 3).reshape(b, s, h * d)


def layer_norm(x, g, b):
    xf = x.astype(jnp.float32)
    mu = jnp.mean(xf, -1, keepdims=True)
    var = jnp.mean(jnp.square(xf - mu), -1, keepdims=True)
    return ((xf - mu) * lax.rsqrt(var + LN_EPS) * g + b).astype(x.dtype)


def token_shift(p, mu):
    prev = jnp.pad(p, ((0, 0), (1, 0), (0, 0)))[:, :-1]
    return p + (prev - p) * mu


def alibi_slopes(n_heads):
    return 2.0 ** (-8.0 * jnp.arange(1, n_heads + 1, dtype=jnp.float32) / n_heads)


def gla_mixer(q, k, v, a_lo, og, w_alpha, b_alpha, norm_g):
    f32 = jnp.float32
    b_, s_, _ = q.shape
    log_a = jax.nn.log_sigmoid((a_lo @ w_alpha + b_alpha).astype(f32)) / GLA_GATE_TAU
    nc = s_ // GLA_CHUNK

    def chunk(t):
        t = _split_heads(t.astype(f32), GLA_HEADS)
        return t.reshape(b_, GLA_HEADS, nc, GLA_CHUNK, t.shape[-1])

    q, k, v, log_a = chunk(q) * GLA_HEAD_K ** -0.5, chunk(k), chunk(v), chunk(log_a)
    cum = jnp.cumsum(log_a, axis=3)
    cum_last = cum[:, :, :, -1:, :]
    q_e = q * jnp.exp(cum)
    k_e = k * jnp.exp(-cum)
    causal = jnp.tril(jnp.ones((GLA_CHUNK, GLA_CHUNK), bool))
    att = jnp.where(causal, jnp.einsum('bhnck,bhnsk->bhncs', q_e, k_e), 0.0)
    o = jnp.einsum('bhncs,bhnsv->bhncv', att, v)
    d_state = jnp.einsum('bhnck,bhncv->bhnkv', k * jnp.exp(cum_last - cum), v)
    decay = jnp.exp(cum_last[:, :, :, 0, :])

    def step(state, inp):
        dec, ds = inp
        return dec[..., None] * state + ds, state

    state0 = jnp.zeros((b_, GLA_HEADS, GLA_HEAD_K, GLA_HEAD_V), f32)
    _, s_prev = lax.scan(step, state0, (jnp.moveaxis(decay, 2, 0), jnp.moveaxis(d_state, 2, 0)))
    s_prev = jnp.moveaxis(s_prev, 0, 2)
    o = o + jnp.einsum('bhnck,bhnkv->bhncv', q_e, s_prev)
    o = o.reshape(b_, GLA_HEADS, s_, GLA_HEAD_V)
    o = o * lax.rsqrt(jnp.mean(o * o, -1, keepdims=True) + GLA_NORM_EPS) * norm_g
    return _merge_heads(o) * jax.nn.silu(og.astype(f32))


def rwkv7_mixer(r, k, v, w_lo, a_lo, g_lo, w0, w_up, a0, a_up, g_up, k_k, k_a, r_k, gn_g, gn_b):
    f32 = jnp.float32
    b_, s_, _ = r.shape
    r, k, v, w_lo, a_lo, g_lo = (t.astype(f32) for t in (r, k, v, w_lo, a_lo, g_lo))
    w_log = -jax.nn.softplus(-(w0 + jnp.tanh(w_lo) @ w_up)) - 0.5
    decay = jnp.exp(-jnp.exp(w_log))
    a = jax.nn.sigmoid(a0 + a_lo @ a_up)
    g = jax.nn.sigmoid(g_lo) @ g_up
    kk = k * k_k
    k = k * (1.0 + (a - 1.0) * k_a)
    hd = lambda t: t.reshape(b_, s_, RWKV_HEADS, RWKV_HEAD)
    r, k, v, kk, a, decay = map(hd, (r, k, v, kk, a, decay))
    kk = kk / jnp.maximum(jnp.sqrt(jnp.sum(kk * kk, -1, keepdims=True)), 1e-12)

    def step(state, inp):
        r_t, w_t, k_t, v_t, kk_t, b_t = inp
        sa = jnp.einsum('bhvk,bhk->bhv', state, -kk_t)
        state = state * w_t[:, :, None, :] + sa[..., None] * b_t[:, :, None, :] + v_t[..., None] * k_t[:, :, None, :]
        return state, jnp.einsum('bhvk,bhk->bhv', state, r_t)

    tm = lambda t: jnp.moveaxis(t, 1, 0)
    state0 = jnp.zeros((b_, RWKV_HEADS, RWKV_HEAD, RWKV_HEAD), f32)
    _, y = lax.scan(step, state0, (tm(r), tm(decay), tm(k), tm(v), tm(kk), tm(kk * a)))
    y = jnp.moveaxis(y, 0, 1)
    mu = jnp.mean(y, -1, keepdims=True)
    var = jnp.mean(jnp.square(y - mu), -1, keepdims=True)
    y = (y - mu) * lax.rsqrt(var + RWKV_GN_EPS) * gn_g.reshape(RWKV_HEADS, RWKV_HEAD) + gn_b.reshape(RWKV_HEADS, RWKV_HEAD)
    y = y + jnp.sum(r * k * r_k, -1, keepdims=True) * v
    return y.reshape(b_, s_, RWKV_DIM) * g


def dilated_pattern(q, k, v, window, dilation):
    b_, h_, s_, dh = q.shape
    span = window // dilation
    unit = dilation * ATT_BLOCK
    s_pad = -(-s_ // unit) * unit
    n_sub = s_pad // dilation
    nb = n_sub // ATT_BLOCK

    def to_blocks(t):
        t = jnp.pad(t, ((0, 0), (0, 0), (0, s_pad - s_), (0, 0)))
        t = t.reshape(b_, h_, n_sub, dilation, dh).transpose(0, 1, 3, 2, 4)
        return t.reshape(b_, h_, dilation, nb, ATT_BLOCK, dh)

    def with_prev(t):
        prev = jnp.pad(t, ((0, 0), (0, 0), (0, 0), (1, 0), (0, 0), (0, 0)))[:, :, :, :-1]
        return jnp.concatenate([prev, t], axis=4)

    qb, kb, vb = to_blocks(q), to_blocks(k), to_blocks(v)
    kc, vc = with_prev(kb), with_prev(vb)
    s = jnp.einsum('bhrnqd,bhrnkd->bhrnqk', qb, kc) * dh ** -0.5
    key_idx = jnp.arange(2 * ATT_BLOCK)
    steps = jnp.arange(ATT_BLOCK)[:, None] + ATT_BLOCK - key_idx[None, :]
    blk = jnp.arange(nb)[:, None, None]
    valid = (steps >= 0) & (steps <= span) & ((blk > 0) | (key_idx >= ATT_BLOCK)[None, None, :])
    bias = -alibi_slopes(h_)[:, None, None, None, None] * (steps * dilation).astype(jnp.float32)
    s = jnp.where(valid, s + bias, -jnp.inf)
    m = jnp.max(s, -1, keepdims=True)
    p = jnp.exp(s - m)
    den = jnp.sum(p, -1, keepdims=True)
    o = jnp.einsum('bhrnqk,bhrnkd->bhrnqd', p, vc) / den
    lse = (m + jnp.log(den))[..., 0]

    def from_blocks(t):
        tail = t.shape[5:]
        t = t.reshape(b_, h_, dilation, n_sub, *tail)
        t = jnp.moveaxis(t, 2, 3).reshape(b_, h_, s_pad, *tail)
        return t[:, :, :s_]

    return from_blocks(o), from_blocks(lse)


def dilated_attention(q, k, v):
    q, k, v = (_split_heads(t.astype(jnp.float32), ATT_HEADS) for t in (q, k, v))
    res = [dilated_pattern(q, k, v, w, d) for (w, d) in DILATED_PATTERNS]
    outs = jnp.stack([o for o, _ in res])
    lses = jnp.stack([l for _, l in res])
    wts = jax.nn.softmax(lses, axis=0)
    return _merge_heads(jnp.sum(wts[..., None] * outs, axis=0))


def hybrid_mixer(u, w_in, gla_w_alpha, gla_b_alpha, gla_norm_g, rwkv_mu, rwkv_w0, rwkv_w_up,
                 rwkv_a0, rwkv_a_up, rwkv_g_up, rwkv_k_k, rwkv_k_a, rwkv_r_k, rwkv_gn_g, rwkv_gn_b,
                 w_branch, w_out):
    b_, s_, _ = u.shape
    p = u @ w_in
    gla_p, rwkv_p, att_p, gate_p = _split(p, GROUP_WIDTHS)
    gq, gk, gv, ga, gg = _split(gla_p, GLA_WIDTHS)
    rr, rk, rv, rw, ra, rg = _split(token_shift(rwkv_p, rwkv_mu), RWKV_WIDTHS)
    aq, ak, av = _split(att_p, ATT_WIDTHS)
    o_a = gla_mixer(gq, gk, gv, ga, gg, gla_w_alpha, gla_b_alpha, gla_norm_g)
    o_b = rwkv7_mixer(rr, rk, rv, rw, ra, rg, rwkv_w0, rwkv_w_up, rwkv_a0, rwkv_a_up, rwkv_g_up,
                      rwkv_k_k, rwkv_k_a, rwkv_r_k, rwkv_gn_g, rwkv_gn_b)
    o_c = dilated_attention(aq, ak, av)
    branches = jnp.stack([o_a, o_b, o_c], axis=2).astype(u.dtype)
    proj = jnp.einsum('bsnc,ncd->bsnd', branches, w_branch)
    gates = jax.nn.sigmoid(gate_p.astype(jnp.float32)).reshape(b_, s_, N_BRANCH, D_MODEL)
    merged = jnp.sum(gates * proj, axis=2).astype(u.dtype)
    return merged @ w_out


def swiglu(u, w1, w2):
    gate, up = jnp.split(u @ w1, 2, axis=-1)
    return (jax.nn.silu(gate) * up) @ w2


def setup_inputs(seed: int = 0) -> dict:
    key = jax.random.key(seed)
    ks = iter(jax.random.split(key, 40))
    f32 = jnp.float32
    L = DEPTH
    nrm = lambda shape, scale: jax.random.normal(next(ks), shape, f32) * scale
    uni = lambda shape, lo, hi: jax.random.uniform(next(ks), shape, f32, lo, hi)
    return {
        "x": nrm((BATCH, SEQ, D_MODEL), 1.0),
        "c": nrm((BATCH, D_MODEL), 1.0),
        "w_ada": nrm((L, D_MODEL, 6 * D_MODEL), 0.1 * D_MODEL ** -0.5),
        "b_ada": nrm((L, 6 * D_MODEL), 0.01),
        "w_in": nrm((L, D_MODEL, D_IN), D_MODEL ** -0.5),
        "gla_w_alpha": nrm((L, GLA_GATE_RANK, GLA_DK), GLA_GATE_RANK ** -0.5),
        "gla_b_alpha": nrm((L, GLA_DK), 0.1),
        "gla_norm_g": 1.0 + nrm((L, GLA_HEAD_V), 0.02),
        "rwkv_mu": uni((L, RWKV_IN), 0.0, 1.0),
        "rwkv_w0": uni((L, RWKV_DIM), -6.0, 1.0),
        "rwkv_w_up": nrm((L, RWKV_DECAY_RANK, RWKV_DIM), 0.1 * RWKV_DECAY_RANK ** -0.5),
        "rwkv_a0": nrm((L, RWKV_DIM), 0.1),
        "rwkv_a_up": nrm((L, RWKV_ICLR_RANK, RWKV_DIM), 0.1 * RWKV_ICLR_RANK ** -0.5),
        "rwkv_g_up": nrm((L, RWKV_GATE_RANK, RWKV_DIM), RWKV_GATE_RANK ** -0.5),
        "rwkv_k_k": 0.85 + nrm((L, RWKV_DIM), 0.02),
        "rwkv_k_a": 1.0 + nrm((L, RWKV_DIM), 0.02),
        "rwkv_r_k": nrm((L, RWKV_HEADS, RWKV_HEAD), 0.1),
        "rwkv_gn_g": 1.0 + nrm((L, RWKV_DIM), 0.02),
        "rwkv_gn_b": nrm((L, RWKV_DIM), 0.01),
        "w_branch": nrm((L, N_BRANCH, ATT_DIM, D_MODEL), ATT_DIM ** -0.5),
        "w_out": nrm((L, D_MODEL, D_MODEL), DEEPNORM_BETA * D_MODEL ** -0.5),
        "ln1_g": 1.0 + nrm((L, D_MODEL), 0.02),
        "ln1_b": nrm((L, D_MODEL), 0.01),
        "ffn_w1": nrm((L, D_MODEL, 2 * FFN_HIDDEN), D_MODEL ** -0.5),
        "ffn_w2": nrm((L, FFN_HIDDEN, D_MODEL), DEEPNORM_BETA * FFN_HIDDEN ** -0.5),
        "ln2_g": 1.0 + nrm((L, D_MODEL), 0.02),
        "ln2_b": nrm((L, D_MODEL), 0.01),
    }


def reference(x, c, w_ada, b_ada, w_in, gla_w_alpha, gla_b_alpha, gla_norm_g, rwkv_mu, rwkv_w0,
              rwkv_w_up, rwkv_a0, rwkv_a_up, rwkv_g_up, rwkv_k_k, rwkv_k_a, rwkv_r_k, rwkv_gn_g,
              rwkv_gn_b, w_branch, w_out, ln1_g, ln1_b, ffn_w1, ffn_w2, ln2_g, ln2_b):
    for l in range(DEPTH):
        mod = jax.nn.silu(c) @ w_ada[l] + b_ada[l]
        sh1, sc1, g1, sh2, sc2, g2 = jnp.split(mod[:, None, :], 6, axis=-1)
        u = x * (1.0 + sc1) + sh1
        h = hybrid_mixer(u, w_in[l], gla_w_alpha[l], gla_b_alpha[l], gla_norm_g[l], rwkv_mu[l],
                         rwkv_w0[l], rwkv_w_up[l], rwkv_a0[l], rwkv_a_up[l], rwkv_g_up[l],
                         rwkv_k_k[l], rwkv_k_a[l], rwkv_r_k[l], rwkv_gn_g[l], rwkv_gn_b[l],
                         w_branch[l], w_out[l])
        x = layer_norm(DEEPNORM_ALPHA * x + (1.0 + g1) * h, ln1_g[l], ln1_b[l])
        u = x * (1.0 + sc2) + sh2
        h = swiglu(u, ffn_w1[l], ffn_w2[l])
        x = layer_norm(DEEPNORM_ALPHA * x + (1.0 + g2) * h, ln2_g[l], ln2_b[l])
    return x
```

```python
import functools
import math

import jax
import jax.numpy as jnp
from jax import lax
from jax.experimental import pallas as pl
from jax.experimental.pallas import tpu as pltpu

F32 = jnp.float32
BF16 = jnp.bfloat16

D_MODEL = 1024
DEPTH = 4
GLA_HEADS, GLA_HEAD_K, GLA_HEAD_V = 4, 64, 128
GLA_DK, GLA_DV = GLA_HEADS * GLA_HEAD_K, GLA_HEADS * GLA_HEAD_V
GLA_GATE_RANK = 16
GLA_GATE_TAU = 16.0
GLA_NORM_EPS = 1e-5
RWKV_HEADS, RWKV_HEAD = 8, 64
RWKV_DIM = RWKV_HEADS * RWKV_HEAD
RWKV_DECAY_RANK, RWKV_ICLR_RANK, RWKV_GATE_RANK = 64, 64, 128
RWKV_IN = 3 * RWKV_DIM + RWKV_DECAY_RANK + RWKV_ICLR_RANK + RWKV_GATE_RANK
RWKV_GN_EPS = 64e-5
ATT_HEADS, ATT_HEAD = 8, 64
ATT_DIM = ATT_HEADS * ATT_HEAD
ATT_BLOCK = 128
DILATIONS = (1, 4, 16)
N_BRANCH = 3
FFN_HIDDEN = 2816
LN_EPS = 1e-5
DEEPNORM_ALPHA = (2 * DEPTH) ** 0.25
GLA_IN = 2 * GLA_DK + GLA_DV + GLA_GATE_RANK + GLA_DV

LANES = 128
CHUNK = 64
VMEM_LIMIT_CAP = 60000 * 1024

NN = ((1,), (0,))
NT = ((1,), (1,))
TN = ((0,), (0,))


def _dg(a, b, dims=NN):
    return lax.dot_general(a, b, (dims, ((), ())), preferred_element_type=F32)


def _dot1(a, b, dims=NN):
    return _dg(a.astype(BF16), b.astype(BF16), dims)


def _split2(a):
    hi = a.astype(BF16)
    lo = (a - hi.astype(F32)).astype(BF16)
    return hi, lo


def _dot3(a, b, dims=NN):
    ah, al = _split2(a)
    bh, bl = _split2(b)
    return _dg(ah, bh, dims) + (_dg(ah, bl, dims) + _dg(al, bh, dims))


def _dot_exact_rhs(a, m_bf16, dims=NN, parts=3):
    acc = None
    rem = a
    for _ in range(parts):
        hi = rem.astype(BF16)
        term = _dg(hi, m_bf16, dims)
        acc = term if acc is None else acc + term
        rem = rem - hi.astype(F32)
    return acc


def _dot_exact_lhs(m_bf16, a, parts=3):
    acc = None
    rem = a
    for _ in range(parts):
        hi = rem.astype(BF16)
        term = _dg(m_bf16, hi)
        acc = term if acc is None else acc + term
        rem = rem - hi.astype(F32)
    return acc


def _sigmoid(x):
    return 1.0 / (1.0 + jnp.exp(-x))


def _silu(x):
    return x * _sigmoid(x)


def _log_sigmoid(x):
    return jnp.minimum(x, 0.0) - jnp.log(1.0 + jnp.exp(-jnp.abs(x)))


def _layer_norm(z, g, b):
    mu = jnp.mean(z, axis=-1, keepdims=True)
    zc = z - mu
    var = jnp.mean(zc * zc, axis=-1, keepdims=True)
    return zc * lax.rsqrt(var + LN_EPS) * g + b


def _stack_heads(x, lane):
    lo = jnp.where(lane < RWKV_HEAD, x, 0.0)
    hi = jnp.where(lane >= RWKV_HEAD, x, 0.0)
    return jnp.concatenate([lo, hi], axis=0)


def _cparams(sem, vmem_bytes):
    return pltpu.CompilerParams(dimension_semantics=sem,
                                vmem_limit_bytes=int(min(vmem_bytes, VMEM_LIMIT_CAP)))


def _const_spec(shape):
    zeros = (0,) * len(shape)
    return pl.BlockSpec(shape, lambda *_: zeros, pipeline_mode=pl.Buffered(1))


def _mod_kernel(c_ref, w_ref, b_ref, o_ref):
    c = c_ref[...]
    s = _silu(c)
    o_ref[0] = jnp.dot(s, w_ref[0], preferred_element_type=F32,
                       precision=lax.Precision.HIGHEST) + b_ref[0]


def _modulation(c, w_ada, b_ada):
    n_l = w_ada.shape[0]
    b = c.shape[0]
    rows = 8
    c_pad = jnp.zeros((rows, D_MODEL), F32).at[:b].set(c)
    tn = 1536
    out = pl.pallas_call(
        _mod_kernel,
        grid=(n_l, 6 * D_MODEL // tn),
        in_specs=[pl.BlockSpec((rows, D_MODEL), lambda l, j: (0, 0)),
                  pl.BlockSpec((1, D_MODEL, tn), lambda l, j: (l, 0, j)),
                  pl.BlockSpec((1, 1, tn), lambda l, j: (l, 0, j))],
        out_specs=pl.BlockSpec((1, rows, tn), lambda l, j: (l, 0, j)),
        out_shape=jax.ShapeDtypeStruct((n_l, rows, 6 * D_MODEL), F32),
        compiler_params=_cparams(("parallel", "parallel"), 4 * D_MODEL * tn * 4),
        name="adaln_mod",
    )(c_pad, w_ada, b_ada.reshape(n_l, 1, 6 * D_MODEL))
    return out[:, :b].reshape(n_l, b, 6, D_MODEL)


def _gla_kernel(x_ref, mod_ref, wm_ref, wga_ref, wal_ref, bal_ref, ng_ref, o_ref,
                st_ref, p_s, cum_s, o_s, *, tile):
    @pl.when(pl.program_id(1) == 0)
    def _():
        st_ref[...] = jnp.zeros_like(st_ref)

    x = x_ref[0]
    u = (x * (1.0 + mod_ref[0, 1:2, :]) + mod_ref[0, 0:1, :]).astype(BF16)
    p_s[...] = _dg(u, wm_ref[...])
    a_lo = _dg(u, wga_ref[...])
    z = _dot3(a_lo, wal_ref[...]) + bal_ref[...]
    log_a = _log_sigmoid(z) * (1.0 / GLA_GATE_TAU)
    ri = lax.broadcasted_iota(jnp.int32, (tile, tile), 0)
    ci = lax.broadcasted_iota(jnp.int32, (tile, tile), 1)
    ltri = jnp.where(((ri >> 6) == (ci >> 6)) & (ci <= ri), 1.0, 0.0).astype(BF16)
    cum_s[...] = _dot_exact_lhs(ltri, log_a)

    lane = lax.broadcasted_iota(jnp.int32, (CHUNK, LANES), 1)
    i2 = lax.broadcasted_iota(jnp.int32, (2 * CHUNK, 2 * CHUNK), 0)
    j2 = lax.broadcasted_iota(jnp.int32, (2 * CHUNK, 2 * CHUNK), 1)
    causal = ((i2 >> 6) == (j2 >> 6)) & (j2 <= i2)
    scale = GLA_HEAD_K ** -0.5

    def chunk_body(c, carry):
        r0 = pl.multiple_of(c * CHUNK, CHUNK)
        rows = pl.ds(r0, CHUNK)
        for pair in range(GLA_HEADS // 2):
            lo = pair * LANES
            cumc = cum_s[rows, lo:lo + LANES]
            qc = p_s[rows, lo:lo + LANES]
            kc = p_s[rows, GLA_DK + lo:GLA_DK + lo + LANES]
            vbase = 2 * GLA_DK + 2 * pair * GLA_HEAD_V
            vst = jnp.concatenate([p_s[rows, vbase:vbase + GLA_HEAD_V],
                                   p_s[rows, vbase + GLA_HEAD_V:vbase + 2 * GLA_HEAD_V]], axis=0)
            cl = cumc[CHUNK - 1:CHUNK, :]
            qsm = _stack_heads(qc * scale * jnp.exp(cumc), lane)
            ksm = _stack_heads(kc * jnp.exp(-cumc), lane)
            kdsm = _stack_heads(kc * jnp.exp(cl - cumc), lane)
            att = jnp.where(causal, _dot1(qsm, ksm, NT), 0.0)
            gt = st_ref[pair]
            o_st = _dot1(att, vst) + _dot1(qsm, gt, NT)
            st_ref[pair] = gt * jnp.exp(cl) + _dot1(vst, kdsm, TN)
            ob = 2 * pair * GLA_HEAD_V
            o_s[rows, ob:ob + GLA_HEAD_V] = o_st[:CHUNK]
            o_s[rows, ob + GLA_HEAD_V:ob + 2 * GLA_HEAD_V] = o_st[CHUNK:]
        return carry

    lax.fori_loop(0, tile // CHUNK, chunk_body, 0)

    og_base = 2 * GLA_DK + GLA_DV
    for h in range(GLA_HEADS):
        sl = slice(h * GLA_HEAD_V, (h + 1) * GLA_HEAD_V)
        oh = o_s[:, sl]
        on = oh * lax.rsqrt(jnp.mean(oh * oh, axis=-1, keepdims=True) + GLA_NORM_EPS) * ng_ref[...]
        og = p_s[:, og_base + h * GLA_HEAD_V:og_base + (h + 1) * GLA_HEAD_V]
        o_ref[0, :, sl] = on * _silu(og)


def _gla_layer(x, mod, wm, wga, wal, bal, ng, tile=512):
    b, s, _ = x.shape
    n_main = 2 * GLA_DK + 2 * GLA_DV
    vmem = (4 * tile * D_MODEL * 4 + 4 * tile * GLA_DV * 4 + D_MODEL * (n_main + LANES) * 2
            + tile * (n_main + GLA_DK + GLA_DV) * 4 + 6 * tile * n_main * 4 + (8 << 20))
    return pl.pallas_call(
        functools.partial(_gla_kernel, tile=tile),
        grid=(b, s // tile),
        in_specs=[pl.BlockSpec((1, tile, D_MODEL), lambda i, j: (i, j, 0)),
                  pl.BlockSpec((1, 6, D_MODEL), lambda i, j: (i, 0, 0)),
                  _const_spec((D_MODEL, n_main)),
                  _const_spec((D_MODEL, LANES)),
                  _const_spec((LANES, GLA_DK)),
                  _const_spec((1, GLA_DK)),
                  _const_spec((1, GLA_HEAD_V))],
        out_specs=pl.BlockSpec((1, tile, GLA_DV), lambda i, j: (i, j, 0)),
        out_shape=jax.ShapeDtypeStruct((b, s, GLA_DV), F32),
        scratch_shapes=[pltpu.VMEM((GLA_HEADS // 2, GLA_HEAD_V, LANES), F32),
                        pltpu.VMEM((tile, n_main), F32),
                        pltpu.VMEM((tile, GLA_DK), F32),
                        pltpu.VMEM((tile, GLA_DV), F32)],
        compiler_params=_cparams(("parallel", "arbitrary"), vmem),
        name="gla_mixer",
    )(x, mod, wm, wga, wal, bal, ng)


def _inv_unit_lower(n, eye, m16, m32, m64):
    d = jnp.where(m16, n, 0.0)
    x = eye + d
    pw = d
    for _ in range(3):
        pw = _dot3(pw, pw)
        x = x + _dot3(x, pw)
    for m in (m32, m64):
        o = jnp.where(m, n, 0.0)
        x = x + _dot3(_dot3(x, o), x)
    return x


def _rwkv_kernel(x_ref, mod_ref, w_ref, mu_ref, w0_ref, wup_ref, a0_ref, aup_ref, gup_ref,
                 kk_ref, ka_ref, rk_ref, gng_ref, gnb_ref, o_ref,
                 st_ref, carry_ref, a_s, b_s, k_s, r_s, v_s, bb_s, k2_s, cum_s, y_s, bon_s, g_s,
                 *, tile):
    @pl.when(pl.program_id(1) == 0)
    def _():
        st_ref[...] = jnp.zeros_like(st_ref)
        carry_ref[...] = jnp.zeros_like(carry_ref)

    x = x_ref[0]
    u = (x * (1.0 + mod_ref[0, 1:2, :]) + mod_ref[0, 0:1, :]).astype(BF16)
    p = _dg(u, w_ref[...])
    row = lax.broadcasted_iota(jnp.int32, (tile, 1), 0)
    prev = jnp.where(row == 0, carry_ref[0:1, :], pltpu.roll(p, 1, 0))
    carry_ref[0:1, :] = p[tile - 1:tile, :]
    ps = p + (prev - p) * mu_ref[...]

    d = RWKV_DIM
    r = ps[:, 0:d]
    k = ps[:, d:2 * d]
    v = ps[:, 2 * d:3 * d]
    wa_lo = ps[:, 3 * d:3 * d + LANES]
    g_lo = ps[:, 3 * d + LANES:3 * d + 2 * LANES]
    wl = w0_ref[...] + _dot3(jnp.tanh(wa_lo), wup_ref[...])
    lw = -_sigmoid(wl) * math.exp(-0.5)
    a = _sigmoid(a0_ref[...] + _dot3(wa_lo, aup_ref[...]))
    g_s[...] = _dot3(_sigmoid(g_lo), gup_ref[...])
    kk = k * kk_ref[...]
    k2 = k * (1.0 + (a - 1.0) * ka_ref[...])

    bi = lax.broadcasted_iota(jnp.int32, (LANES, LANES), 0)
    bj = lax.broadcasted_iota(jnp.int32, (LANES, LANES), 1)
    same64 = (bi >> 6) == (bj >> 6)
    seg = jnp.where(same64, 1.0, 0.0).astype(BF16)

    def seg_sum(t):
        return jnp.concatenate(
            [_dot_exact_rhs(t[:, q * LANES:(q + 1) * LANES], seg) for q in range(d // LANES)],
            axis=1)

    kk = kk / jnp.maximum(jnp.sqrt(seg_sum(kk * kk)), 1e-12)
    bb = kk * a
    bon_s[...] = seg_sum(r * k2 * rk_ref[...]) * v

    ri = lax.broadcasted_iota(jnp.int32, (tile, tile), 0)
    ci = lax.broadcasted_iota(jnp.int32, (tile, tile), 1)
    ltri = jnp.where(((ri >> 6) == (ci >> 6)) & (ci <= ri), 1.0, 0.0).astype(BF16)
    cum = _dot_exact_lhs(ltri, lw)
    e_neg = jnp.exp(-cum)
    a_s[...] = -kk * jnp.exp(cum - lw)
    b_s[...] = bb * e_neg
    k_s[...] = k2 * e_neg
    r_s[...] = r * jnp.exp(cum)
    v_s[...] = v
    bb_s[...] = bb
    k2_s[...] = k2
    cum_s[...] = cum

    lane = lax.broadcasted_iota(jnp.int32, (CHUNK, LANES), 1)
    strict = same64 & (bj < bi)
    incl = same64 & (bj <= bi)
    m16 = (bi >> 4) == (bj >> 4)
    m32 = ((bi >> 5) == (bj >> 5)) & jnp.logical_not(m16)
    m64 = same64 & ((bi >> 5) != (bj >> 5))
    eye = jnp.where(bi == bj, 1.0, 0.0)

    def chunk_body(c, carry):
        r0 = pl.multiple_of(c * CHUNK, CHUNK)
        rows = pl.ds(r0, CHUNK)
        for pair in range(RWKV_HEADS // 2):
            sl = slice(pair * LANES, (pair + 1) * LANES)
            asm = _stack_heads(a_s[rows, sl], lane)
            bsm = _stack_heads(b_s[rows, sl], lane)
            ksm = _stack_heads(k_s[rows, sl], lane)
            rsm = _stack_heads(r_s[rows, sl], lane)
            vsm = _stack_heads(v_s[rows, sl], lane)
            ab = jnp.where(strict, _dot3(asm, bsm, NT), 0.0)
            ak = jnp.where(strict, _dot3(asm, ksm, NT), 0.0)
            rb = jnp.where(incl, _dot3(rsm, bsm, NT), 0.0)
            rk = jnp.where(incl, _dot3(rsm, ksm, NT), 0.0)
            tinv = _inv_unit_lower(ab, eye, m16, m32, m64)
            w_mat = _dot3(tinv, asm)
            u0 = _dot3(tinv, _dot3(ak, vsm))
            gt = st_ref[pair]
            u_mat = _dot3(w_mat, gt, NT) + u0
            y = _dot3(rsm, gt, NT) + _dot3(rb, u_mat) + _dot3(rk, vsm)
            cumc = cum_s[rows, sl]
            cl = cumc[CHUNK - 1:CHUNK, :]
            dec = jnp.exp(cl - cumc)
            bdm = _stack_heads(bb_s[rows, sl] * dec, lane)
            kdm = _stack_heads(k2_s[rows, sl] * dec, lane)
            st_ref[pair] = gt * jnp.exp(cl) + _dot3(u_mat, bdm, TN) + _dot3(vsm, kdm, TN)
            y_s[rows, sl] = y[:CHUNK] + y[CHUNK:]
        return carry

    lax.fori_loop(0, tile // CHUNK, chunk_body, 0)

    y = y_s[...]
    inv_n = 1.0 / RWKV_HEAD
    mu_h = seg_sum(y) * inv_n
    yc = y - mu_h
    var = seg_sum(yc * yc) * inv_n
    yn = yc * lax.rsqrt(var + RWKV_GN_EPS) * gng_ref[...] + gnb_ref[...]
    o_ref[0] = (yn + bon_s[...]) * g_s[...]


def _rwkv_layer(x, mod, w, mu, w0, wup, a0, aup, gup, k_k, k_a, r_k, gn_g, gn_b, tile=256):
    b, s, _ = x.shape
    d = RWKV_DIM
    vec = lambda: _const_spec((1, d))
    vmem = (4 * tile * D_MODEL * 4 + 4 * tile * d * 4 + D_MODEL * RWKV_IN * 2
            + 11 * tile * d * 4 + 8 * tile * RWKV_IN * 4 + (12 << 20))
    return pl.pallas_call(
        functools.partial(_rwkv_kernel, tile=tile),
        grid=(b, s // tile),
        in_specs=[pl.BlockSpec((1, tile, D_MODEL), lambda i, j: (i, j, 0)),
                  pl.BlockSpec((1, 6, D_MODEL), lambda i, j: (i, 0, 0)),
                  _const_spec((D_MODEL, RWKV_IN)),
                  _const_spec((1, RWKV_IN)),
                  vec(), _const_spec((LANES, d)), vec(), _const_spec((LANES, d)),
                  _const_spec((LANES, d)), vec(), vec(), vec(), vec(), vec()],
        out_specs=pl.BlockSpec((1, tile, d), lambda i, j: (i, j, 0)),
        out_shape=jax.ShapeDtypeStruct((b, s, d), F32),
        scratch_shapes=[pltpu.VMEM((RWKV_HEADS // 2, LANES, LANES), F32),
                        pltpu.VMEM((8, RWKV_IN), F32)]
                       + [pltpu.VMEM((tile, d), F32) for _ in range(11)],
        compiler_params=_cparams(("parallel", "arbitrary"), vmem),
        name="rwkv7_mixer",
    )(x, mod, w, mu, w0, wup, a0, aup, gup, k_k, k_a, r_k, gn_g, gn_b)


def _qkv_kernel(x_ref, mod_ref, w_ref, q_ref, k_ref, v_ref):
    x = x_ref[0]
    u = (x * (1.0 + mod_ref[0, 1:2, :]) + mod_ref[0, 0:1, :]).astype(BF16)
    p = _dg(u, w_ref[...])
    q_ref[0] = (p[:, :ATT_DIM] * ATT_HEAD ** -0.5).astype(BF16)
    k_ref[0] = p[:, ATT_DIM:2 * ATT_DIM].astype(BF16)
    v_ref[0] = p[:, 2 * ATT_DIM:].astype(BF16)


def _qkv_layer(x, mod, w, tile=512):
    b, s, _ = x.shape
    out = jax.ShapeDtypeStruct((b, s, ATT_DIM), BF16)
    ospec = pl.BlockSpec((1, tile, ATT_DIM), lambda i, j: (i, j, 0))
    vmem = 4 * tile * D_MODEL * 4 + D_MODEL * 3 * ATT_DIM * 2 + 4 * tile * 3 * ATT_DIM * 4 + (4 << 20)
    return pl.pallas_call(
        _qkv_kernel,
        grid=(b, s // tile),
        in_specs=[pl.BlockSpec((1, tile, D_MODEL), lambda i, j: (i, j, 0)),
                  pl.BlockSpec((1, 6, D_MODEL), lambda i, j: (i, 0, 0)),
                  _const_spec((D_MODEL, 3 * ATT_DIM))],
        out_specs=[ospec, ospec, ospec],
        out_shape=[out, out, out],
        compiler_params=_cparams(("parallel", "parallel"), vmem),
        name="att_qkv",
    )(x, mod, w)


def _att_kernel(q_ref, kp_ref, kc_ref, vp_ref, vc_ref, o_ref, l_ref, *, dilation):
    n = pl.program_id(2)
    q = q_ref[0]
    kcat = jnp.concatenate([kp_ref[0], kc_ref[0]], axis=0)
    vcat = jnp.concatenate([vp_ref[0], vc_ref[0]], axis=0)
    qi = lax.broadcasted_iota(jnp.int32, (ATT_BLOCK, 2 * ATT_BLOCK), 0)
    kj = lax.broadcasted_iota(jnp.int32, (ATT_BLOCK, 2 * ATT_BLOCK), 1)
    steps = qi + ATT_BLOCK - kj
    first_key = jnp.where(n > 0, 0, ATT_BLOCK)
    valid = (steps >= 0) & (steps <= ATT_BLOCK) & (kj >= first_key)
    dist = (steps * dilation).astype(F32)
    lane_q = lax.broadcasted_iota(jnp.int32, (ATT_BLOCK, LANES), 1)
    zero = jnp.zeros((), BF16)
    lse_blk = jnp.zeros((ATT_BLOCK, LANES), F32)
    for pair in range(ATT_HEADS // 2):
        sl = slice(pair * LANES, (pair + 1) * LANES)
        qp, kp, vp = q[:, sl], kcat[:, sl], vcat[:, sl]
        outs = []
        for half in range(2):
            h = 2 * pair + half
            slope = 2.0 ** (-8.0 * (h + 1) / ATT_HEADS)
            sel = (lane_q >= ATT_HEAD) if half else (lane_q < ATT_HEAD)
            s = _dg(jnp.where(sel, qp, zero), kp, NT)
            s = jnp.where(valid, s - slope * dist, -jnp.inf)
            m = jnp.max(s, axis=-1, keepdims=True)
            e = jnp.exp(s - m)
            den = jnp.sum(e, axis=-1, keepdims=True)
            outs.append(_dg(e.astype(BF16), vp) / den)
            lse_blk = jnp.where(lane_q == h, m + jnp.log(den), lse_blk)
        o_ref[0, :, sl] = jnp.where(lane_q < ATT_HEAD, outs[0], outs[1])
    l_ref[0] = lse_blk


def _att_pattern(q, k, v, dilation):
    b, s, _ = q.shape
    n_sub = s // dilation
    nb = n_sub // ATT_BLOCK
    width = dilation * ATT_DIM
    view = lambda t: t.reshape(b, n_sub, width)
    cur = lambda i, r, n: (i, n, r)
    prev = lambda i, r, n: (i, jnp.maximum(n - 1, 0), r)
    blk = (1, ATT_BLOCK, ATT_DIM)
    o, lse = pl.pallas_call(
        functools.partial(_att_kernel, dilation=dilation),
        grid=(b, dilation, nb),
        in_specs=[pl.BlockSpec(blk, cur), pl.BlockSpec(blk, prev), pl.BlockSpec(blk, cur),
                  pl.BlockSpec(blk, prev), pl.BlockSpec(blk, cur)],
        out_specs=[pl.BlockSpec(blk, cur),
                   pl.BlockSpec((1, ATT_BLOCK, LANES), cur)],
        out_shape=[jax.ShapeDtypeStruct((b, n_sub, width), F32),
                   jax.ShapeDtypeStruct((b, n_sub, dilation * LANES), F32)],
        compiler_params=_cparams(("parallel", "parallel", "arbitrary"), 24 << 20),
        name=f"dilated_att_d{dilation}",
    )(view(q), view(k), view(k), view(v), view(v))
    return o.reshape(b, s, ATT_DIM), lse.reshape(b, s, LANES)


def _merge_kernel(x_ref, mod_ref, oa_ref, ob_ref, o1_ref, o4_ref, o16_ref, l1_ref, l4_ref, l16_ref,
                  wg_ref, wb_ref, wo_ref, g_ref, b_ref, out_ref):
    x = x_ref[0]
    u = (x * (1.0 + mod_ref[0, 1:2, :]) + mod_ref[0, 0:1, :]).astype(BF16)
    gates = _sigmoid(_dg(u, wg_ref[...]))
    l1, l4, l16 = l1_ref[0], l4_ref[0], l16_ref[0]
    m = jnp.maximum(jnp.maximum(l1, l4), l16)
    e1, e4, e16 = jnp.exp(l1 - m), jnp.exp(l4 - m), jnp.exp(l16 - m)
    inv = 1.0 / (e1 + e4 + e16)
    hi = lax.broadcasted_iota(jnp.int32, (LANES, ATT_DIM), 0)
    hj = lax.broadcasted_iota(jnp.int32, (LANES, ATT_DIM), 1)
    expand = jnp.where(hi == (hj >> 6), 1.0, 0.0).astype(BF16)
    o_c = (_dot_exact_rhs(e1 * inv, expand, parts=2) * o1_ref[0]
           + _dot_exact_rhs(e4 * inv, expand, parts=2) * o4_ref[0]
           + _dot_exact_rhs(e16 * inv, expand, parts=2) * o16_ref[0])
    merged = (gates[:, :D_MODEL] * _dot1(oa_ref[0], wb_ref[0])
              + gates[:, D_MODEL:2 * D_MODEL] * _dot1(ob_ref[0], wb_ref[1])
              + gates[:, 2 * D_MODEL:] * _dot1(o_c, wb_ref[2]))
    h = _dot1(merged, wo_ref[...])
    z = DEEPNORM_ALPHA * x + (1.0 + mod_ref[0, 2:3, :]) * h
    out_ref[0] = _layer_norm(z, g_ref[...], b_ref[...])


def _merge_layer(x, mod, oa, ob, oc, lses, wg, wb, wo, g, bta, tile=512):
    b, s, _ = x.shape
    xs = pl.BlockSpec((1, tile, D_MODEL), lambda i, j: (i, j, 0))
    bs = pl.BlockSpec((1, tile, ATT_DIM), lambda i, j: (i, j, 0))
    ls = pl.BlockSpec((1, tile, LANES), lambda i, j: (i, j, 0))
    vmem = (4 * tile * D_MODEL * 4 + 10 * tile * ATT_DIM * 4 + 6 * tile * LANES * 4
            + (3 * D_MODEL * D_MODEL + 3 * ATT_DIM * D_MODEL + D_MODEL * D_MODEL) * 2
            + 6 * tile * 3 * D_MODEL * 4 + (4 << 20))
    return pl.pallas_call(
        _merge_kernel,
        grid=(b, s // tile),
        in_specs=[xs, pl.BlockSpec((1, 6, D_MODEL), lambda i, j: (i, 0, 0)),
                  bs, bs, bs, bs, bs, ls, ls, ls,
                  _const_spec((D_MODEL, N_BRANCH * D_MODEL)),
                  _const_spec((N_BRANCH, ATT_DIM, D_MODEL)),
                  _const_spec((D_MODEL, D_MODEL)),
                  _const_spec((1, D_MODEL)), _const_spec((1, D_MODEL))],
        out_specs=xs,
        out_shape=jax.ShapeDtypeStruct((b, s, D_MODEL), F32),
        compiler_params=_cparams(("parallel", "parallel"), vmem),
        name="merge_ln1",
    )(x, mod, oa, ob, oc[0], oc[1], oc[2], lses[0], lses[1], lses[2], wg, wb, wo, g, bta)


def _ffn_kernel(x_ref, mod_ref, w1_ref, w2_ref, g_ref, b_ref, out_ref):
    x = x_ref[0]
    u = (x * (1.0 + mod_ref[0, 4:5, :]) + mod_ref[0, 3:4, :]).astype(BF16)
    h = _dg(u, w1_ref[...])
    act = (_silu(h[:, :FFN_HIDDEN]) * h[:, FFN_HIDDEN:]).astype(BF16)
    y = _dg(act, w2_ref[...])
    z = DEEPNORM_ALPHA * x + (1.0 + mod_ref[0, 5:6, :]) * y
    out_ref[0] = _layer_norm(z, g_ref[...], b_ref[...])


def _ffn_layer(x, mod, w1, w2, g, bta, tile=512):
    b, s, _ = x.shape
    xs = pl.BlockSpec((1, tile, D_MODEL), lambda i, j: (i, j, 0))
    vmem = (4 * tile * D_MODEL * 4 + 3 * D_MODEL * FFN_HIDDEN * 2
            + 4 * tile * 2 * FFN_HIDDEN * 4 + (4 << 20))
    return pl.pallas_call(
        _ffn_kernel,
        grid=(b, s // tile),
        in_specs=[xs, pl.BlockSpec((1, 6, D_MODEL), lambda i, j: (i, 0, 0)),
                  _const_spec((D_MODEL, 2 * FFN_HIDDEN)),
                  _const_spec((FFN_HIDDEN, D_MODEL)),
                  _const_spec((1, D_MODEL)), _const_spec((1, D_MODEL))],
        out_specs=xs,
        out_shape=jax.ShapeDtypeStruct((b, s, D_MODEL), F32),
        compiler_params=_cparams(("parallel", "parallel"), vmem),
        name="ffn_ln2",
    )(x, mod, w1, w2, g, bta)


def _pad_rows(m, rows, offset=0):
    out = jnp.zeros((rows,) + m.shape[1:], m.dtype)
    return out.at[offset:offset + m.shape[0]].set(m)


def _mixer_layer(x, mod, w_in, gla_w_alpha, gla_b_alpha, gla_norm_g, rwkv_mu, rwkv_w0, rwkv_w_up,
                 rwkv_a0, rwkv_a_up, rwkv_g_up, rwkv_k_k, rwkv_k_a, rwkv_r_k, rwkv_gn_g,
                 rwkv_gn_b, w_branch, w_out, ln_g, ln_b):
    row = lambda t: t.reshape(1, -1)
    o0 = 0
    gq_end = 2 * GLA_DK + GLA_DV
    w_gla = jnp.concatenate([w_in[:, :gq_end], w_in[:, gq_end + GLA_GATE_RANK:GLA_IN]],
                            axis=1).astype(BF16)
    w_ga = jnp.zeros((D_MODEL, LANES), F32).at[:, :GLA_GATE_RANK].set(
        w_in[:, gq_end:gq_end + GLA_GATE_RANK]).astype(BF16)
    o0 += GLA_IN
    w_rwkv = w_in[:, o0:o0 + RWKV_IN].astype(BF16)
    o0 += RWKV_IN
    w_att = w_in[:, o0:o0 + 3 * ATT_DIM].astype(BF16)
    o0 += 3 * ATT_DIM
    w_gate = w_in[:, o0:].astype(BF16)

    o_a = _gla_layer(x, mod, w_gla, w_ga, _pad_rows(gla_w_alpha, LANES), row(gla_b_alpha),
                     row(gla_norm_g))
    o_b = _rwkv_layer(x, mod, w_rwkv, row(rwkv_mu), row(rwkv_w0),
                      _pad_rows(rwkv_w_up, LANES), row(rwkv_a0),
                      _pad_rows(rwkv_a_up, LANES, RWKV_DECAY_RANK), rwkv_g_up,
                      row(rwkv_k_k), row(rwkv_k_a), row(rwkv_r_k), row(rwkv_gn_g),
                      row(rwkv_gn_b))
    q, k, v = _qkv_layer(x, mod, w_att)
    res = [_att_pattern(q, k, v, dil) for dil in DILATIONS]
    return _merge_layer(x, mod, o_a, o_b, [o for o, _ in res], [l for _, l in res],
                        w_gate, w_branch.astype(BF16), w_out.astype(BF16), row(ln_g), row(ln_b))


def kernel(x, c, w_ada, b_ada, w_in, gla_w_alpha, gla_b_alpha, gla_norm_g, rwkv_mu, rwkv_w0,
           rwkv_w_up, rwkv_a0, rwkv_a_up, rwkv_g_up, rwkv_k_k, rwkv_k_a, rwkv_r_k, rwkv_gn_g,
           rwkv_gn_b, w_branch, w_out, ln1_g, ln1_b, ffn_w1, ffn_w2, ln2_g, ln2_b):
    mod = _modulation(c, w_ada, b_ada)
    for l in range(DEPTH):
        x = _mixer_layer(x, mod[l], w_in[l], gla_w_alpha[l], gla_b_alpha[l], gla_norm_g[l],
                         rwkv_mu[l], rwkv_w0[l], rwkv_w_up[l], rwkv_a0[l], rwkv_a_up[l],
                         rwkv_g_up[l], rwkv_k_k[l], rwkv_k_a[l], rwkv_r_k[l], rwkv_gn_g[l],
                         rwkv_gn_b[l], w_branch[l], w_out[l], ln1_g[l], ln1_b[l])
        x = _ffn_layer(x, mod[l], ffn_w1[l].astype(BF16), ffn_w2[l].astype(BF16),
                       ln2_g[l].reshape(1, -1), ln2_b[l].reshape(1, -1))
    return x
```

```python
import functools
import math

import jax
import jax.numpy as jnp
from jax import lax
from jax.experimental import pallas as pl
from jax.experimental.pallas import tpu as pltpu

F32 = jnp.float32
BF16 = jnp.bfloat16

D_MODEL = 1024
DEPTH = 4
GLA_HEADS, GLA_HEAD_K, GLA_HEAD_V = 4, 64, 128
GLA_DK, GLA_DV = GLA_HEADS * GLA_HEAD_K, GLA_HEADS * GLA_HEAD_V
GLA_GATE_RANK = 16
GLA_GATE_TAU = 16.0
GLA_NORM_EPS = 1e-5
RWKV_HEADS, RWKV_HEAD = 8, 64
RWKV_DIM = RWKV_HEADS * RWKV_HEAD
RWKV_DECAY_RANK, RWKV_ICLR_RANK, RWKV_GATE_RANK = 64, 64, 128
RWKV_IN = 3 * RWKV_DIM + RWKV_DECAY_RANK + RWKV_ICLR_RANK + RWKV_GATE_RANK
RWKV_GN_EPS = 64e-5
ATT_HEADS, ATT_HEAD = 8, 64
ATT_DIM = ATT_HEADS * ATT_HEAD
ATT_BLOCK = 128
DILATIONS = (1, 4, 16)
N_BRANCH = 3
FFN_HIDDEN = 2816
LN_EPS = 1e-5
DEEPNORM_ALPHA = (2 * DEPTH) ** 0.25
GLA_IN = 2 * GLA_DK + GLA_DV + GLA_GATE_RANK + GLA_DV

LANES = 128
CHUNK = 64
VMEM_LIMIT_CAP = 60000 * 1024

NN = ((1,), (0,))
NT = ((1,), (1,))
TN = ((0,), (0,))


def _dg(a, b, dims=NN):
    return lax.dot_general(a, b, (dims, ((), ())), preferred_element_type=F32)


def _dot1(a, b, dims=NN):
    return _dg(a.astype(BF16), b.astype(BF16), dims)


def _split2(a):
    hi = a.astype(BF16)
    lo = (a - hi.astype(F32)).astype(BF16)
    return hi, lo


def _dot3(a, b, dims=NN):
    ah, al = _split2(a)
    bh, bl = _split2(b)
    return _dg(ah, bh, dims) + (_dg(ah, bl, dims) + _dg(al, bh, dims))


def _dot_exact_rhs(a, m_bf16, dims=NN, parts=3):
    acc = None
    rem = a
    for _ in range(parts):
        hi = rem.astype(BF16)
        term = _dg(hi, m_bf16, dims)
        acc = term if acc is None else acc + term
        rem = rem - hi.astype(F32)
    return acc


def _dot_exact_lhs(m_bf16, a, parts=3):
    acc = None
    rem = a
    for _ in range(parts):
        hi = rem.astype(BF16)
        term = _dg(m_bf16, hi)
        acc = term if acc is None else acc + term
        rem = rem - hi.astype(F32)
    return acc


def _sigmoid(x):
    return 1.0 / (1.0 + jnp.exp(-x))


def _silu(x):
    return x * _sigmoid(x)


def _log_sigmoid(x):
    return jnp.minimum(x, 0.0) - jnp.log(1.0 + jnp.exp(-jnp.abs(x)))


def _layer_norm(z, g, b):
    mu = jnp.mean(z, axis=-1, keepdims=True)
    zc = z - mu
    var = jnp.mean(zc * zc, axis=-1, keepdims=True)
    return zc * lax.rsqrt(var + LN_EPS) * g + b


def _stack_heads(x, lane):
    lo = jnp.where(lane < RWKV_HEAD, x, 0.0)
    hi = jnp.where(lane >= RWKV_HEAD, x, 0.0)
    return jnp.concatenate([lo, hi], axis=0)


def _cparams(sem, vmem_bytes):
    return pltpu.CompilerParams(dimension_semantics=sem,
                                vmem_limit_bytes=int(min(vmem_bytes, VMEM_LIMIT_CAP)))


def _const_spec(shape):
    zeros = (0,) * len(shape)
    return pl.BlockSpec(shape, lambda *_: zeros, pipeline_mode=pl.Buffered(1))


def _mod_kernel(c_ref, w_ref, b_ref, o_ref):
    c = c_ref[...]
    s = _silu(c)
    o_ref[0] = jnp.dot(s, w_ref[0], preferred_element_type=F32,
                       precision=lax.Precision.HIGHEST) + b_ref[0]


def _modulation(c, w_ada, b_ada):
    n_l = w_ada.shape[0]
    b = c.shape[0]
    rows = 8
    c_pad = jnp.zeros((rows, D_MODEL), F32).at[:b].set(c)
    tn = 1536
    out = pl.pallas_call(
        _mod_kernel,
        grid=(n_l, 6 * D_MODEL // tn),
        in_specs=[pl.BlockSpec((rows, D_MODEL), lambda l, j: (0, 0)),
                  pl.BlockSpec((1, D_MODEL, tn), lambda l, j: (l, 0, j)),
                  pl.BlockSpec((1, 1, tn), lambda l, j: (l, 0, j))],
        out_specs=pl.BlockSpec((1, rows, tn), lambda l, j: (l, 0, j)),
        out_shape=jax.ShapeDtypeStruct((n_l, rows, 6 * D_MODEL), F32),
        compiler_params=_cparams(("parallel", "parallel"), 4 * D_MODEL * tn * 4),
        name="adaln_mod",
    )(c_pad, w_ada, b_ada.reshape(n_l, 1, 6 * D_MODEL))
    return out[:, :b].reshape(n_l, b, 6, D_MODEL)


def _gla_kernel(x_ref, mod_ref, wm_ref, wga_ref, wal_ref, bal_ref, ng_ref, o_ref,
                st_ref, p_s, cum_s, o_s, *, tile):
    @pl.when(pl.program_id(1) == 0)
    def _():
        st_ref[...] = jnp.zeros_like(st_ref)

    x = x_ref[0]
    u = (x * (1.0 + mod_ref[0, 1:2, :]) + mod_ref[0, 0:1, :]).astype(BF16)
    p_s[...] = _dg(u, wm_ref[...])
    a_lo = _dg(u, wga_ref[...])
    z = _dot3(a_lo, wal_ref[...]) + bal_ref[...]
    log_a = _log_sigmoid(z) * (1.0 / GLA_GATE_TAU)
    ri = lax.broadcasted_iota(jnp.int32, (tile, tile), 0)
    ci = lax.broadcasted_iota(jnp.int32, (tile, tile), 1)
    ltri = jnp.where(((ri >> 6) == (ci >> 6)) & (ci <= ri), 1.0, 0.0).astype(BF16)
    cum_s[...] = _dot_exact_lhs(ltri, log_a)

    lane = lax.broadcasted_iota(jnp.int32, (CHUNK, LANES), 1)
    i2 = lax.broadcasted_iota(jnp.int32, (2 * CHUNK, 2 * CHUNK), 0)
    j2 = lax.broadcasted_iota(jnp.int32, (2 * CHUNK, 2 * CHUNK), 1)
    causal = ((i2 >> 6) == (j2 >> 6)) & (j2 <= i2)
    scale = GLA_HEAD_K ** -0.5

    n_chunks = tile // CHUNK
    n_pairs = GLA_HEADS // 2
    units = [(c, pr) for c in range(n_chunks) for pr in range(n_pairs)]

    qsms, ksms, kdsms, vsts, decs = [], [], [], [], []
    for c, pr in units:
        rows = slice(c * CHUNK, (c + 1) * CHUNK)
        lo = pr * LANES
        cumc = cum_s[rows, lo:lo + LANES]
        qc = p_s[rows, lo:lo + LANES]
        kc = p_s[rows, GLA_DK + lo:GLA_DK + lo + LANES]
        vbase = 2 * GLA_DK + 2 * pr * GLA_HEAD_V
        vsts.append(jnp.concatenate([p_s[rows, vbase:vbase + GLA_HEAD_V],
                                     p_s[rows, vbase + GLA_HEAD_V:vbase + 2 * GLA_HEAD_V]],
                                    axis=0).astype(BF16))
        cl = cumc[CHUNK - 1:CHUNK, :]
        qsms.append(_stack_heads(qc * scale * jnp.exp(cumc), lane).astype(BF16))
        ksms.append(_stack_heads(kc * jnp.exp(-cumc), lane).astype(BF16))
        kdsms.append(_stack_heads(kc * jnp.exp(cl - cumc), lane).astype(BF16))
        decs.append(jnp.exp(cl))
    atts = [jnp.where(causal, _dg(q_, k_, NT), 0.0).astype(BF16) for q_, k_ in zip(qsms, ksms)]
    o_intra = [_dg(a_, v_) for a_, v_ in zip(atts, vsts)]
    d_states = [_dg(v_, kd_, TN) for v_, kd_ in zip(vsts, kdsms)]

    gts = [None] * len(units)
    for pr in range(n_pairs):
        g = st_ref[pr]
        for c in range(n_chunks):
            i = c * n_pairs + pr
            gts[i] = g.astype(BF16)
            g = g * decs[i] + d_states[i]
        st_ref[pr] = g

    for i, (c, pr) in enumerate(units):
        o_st = o_intra[i] + _dg(qsms[i], gts[i], NT)
        rows = slice(c * CHUNK, (c + 1) * CHUNK)
        ob = 2 * pr * GLA_HEAD_V
        o_s[rows, ob:ob + GLA_HEAD_V] = o_st[:CHUNK]
        o_s[rows, ob + GLA_HEAD_V:ob + 2 * GLA_HEAD_V] = o_st[CHUNK:]

    og_base = 2 * GLA_DK + GLA_DV
    for h in range(GLA_HEADS):
        sl = slice(h * GLA_HEAD_V, (h + 1) * GLA_HEAD_V)
        oh = o_s[:, sl]
        on = oh * lax.rsqrt(jnp.mean(oh * oh, axis=-1, keepdims=True) + GLA_NORM_EPS) * ng_ref[...]
        og = p_s[:, og_base + h * GLA_HEAD_V:og_base + (h + 1) * GLA_HEAD_V]
        o_ref[0, :, sl] = on * _silu(og)


def _gla_layer(x, mod, wm, wga, wal, bal, ng, tile=512):
    b, s, _ = x.shape
    n_main = 2 * GLA_DK + 2 * GLA_DV
    vmem = (4 * tile * D_MODEL * 4 + 4 * tile * GLA_DV * 4 + D_MODEL * (n_main + LANES) * 2
            + tile * (n_main + GLA_DK + GLA_DV) * 4 + 6 * tile * n_main * 4 + (8 << 20))
    return pl.pallas_call(
        functools.partial(_gla_kernel, tile=tile),
        grid=(b, s // tile),
        in_specs=[pl.BlockSpec((1, tile, D_MODEL), lambda i, j: (i, j, 0)),
                  pl.BlockSpec((1, 6, D_MODEL), lambda i, j: (i, 0, 0)),
                  _const_spec((D_MODEL, n_main)),
                  _const_spec((D_MODEL, LANES)),
                  _const_spec((LANES, GLA_DK)),
                  _const_spec((1, GLA_DK)),
                  _const_spec((1, GLA_HEAD_V))],
        out_specs=pl.BlockSpec((1, tile, GLA_DV), lambda i, j: (i, j, 0)),
        out_shape=jax.ShapeDtypeStruct((b, s, GLA_DV), F32),
        scratch_shapes=[pltpu.VMEM((GLA_HEADS // 2, GLA_HEAD_V, LANES), F32),
                        pltpu.VMEM((tile, n_main), F32),
                        pltpu.VMEM((tile, GLA_DK), F32),
                        pltpu.VMEM((tile, GLA_DV), F32)],
        compiler_params=_cparams(("parallel", "arbitrary"), vmem),
        name="gla_mixer",
    )(x, mod, wm, wga, wal, bal, ng)


def _inv_unit_lower(ns, eye, m16, m32, m64):
    ds = [jnp.where(m16, n, 0.0) for n in ns]
    xs = [eye + d for d in ds]
    pws = [d.astype(BF16) for d in ds]
    for level in range(3):
        pws = [_dg(p, p) for p in pws]
        pws = [p.astype(BF16) for p in pws]
        xs = [x + _dg(x.astype(BF16), p) for x, p in zip(xs, pws)]
    for m in (m32, m64):
        xbs = [x.astype(BF16) for x in xs]
        xos = [_dg(xb, jnp.where(m, n, 0.0).astype(BF16)) for xb, n in zip(xbs, ns)]
        xs = [x + _dg(xo.astype(BF16), xb) for x, xo, xb in zip(xs, xos, xbs)]
    return xs


def _rwkv_kernel(x_ref, mod_ref, w_ref, mu_ref, w0_ref, wup_ref, a0_ref, aup_ref, gup_ref,
                 kk_ref, ka_ref, rk_ref, gng_ref, gnb_ref, o_ref,
                 st_ref, carry_ref, a_s, b_s, k_s, r_s, v_s, bb_s, k2_s, cum_s, y_s, bon_s, g_s,
                 *, tile):
    @pl.when(pl.program_id(1) == 0)
    def _():
        st_ref[...] = jnp.zeros_like(st_ref)
        carry_ref[...] = jnp.zeros_like(carry_ref)

    x = x_ref[0]
    u = (x * (1.0 + mod_ref[0, 1:2, :]) + mod_ref[0, 0:1, :]).astype(BF16)
    p = _dg(u, w_ref[...])
    row = lax.broadcasted_iota(jnp.int32, (tile, 1), 0)
    prev = jnp.where(row == 0, carry_ref[0:1, :], pltpu.roll(p, 1, 0))
    carry_ref[0:1, :] = p[tile - 1:tile, :]
    ps = p + (prev - p) * mu_ref[...]

    d = RWKV_DIM
    r = ps[:, 0:d]
    k = ps[:, d:2 * d]
    v = ps[:, 2 * d:3 * d]
    wa_lo = ps[:, 3 * d:3 * d + LANES]
    g_lo = ps[:, 3 * d + LANES:3 * d + 2 * LANES]
    wl = w0_ref[...] + _dot3(jnp.tanh(wa_lo), wup_ref[...])
    lw = -_sigmoid(wl) * math.exp(-0.5)
    a = _sigmoid(a0_ref[...] + _dot3(wa_lo, aup_ref[...]))
    g_s[...] = _dot3(_sigmoid(g_lo), gup_ref[...])
    kk = k * kk_ref[...]
    k2 = k * (1.0 + (a - 1.0) * ka_ref[...])

    bi = lax.broadcasted_iota(jnp.int32, (LANES, LANES), 0)
    bj = lax.broadcasted_iota(jnp.int32, (LANES, LANES), 1)
    same64 = (bi >> 6) == (bj >> 6)
    seg = jnp.where(same64, 1.0, 0.0).astype(BF16)

    def seg_sum(t):
        return jnp.concatenate(
            [_dot_exact_rhs(t[:, q * LANES:(q + 1) * LANES], seg) for q in range(d // LANES)],
            axis=1)

    kk = kk / jnp.maximum(jnp.sqrt(seg_sum(kk * kk)), 1e-12)
    bb = kk * a
    bon_s[...] = seg_sum(r * k2 * rk_ref[...]) * v

    ri = lax.broadcasted_iota(jnp.int32, (tile, tile), 0)
    ci = lax.broadcasted_iota(jnp.int32, (tile, tile), 1)
    ltri = jnp.where(((ri >> 6) == (ci >> 6)) & (ci <= ri), 1.0, 0.0).astype(BF16)
    cum = _dot_exact_lhs(ltri, lw)
    e_neg = jnp.exp(-cum)
    a_s[...] = (-kk * jnp.exp(cum - lw)).astype(BF16)
    b_s[...] = (bb * e_neg).astype(BF16)
    k_s[...] = (k2 * e_neg).astype(BF16)
    r_s[...] = (r * jnp.exp(cum)).astype(BF16)
    v_s[...] = v.astype(BF16)
    bb_s[...] = bb
    k2_s[...] = k2
    cum_s[...] = cum

    lane = lax.broadcasted_iota(jnp.int32, (CHUNK, LANES), 1)
    strict = same64 & (bj < bi)
    incl = same64 & (bj <= bi)
    m16 = (bi >> 4) == (bj >> 4)
    m32 = ((bi >> 5) == (bj >> 5)) & jnp.logical_not(m16)
    m64 = same64 & ((bi >> 5) != (bj >> 5))
    eye = jnp.where(bi == bj, 1.0, 0.0)
    h2 = 2 * CHUNK

    n_pairs = RWKV_HEADS // 2
    units = [(c, pr) for c in range(tile // CHUNK) for pr in range(n_pairs)]

    def rows_of(c):
        return slice(c * CHUNK, (c + 1) * CHUNK)

    def lanes_of(pr):
        return slice(pr * LANES, (pr + 1) * LANES)

    stacked = {}
    for name, ref in (("a", a_s), ("b", b_s), ("k", k_s), ("r", r_s), ("v", v_s)):
        stacked[name] = [_stack_heads(ref[rows_of(c), lanes_of(pr)], lane) for c, pr in units]
    scs = [_dg(jnp.concatenate([a_, r_], axis=0), jnp.concatenate([b_, k_], axis=0), NT)
           for a_, r_, b_, k_ in zip(stacked["a"], stacked["r"], stacked["b"], stacked["k"])]
    abs_ = [jnp.where(strict, sc[:h2, :h2], 0.0) for sc in scs]
    akvs = [_dg(jnp.where(strict, sc[:h2, h2:], 0.0).astype(BF16), v_)
            for sc, v_ in zip(scs, stacked["v"])]
    rbks = [jnp.concatenate([jnp.where(incl, sc[h2:, :h2], 0.0),
                             jnp.where(incl, sc[h2:, h2:], 0.0)], axis=1).astype(BF16)
            for sc in scs]
    tinvs = _inv_unit_lower(abs_, eye, m16, m32, m64)
    wus = [_dg(t.astype(BF16), jnp.concatenate([a_, akv.astype(BF16)], axis=1))
           for t, a_, akv in zip(tinvs, stacked["a"], akvs)]

    for c in range(tile // CHUNK):
        idx = [c * n_pairs + pr for pr in range(n_pairs)]
        gts = [st_ref[pr] for pr in range(n_pairs)]
        gtbs = [g_.astype(BF16) for g_ in gts]
        ums = [(_dg(wus[i][:, :LANES].astype(BF16), gb, NT) + wus[i][:, LANES:]).astype(BF16)
               for i, gb in zip(idx, gtbs)]
        uvs = [jnp.concatenate([um, stacked["v"][i]], axis=0) for i, um in zip(idx, ums)]
        ys = [_dg(stacked["r"][i], gb, NT) + _dg(rbks[i], uv) for i, gb, uv in zip(idx, gtbs, uvs)]
        for pr in range(n_pairs):
            cumc = cum_s[rows_of(c), lanes_of(pr)]
            cl = cumc[CHUNK - 1:CHUNK, :]
            dec = jnp.exp(cl - cumc)
            bkd = jnp.concatenate([_stack_heads(bb_s[rows_of(c), lanes_of(pr)] * dec, lane),
                                   _stack_heads(k2_s[rows_of(c), lanes_of(pr)] * dec, lane)],
                                  axis=0).astype(BF16)
            st_ref[pr] = gts[pr] * jnp.exp(cl) + _dg(uvs[pr], bkd, TN)
            y_s[rows_of(c), lanes_of(pr)] = ys[pr][:CHUNK] + ys[pr][CHUNK:]

    y = y_s[...]
    inv_n = 1.0 / RWKV_HEAD
    mu_h = seg_sum(y) * inv_n
    yc = y - mu_h
    var = seg_sum(yc * yc) * inv_n
    yn = yc * lax.rsqrt(var + RWKV_GN_EPS) * gng_ref[...] + gnb_ref[...]
    o_ref[0] = (yn + bon_s[...]) * g_s[...]


def _rwkv_layer(x, mod, w, mu, w0, wup, a0, aup, gup, k_k, k_a, r_k, gn_g, gn_b, tile=256):
    b, s, _ = x.shape
    d = RWKV_DIM
    vec = lambda: _const_spec((1, d))
    vmem = (4 * tile * D_MODEL * 4 + 4 * tile * d * 4 + D_MODEL * RWKV_IN * 2
            + 11 * tile * d * 4 + 8 * tile * RWKV_IN * 4 + (12 << 20))
    return pl.pallas_call(
        functools.partial(_rwkv_kernel, tile=tile),
        grid=(b, s // tile),
        in_specs=[pl.BlockSpec((1, tile, D_MODEL), lambda i, j: (i, j, 0)),
                  pl.BlockSpec((1, 6, D_MODEL), lambda i, j: (i, 0, 0)),
                  _const_spec((D_MODEL, RWKV_IN)),
                  _const_spec((1, RWKV_IN)),
                  vec(), _const_spec((LANES, d)), vec(), _const_spec((LANES, d)),
                  _const_spec((LANES, d)), vec(), vec(), vec(), vec(), vec()],
        out_specs=pl.BlockSpec((1, tile, d), lambda i, j: (i, j, 0)),
        out_shape=jax.ShapeDtypeStruct((b, s, d), F32),
        scratch_shapes=[pltpu.VMEM((RWKV_HEADS // 2, LANES, LANES), F32),
                        pltpu.VMEM((8, RWKV_IN), F32)]
                       + [pltpu.VMEM((tile, d), BF16) for _ in range(5)]
                       + [pltpu.VMEM((tile, d), F32) for _ in range(6)],
        compiler_params=_cparams(("parallel", "arbitrary"), vmem),
        name="rwkv7_mixer",
    )(x, mod, w, mu, w0, wup, a0, aup, gup, k_k, k_a, r_k, gn_g, gn_b)


def _qkv_kernel(x_ref, mod_ref, w_ref, *refs, tile):
    out_refs, p_s = refs[:-1], refs[-1]
    x = x_ref[0]
    u = (x * (1.0 + mod_ref[0, 1:2, :]) + mod_ref[0, 0:1, :]).astype(BF16)
    p = _dg(u, w_ref[...])
    n_grp = 3 * ATT_DIM // LANES
    for g in range(n_grp):
        p_s[g] = p[:, g * LANES:(g + 1) * LANES]
    per = ATT_DIM // LANES
    for pi, dil in enumerate(DILATIONS):
        q_ref, k_ref, v_ref = out_refs[3 * pi:3 * pi + 3]
        for r in range(dil):
            rows = pl.ds(r, tile // dil, stride=dil) if dil > 1 else slice(None)
            grp = lambda t: jnp.concatenate([p_s[t * per + g, rows, :] for g in range(per)], axis=1)
            q_ref[0, r] = (grp(0) * ATT_HEAD ** -0.5).astype(BF16)
            k_ref[0, r] = grp(1).astype(BF16)
            v_ref[0, r] = grp(2).astype(BF16)


def _qkv_layer(x, mod, w, tile=512):
    b, s, _ = x.shape
    out_shape, out_specs = [], []
    for dil in DILATIONS:
        for _ in range(3):
            out_shape.append(jax.ShapeDtypeStruct((b, dil, s // dil, ATT_DIM), BF16))
            out_specs.append(pl.BlockSpec((1, dil, tile // dil, ATT_DIM), lambda i, j: (i, 0, j, 0)))
    vmem = (4 * tile * D_MODEL * 4 + D_MODEL * 3 * ATT_DIM * 2 + 4 * tile * 3 * ATT_DIM * 4
            + 2 * 9 * tile * ATT_DIM * 2 + (4 << 20))
    outs = pl.pallas_call(
        functools.partial(_qkv_kernel, tile=tile),
        grid=(b, s // tile),
        in_specs=[pl.BlockSpec((1, tile, D_MODEL), lambda i, j: (i, j, 0)),
                  pl.BlockSpec((1, 6, D_MODEL), lambda i, j: (i, 0, 0)),
                  _const_spec((D_MODEL, 3 * ATT_DIM))],
        out_specs=out_specs,
        out_shape=out_shape,
        scratch_shapes=[pltpu.VMEM((3 * ATT_DIM // LANES, tile, LANES), F32)],
        compiler_params=_cparams(("parallel", "parallel"), vmem),
        name="att_qkv",
    )(x, mod, w)
    return [outs[3 * pi:3 * pi + 3] for pi in range(len(DILATIONS))]


ATT_SPAN = 2048


def _att_kernel(q_ref, kp_ref, kc_ref, vp_ref, vc_ref, o_ref, l_ref, kbuf, vbuf, *, dilation):
    n_sub = ATT_SPAN // dilation
    n_blk = n_sub // ATT_BLOCK
    span = pl.program_id(1)
    kbuf[:, :ATT_BLOCK] = kp_ref[0]
    kbuf[:, ATT_BLOCK:] = kc_ref[0]
    vbuf[:, :ATT_BLOCK] = vp_ref[0]
    vbuf[:, ATT_BLOCK:] = vc_ref[0]

    qi = lax.broadcasted_iota(jnp.int32, (ATT_BLOCK, 2 * ATT_BLOCK), 0)
    kj = lax.broadcasted_iota(jnp.int32, (ATT_BLOCK, 2 * ATT_BLOCK), 1)
    steps = qi + ATT_BLOCK - kj
    window = (steps >= 0) & (steps <= ATT_BLOCK)
    dist = (steps * dilation).astype(F32)
    lane_q = lax.broadcasted_iota(jnp.int32, (ATT_BLOCK, LANES), 1)
    zero = jnp.zeros((), BF16)
    heads = range(ATT_HEADS)
    biases = [jnp.where(window, dist * -(2.0 ** (-8.0 * (h + 1) / ATT_HEADS)), -jnp.inf)
              for h in heads]

    def unit(uidx, carry):
        r = uidx >> (n_blk.bit_length() - 1)
        n = uidx & (n_blk - 1)
        row0 = pl.multiple_of(n * ATT_BLOCK, ATT_BLOCK)
        q = q_ref[0, r, pl.ds(row0, ATT_BLOCK), :]
        kk = kbuf[r, pl.ds(row0, 2 * ATT_BLOCK), :]
        vv = vbuf[r, pl.ds(row0, 2 * ATT_BLOCK), :]
        first_key = jnp.where((span == 0) & (n == 0), ATT_BLOCK, 0)
        head_mask = jnp.where(kj >= first_key, 0.0, -jnp.inf)
        pair_of = lambda t, h: t[:, (h // 2) * LANES:(h // 2 + 1) * LANES]
        sel = lambda h: (lane_q >= ATT_HEAD) if h % 2 else (lane_q < ATT_HEAD)
        ss = [_dg(jnp.where(sel(h), pair_of(q, h), zero), pair_of(kk, h), NT) for h in heads]
        ss = [s + (biases[h] + head_mask) for h, s in zip(heads, ss)]
        ms = [jnp.max(s, axis=-1, keepdims=True) for s in ss]
        es = [jnp.exp(s - m) for s, m in zip(ss, ms)]
        dens = [jnp.sum(e, axis=-1, keepdims=True) for e in es]
        pvs = [_dg(e.astype(BF16), pair_of(vv, h)) for h, e in zip(heads, es)]
        outs = [pv / den for pv, den in zip(pvs, dens)]
        lse_blk = jnp.zeros((ATT_BLOCK, LANES), F32)
        for h in heads:
            lse_blk = jnp.where(lane_q == h, ms[h] + jnp.log(dens[h]), lse_blk)
        tok0 = n * (ATT_BLOCK * dilation) + r
        if dilation == 1:
            rows = pl.ds(pl.multiple_of(tok0, ATT_BLOCK), ATT_BLOCK)
        else:
            rows = pl.ds(tok0, ATT_BLOCK, stride=dilation)
        for p in range(ATT_HEADS // 2):
            o_ref[0, p, rows, :] = jnp.where(lane_q < ATT_HEAD, outs[2 * p], outs[2 * p + 1])
        l_ref[0, rows, :] = lse_blk
        return carry

    lax.fori_loop(0, dilation * n_blk, unit, 0)


def _att_pattern(q, k, v, dilation):
    b, _, n_res, _ = q.shape
    s = dilation * n_res
    n_sub = ATT_SPAN // dilation
    n_blk = n_sub // ATT_BLOCK
    cur = pl.BlockSpec((1, dilation, n_sub, ATT_DIM), lambda i, j: (i, 0, j, 0))
    prev = pl.BlockSpec((1, dilation, ATT_BLOCK, ATT_DIM),
                        lambda i, j: (i, 0, jnp.maximum(j * n_blk - 1, 0), 0))
    buf = pltpu.VMEM((dilation, ATT_BLOCK + n_sub, ATT_DIM), BF16)
    vmem = (2 * (3 * ATT_SPAN + 2 * dilation * ATT_BLOCK) * ATT_DIM * 2
            + 2 * dilation * (ATT_BLOCK + n_sub) * ATT_DIM * 2
            + 2 * ATT_SPAN * (ATT_DIM + LANES) * 4 + (8 << 20))
    return pl.pallas_call(
        functools.partial(_att_kernel, dilation=dilation),
        grid=(b, s // ATT_SPAN),
        in_specs=[cur, prev, cur, prev, cur],
        out_specs=[pl.BlockSpec((1, ATT_HEADS // 2, ATT_SPAN, LANES), lambda i, j: (i, 0, j, 0)),
                   pl.BlockSpec((1, ATT_SPAN, LANES), lambda i, j: (i, j, 0))],
        out_shape=[jax.ShapeDtypeStruct((b, ATT_HEADS // 2, s, LANES), F32),
                   jax.ShapeDtypeStruct((b, s, LANES), F32)],
        scratch_shapes=[buf, buf],
        compiler_params=_cparams(("parallel", "parallel"), vmem),
        name=f"dilated_att_d{dilation}",
    )(q, k, k, v, v)


def _merge_kernel(x_ref, mod_ref, oa_ref, ob_ref, o1_ref, o4_ref, o16_ref, l1_ref, l4_ref, l16_ref,
                  wg_ref, wb_ref, wo_ref, g_ref, b_ref, out_ref):
    x = x_ref[0]
    u = (x * (1.0 + mod_ref[0, 1:2, :]) + mod_ref[0, 0:1, :]).astype(BF16)
    gates = _sigmoid(_dg(u, wg_ref[...]))
    l1, l4, l16 = l1_ref[0], l4_ref[0], l16_ref[0]
    m = jnp.maximum(jnp.maximum(l1, l4), l16)
    e1, e4, e16 = jnp.exp(l1 - m), jnp.exp(l4 - m), jnp.exp(l16 - m)
    inv = 1.0 / (e1 + e4 + e16)
    hi = lax.broadcasted_iota(jnp.int32, (LANES, ATT_DIM), 0)
    hj = lax.broadcasted_iota(jnp.int32, (LANES, ATT_DIM), 1)
    expand = jnp.where(hi == (hj >> 6), 1.0, 0.0).astype(BF16)
    lane_cat = lambda ref: jnp.concatenate([ref[0, p] for p in range(ATT_HEADS // 2)], axis=1)
    o_c = (_dot_exact_rhs(e1 * inv, expand, parts=2) * lane_cat(o1_ref)
           + _dot_exact_rhs(e4 * inv, expand, parts=2) * lane_cat(o4_ref)
           + _dot_exact_rhs(e16 * inv, expand, parts=2) * lane_cat(o16_ref))
    merged = (gates[:, :D_MODEL] * _dot1(oa_ref[0], wb_ref[0])
              + gates[:, D_MODEL:2 * D_MODEL] * _dot1(ob_ref[0], wb_ref[1])
              + gates[:, 2 * D_MODEL:] * _dot1(o_c, wb_ref[2]))
    h = _dot1(merged, wo_ref[...])
    z = DEEPNORM_ALPHA * x + (1.0 + mod_ref[0, 2:3, :]) * h
    out_ref[0] = _layer_norm(z, g_ref[...], b_ref[...])


def _merge_layer(x, mod, oa, ob, oc, lses, wg, wb, wo, g, bta, tile=512):
    b, s, _ = x.shape
    xs = pl.BlockSpec((1, tile, D_MODEL), lambda i, j: (i, j, 0))
    bs = pl.BlockSpec((1, tile, ATT_DIM), lambda i, j: (i, j, 0))
    ls = pl.BlockSpec((1, tile, LANES), lambda i, j: (i, j, 0))
    cs = pl.BlockSpec((1, ATT_HEADS // 2, tile, LANES), lambda i, j: (i, 0, j, 0))
    vmem = (4 * tile * D_MODEL * 4 + 10 * tile * ATT_DIM * 4 + 6 * tile * LANES * 4
            + (3 * D_MODEL * D_MODEL + 3 * ATT_DIM * D_MODEL + D_MODEL * D_MODEL) * 2
            + 6 * tile * 3 * D_MODEL * 4 + (4 << 20))
    return pl.pallas_call(
        _merge_kernel,
        grid=(b, s // tile),
        in_specs=[xs, pl.BlockSpec((1, 6, D_MODEL), lambda i, j: (i, 0, 0)),
                  bs, bs, cs, cs, cs, ls, ls, ls,
                  _const_spec((D_MODEL, N_BRANCH * D_MODEL)),
                  _const_spec((N_BRANCH, ATT_DIM, D_MODEL)),
                  _const_spec((D_MODEL, D_MODEL)),
                  _const_spec((1, D_MODEL)), _const_spec((1, D_MODEL))],
        out_specs=xs,
        out_shape=jax.ShapeDtypeStruct((b, s, D_MODEL), F32),
        compiler_params=_cparams(("parallel", "parallel"), vmem),
        name="merge_ln1",
    )(x, mod, oa, ob, oc[0], oc[1], oc[2], lses[0], lses[1], lses[2], wg, wb, wo, g, bta)


def _ffn_kernel(x_ref, mod_ref, w1_ref, w2_ref, g_ref, b_ref, out_ref):
    x = x_ref[0]
    u = (x * (1.0 + mod_ref[0, 4:5, :]) + mod_ref[0, 3:4, :]).astype(BF16)
    h = _dg(u, w1_ref[...])
    act = (_silu(h[:, :FFN_HIDDEN]) * h[:, FFN_HIDDEN:]).astype(BF16)
    y = _dg(act, w2_ref[...])
    z = DEEPNORM_ALPHA * x + (1.0 + mod_ref[0, 5:6, :]) * y
    out_ref[0] = _layer_norm(z, g_ref[...], b_ref[...])


def _ffn_layer(x, mod, w1, w2, g, bta, tile=512):
    b, s, _ = x.shape
    xs = pl.BlockSpec((1, tile, D_MODEL), lambda i, j: (i, j, 0))
    vmem = (4 * tile * D_MODEL * 4 + 3 * D_MODEL * FFN_HIDDEN * 2
            + 4 * tile * 2 * FFN_HIDDEN * 4 + (4 << 20))
    return pl.pallas_call(
        _ffn_kernel,
        grid=(b, s // tile),
        in_specs=[xs, pl.BlockSpec((1, 6, D_MODEL), lambda i, j: (i, 0, 0)),
                  _const_spec((D_MODEL, 2 * FFN_HIDDEN)),
                  _const_spec((FFN_HIDDEN, D_MODEL)),
                  _const_spec((1, D_MODEL)), _const_spec((1, D_MODEL))],
        out_specs=xs,
        out_shape=jax.ShapeDtypeStruct((b, s, D_MODEL), F32),
        compiler_params=_cparams(("parallel", "parallel"), vmem),
        name="ffn_ln2",
    )(x, mod, w1, w2, g, bta)


def _pad_rows(m, rows, offset=0):
    out = jnp.zeros((rows,) + m.shape[1:], m.dtype)
    return out.at[offset:offset + m.shape[0]].set(m)


def _mixer_layer(x, mod, w_in, gla_w_alpha, gla_b_alpha, gla_norm_g, rwkv_mu, rwkv_w0, rwkv_w_up,
                 rwkv_a0, rwkv_a_up, rwkv_g_up, rwkv_k_k, rwkv_k_a, rwkv_r_k, rwkv_gn_g,
                 rwkv_gn_b, w_branch, w_out, ln_g, ln_b):
    row = lambda t: t.reshape(1, -1)
    o0 = 0
    gq_end = 2 * GLA_DK + GLA_DV
    w_gla = jnp.concatenate([w_in[:, :gq_end], w_in[:, gq_end + GLA_GATE_RANK:GLA_IN]],
                            axis=1).astype(BF16)
    w_ga = jnp.zeros((D_MODEL, LANES), F32).at[:, :GLA_GATE_RANK].set(
        w_in[:, gq_end:gq_end + GLA_GATE_RANK]).astype(BF16)
    o0 += GLA_IN
    w_rwkv = w_in[:, o0:o0 + RWKV_IN].astype(BF16)
    o0 += RWKV_IN
    w_att = w_in[:, o0:o0 + 3 * ATT_DIM].astype(BF16)
    o0 += 3 * ATT_DIM
    w_gate = w_in[:, o0:].astype(BF16)

    o_a = _gla_layer(x, mod, w_gla, w_ga, _pad_rows(gla_w_alpha, LANES), row(gla_b_alpha),
                     row(gla_norm_g))
    o_b = _rwkv_layer(x, mod, w_rwkv, row(rwkv_mu), row(rwkv_w0),
                      _pad_rows(rwkv_w_up, LANES), row(rwkv_a0),
                      _pad_rows(rwkv_a_up, LANES, RWKV_DECAY_RANK), rwkv_g_up,
                      row(rwkv_k_k), row(rwkv_k_a), row(rwkv_r_k), row(rwkv_gn_g),
                      row(rwkv_gn_b))
    qkvs = _qkv_layer(x, mod, w_att)
    res = [_att_pattern(*qkv, dil) for qkv, dil in zip(qkvs, DILATIONS)]
    return _merge_layer(x, mod, o_a, o_b, [o for o, _ in res], [l for _, l in res],
                        w_gate, w_branch.astype(BF16), w_out.astype(BF16), row(ln_g), row(ln_b))


def kernel(x, c, w_ada, b_ada, w_in, gla_w_alpha, gla_b_alpha, gla_norm_g, rwkv_mu, rwkv_w0,
           rwkv_w_up, rwkv_a0, rwkv_a_up, rwkv_g_up, rwkv_k_k, rwkv_k_a, rwkv_r_k, rwkv_gn_g,
           rwkv_gn_b, w_branch, w_out, ln1_g, ln1_b, ffn_w1, ffn_w2, ln2_g, ln2_b):
    mod = _modulation(c, w_ada, b_ada)
    for l in range(DEPTH):
        x = _mixer_layer(x, mod[l], w_in[l], gla_w_alpha[l], gla_b_alpha[l], gla_norm_g[l],
                         rwkv_mu[l], rwkv_w0[l], rwkv_w_up[l], rwkv_a0[l], rwkv_a_up[l],
                         rwkv_g_up[l], rwkv_k_k[l], rwkv_k_a[l], rwkv_r_k[l], rwkv_gn_g[l],
                         rwkv_gn_b[l], w_branch[l], w_out[l], ln1_g[l], ln1_b[l])
        x = _ffn_layer(x, mod[l], ffn_w1[l].astype(BF16), ffn_w2[l].astype(BF16),
                       ln2_g[l].reshape(1, -1), ln2_b[l].reshape(1, -1))
    return x
```

```python
import functools
import math

import jax
import jax.numpy as jnp
from jax import lax
from jax.experimental import pallas as pl
from jax.experimental.pallas import tpu as pltpu

F32 = jnp.float32
BF16 = jnp.bfloat16

D_MODEL = 1024
DEPTH = 4
GLA_HEADS, GLA_HEAD_K, GLA_HEAD_V = 4, 64, 128
GLA_DK, GLA_DV = GLA_HEADS * GLA_HEAD_K, GLA_HEADS * GLA_HEAD_V
GLA_GATE_RANK = 16
GLA_GATE_TAU = 16.0
GLA_NORM_EPS = 1e-5
RWKV_HEADS, RWKV_HEAD = 8, 64
RWKV_DIM = RWKV_HEADS * RWKV_HEAD
RWKV_DECAY_RANK, RWKV_ICLR_RANK, RWKV_GATE_RANK = 64, 64, 128
RWKV_IN = 3 * RWKV_DIM + RWKV_DECAY_RANK + RWKV_ICLR_RANK + RWKV_GATE_RANK
RWKV_GN_EPS = 64e-5
ATT_HEADS, ATT_HEAD = 8, 64
ATT_DIM = ATT_HEADS * ATT_HEAD
ATT_BLOCK = 128
DILATIONS = (1, 4, 16)
N_BRANCH = 3
FFN_HIDDEN = 2816
LN_EPS = 1e-5
DEEPNORM_ALPHA = (2 * DEPTH) ** 0.25
GLA_IN = 2 * GLA_DK + GLA_DV + GLA_GATE_RANK + GLA_DV

LOG2_E = math.log2(math.e)
LN_2 = math.log(2.0)
LANES = 128
CHUNK = 64
VMEM_LIMIT_CAP = 60000 * 1024

NN = ((1,), (0,))
NT = ((1,), (1,))
TN = ((0,), (0,))


def _dg(a, b, dims=NN):
    return lax.dot_general(a, b, (dims, ((), ())), preferred_element_type=F32)


def _dot1(a, b, dims=NN):
    return _dg(a.astype(BF16), b.astype(BF16), dims)


def _split2(a):
    hi = a.astype(BF16)
    lo = (a - hi.astype(F32)).astype(BF16)
    return hi, lo


def _dot3(a, b, dims=NN):
    ah, al = _split2(a)
    bh, bl = _split2(b)
    return _dg(ah, bh, dims) + (_dg(ah, bl, dims) + _dg(al, bh, dims))


def _dot_exact_rhs(a, m_bf16, dims=NN, parts=2):
    acc = None
    rem = a
    for _ in range(parts):
        hi = rem.astype(BF16)
        term = _dg(hi, m_bf16, dims)
        acc = term if acc is None else acc + term
        rem = rem - hi.astype(F32)
    return acc


def _dot_exact_lhs(m_bf16, a, parts=2):
    acc = None
    rem = a
    for _ in range(parts):
        hi = rem.astype(BF16)
        term = _dg(m_bf16, hi)
        acc = term if acc is None else acc + term
        rem = rem - hi.astype(F32)
    return acc


def _sigmoid(x):
    return 1.0 / (1.0 + jnp.exp(-x))


def _silu(x):
    return x * _sigmoid(x)


def _log_sigmoid(x):
    return jnp.minimum(x, 0.0) - jnp.log(1.0 + jnp.exp(-jnp.abs(x)))


def _layer_norm(z, g, b):
    mu = jnp.mean(z, axis=-1, keepdims=True)
    zc = z - mu
    var = jnp.mean(zc * zc, axis=-1, keepdims=True)
    return zc * lax.rsqrt(var + LN_EPS) * g + b


def _stack_heads(x, lane):
    lo = jnp.where(lane < RWKV_HEAD, x, 0.0)
    hi = jnp.where(lane >= RWKV_HEAD, x, 0.0)
    return jnp.concatenate([lo, hi], axis=0)


def _cparams(sem, vmem_bytes):
    return pltpu.CompilerParams(dimension_semantics=sem,
                                vmem_limit_bytes=int(min(vmem_bytes, VMEM_LIMIT_CAP)))


def _const_spec(shape):
    zeros = (0,) * len(shape)
    return pl.BlockSpec(shape, lambda *_: zeros, pipeline_mode=pl.Buffered(1))


def _mod_kernel(ct_ref, w_ref, b_ref, o_ref, *, batch):
    s = _silu(ct_ref[...])
    tn = w_ref.shape[2]
    o_ref[0] = jnp.zeros(o_ref.shape[1:], F32)
    for bi in range(batch):
        sb = jnp.broadcast_to(s[:, bi:bi + 1], (D_MODEL, LANES))
        for j in range(tn // LANES):
            sl = slice(j * LANES, (j + 1) * LANES)
            acc = jnp.sum(w_ref[0, :, sl] * sb, axis=0, keepdims=True)
            o_ref[0, bi:bi + 1, sl] = acc + b_ref[0, :, sl]


def _modulation(c, w_ada, b_ada):
    n_l = w_ada.shape[0]
    b = c.shape[0]
    rows = 8
    assert b <= rows
    c_t = jnp.zeros((D_MODEL, rows), F32).at[:, :b].set(c.T)
    tn = 1536
    out = pl.pallas_call(
        functools.partial(_mod_kernel, batch=b),
        grid=(n_l, 6 * D_MODEL // tn),
        in_specs=[pl.BlockSpec((D_MODEL, rows), lambda l, j: (0, 0)),
                  pl.BlockSpec((1, D_MODEL, tn), lambda l, j: (l, 0, j)),
                  pl.BlockSpec((1, 1, tn), lambda l, j: (l, 0, j))],
        out_specs=pl.BlockSpec((1, rows, tn), lambda l, j: (l, 0, j)),
        out_shape=jax.ShapeDtypeStruct((n_l, rows, 6 * D_MODEL), F32),
        compiler_params=_cparams(("parallel", "parallel"), 4 * D_MODEL * tn * 4),
        name="adaln_mod",
    )(c_t, w_ada, b_ada.reshape(n_l, 1, 6 * D_MODEL))
    return out[:, :b].reshape(n_l, b, 6, D_MODEL)


def _gla_kernel(x_ref, mod_ref, wm_ref, wga_ref, wal_ref, bal_ref, ng_ref, o_ref,
                st_ref, p_s, cum_s, o_s, *, tile):
    @pl.when(pl.program_id(1) == 0)
    def _():
        st_ref[...] = jnp.zeros_like(st_ref)

    x = x_ref[0]
    u = (x * (1.0 + mod_ref[0, 1:2, :]) + mod_ref[0, 0:1, :]).astype(BF16)
    p_s[...] = _dg(u, wm_ref[...])
    a_lo = _dg(u, wga_ref[...])
    z = _dot3(a_lo, wal_ref[...]) + bal_ref[...]
    log_a = _log_sigmoid(z) * (1.0 / GLA_GATE_TAU)
    ri = lax.broadcasted_iota(jnp.int32, (tile, tile), 0)
    ci = lax.broadcasted_iota(jnp.int32, (tile, tile), 1)
    ltri = jnp.where(((ri >> 6) == (ci >> 6)) & (ci <= ri), 1.0, 0.0).astype(BF16)
    cum_s[...] = _dot_exact_lhs(ltri, log_a)

    lane = lax.broadcasted_iota(jnp.int32, (CHUNK, LANES), 1)
    i2 = lax.broadcasted_iota(jnp.int32, (2 * CHUNK, 2 * CHUNK), 0)
    j2 = lax.broadcasted_iota(jnp.int32, (2 * CHUNK, 2 * CHUNK), 1)
    causal = ((i2 >> 6) == (j2 >> 6)) & (j2 <= i2)
    scale = GLA_HEAD_K ** -0.5

    n_chunks = tile // CHUNK
    n_pairs = GLA_HEADS // 2
    units = [(c, pr) for c in range(n_chunks) for pr in range(n_pairs)]

    qsms, ksms, kdsms, vsts, decs = [], [], [], [], []
    for c, pr in units:
        rows = slice(c * CHUNK, (c + 1) * CHUNK)
        lo = pr * LANES
        cumc = cum_s[rows, lo:lo + LANES]
        qc = p_s[rows, lo:lo + LANES]
        kc = p_s[rows, GLA_DK + lo:GLA_DK + lo + LANES]
        vbase = 2 * GLA_DK + 2 * pr * GLA_HEAD_V
        vsts.append(jnp.concatenate([p_s[rows, vbase:vbase + GLA_HEAD_V],
                                     p_s[rows, vbase + GLA_HEAD_V:vbase + 2 * GLA_HEAD_V]],
                                    axis=0).astype(BF16))
        cl = cumc[CHUNK - 1:CHUNK, :]
        qsms.append(_stack_heads(qc * scale * jnp.exp(cumc), lane).astype(BF16))
        ksms.append(_stack_heads(kc * jnp.exp(-cumc), lane).astype(BF16))
        kdsms.append(_stack_heads(kc * jnp.exp(cl - cumc), lane).astype(BF16))
        decs.append(jnp.exp(cl))
    atts = [jnp.where(causal, _dg(q_, k_, NT), 0.0).astype(BF16) for q_, k_ in zip(qsms, ksms)]
    o_intra = [_dg(a_, v_) for a_, v_ in zip(atts, vsts)]
    d_states = [_dg(v_, kd_, TN) for v_, kd_ in zip(vsts, kdsms)]

    gts = [None] * len(units)
    for pr in range(n_pairs):
        g = st_ref[pr]
        for c in range(n_chunks):
            i = c * n_pairs + pr
            gts[i] = g.astype(BF16)
            g = g * decs[i] + d_states[i]
        st_ref[pr] = g

    for i, (c, pr) in enumerate(units):
        o_st = o_intra[i] + _dg(qsms[i], gts[i], NT)
        rows = slice(c * CHUNK, (c + 1) * CHUNK)
        ob = 2 * pr * GLA_HEAD_V
        o_s[rows, ob:ob + GLA_HEAD_V] = o_st[:CHUNK]
        o_s[rows, ob + GLA_HEAD_V:ob + 2 * GLA_HEAD_V] = o_st[CHUNK:]

    og_base = 2 * GLA_DK + GLA_DV
    for h in range(GLA_HEADS):
        sl = slice(h * GLA_HEAD_V, (h + 1) * GLA_HEAD_V)
        oh = o_s[:, sl]
        on = oh * lax.rsqrt(jnp.mean(oh * oh, axis=-1, keepdims=True) + GLA_NORM_EPS) * ng_ref[...]
        og = p_s[:, og_base + h * GLA_HEAD_V:og_base + (h + 1) * GLA_HEAD_V]
        o_ref[0, :, sl] = on * _silu(og)


def _gla_layer(x, mod, wm, wga, wal, bal, ng, tile=512):
    b, s, _ = x.shape
    n_main = 2 * GLA_DK + 2 * GLA_DV
    vmem = (4 * tile * D_MODEL * 4 + 4 * tile * GLA_DV * 4 + D_MODEL * (n_main + LANES) * 2
            + tile * (n_main + GLA_DK + GLA_DV) * 4 + 6 * tile * n_main * 4 + (8 << 20))
    return pl.pallas_call(
        functools.partial(_gla_kernel, tile=tile),
        grid=(b, s // tile),
        in_specs=[pl.BlockSpec((1, tile, D_MODEL), lambda i, j: (i, j, 0)),
                  pl.BlockSpec((1, 6, D_MODEL), lambda i, j: (i, 0, 0)),
                  _const_spec((D_MODEL, n_main)),
                  _const_spec((D_MODEL, LANES)),
                  _const_spec((LANES, GLA_DK)),
                  _const_spec((1, GLA_DK)),
                  _const_spec((1, GLA_HEAD_V))],
        out_specs=pl.BlockSpec((1, tile, GLA_DV), lambda i, j: (i, j, 0)),
        out_shape=jax.ShapeDtypeStruct((b, s, GLA_DV), F32),
        scratch_shapes=[pltpu.VMEM((GLA_HEADS // 2, GLA_HEAD_V, LANES), F32),
                        pltpu.VMEM((tile, n_main), F32),
                        pltpu.VMEM((tile, GLA_DK), F32),
                        pltpu.VMEM((tile, GLA_DV), F32)],
        compiler_params=_cparams(("parallel", "arbitrary"), vmem),
        name="gla_mixer",
    )(x, mod, wm, wga, wal, bal, ng)


def _inv_unit_lower(ns, eye, m16, m32, m64):
    ds = [jnp.where(m16, n, 0.0) for n in ns]
    xs = [eye + d for d in ds]
    pws = [d.astype(BF16) for d in ds]
    pws = [_dg(p, p).astype(BF16) for p in pws]
    for level in range(2):
        prods = [_dg(p, jnp.concatenate([x.astype(BF16), p], axis=1)) for x, p in zip(xs, pws)]
        xs = [x + pr[:, :LANES] for x, pr in zip(xs, prods)]
        pws = [pr[:, LANES:].astype(BF16) for pr in prods]
    xs = [x + _dg(p, x.astype(BF16)) for x, p in zip(xs, pws)]
    for m in (m32, m64):
        xbs = [x.astype(BF16) for x in xs]
        xos = [_dg(xb, jnp.where(m, n, 0.0).astype(BF16)) for xb, n in zip(xbs, ns)]
        xs = [x + _dg(xo.astype(BF16), xb) for x, xo, xb in zip(xs, xos, xbs)]
    return xs


def _rwkv_kernel(x_ref, mod_ref, w_ref, mu_ref, w0_ref, wup_ref, a0_ref, aup_ref, gup_ref,
                 kk_ref, ka_ref, rk_ref, gng_ref, gnb_ref, o_ref,
                 st_ref, carry_ref, a_s, b_s, k_s, r_s, v_s, bb_s, k2_s, cum_s, y_s, bon_s, g_s,
                 *, tile):
    @pl.when(pl.program_id(1) == 0)
    def _():
        st_ref[...] = jnp.zeros_like(st_ref)
        carry_ref[...] = jnp.zeros_like(carry_ref)

    x = x_ref[0]
    u = (x * (1.0 + mod_ref[0, 1:2, :]) + mod_ref[0, 0:1, :]).astype(BF16)
    p = _dg(u, w_ref[...])
    row = lax.broadcasted_iota(jnp.int32, (tile, 1), 0)
    prev = jnp.where(row == 0, carry_ref[0:1, :], pltpu.roll(p, 1, 0))
    carry_ref[0:1, :] = p[tile - 1:tile, :]
    ps = p + (prev - p) * mu_ref[...]

    d = RWKV_DIM
    r = ps[:, 0:d]
    k = ps[:, d:2 * d]
    v = ps[:, 2 * d:3 * d]
    wa_lo = ps[:, 3 * d:3 * d + LANES]
    g_lo = ps[:, 3 * d + LANES:3 * d + 2 * LANES]
    wl = w0_ref[...] + _dot3(jnp.tanh(wa_lo), wup_ref[...])
    lw = -_sigmoid(wl) * math.exp(-0.5)
    a = _sigmoid(a0_ref[...] + _dot1(wa_lo, aup_ref[...]))
    g_s[...] = _dot1(_sigmoid(g_lo), gup_ref[...])
    kk = k * kk_ref[...]
    k2 = k * (1.0 + (a - 1.0) * ka_ref[...])

    bi = lax.broadcasted_iota(jnp.int32, (LANES, LANES), 0)
    bj = lax.broadcasted_iota(jnp.int32, (LANES, LANES), 1)
    same64 = (bi >> 6) == (bj >> 6)
    seg = jnp.where(same64, 1.0, 0.0).astype(BF16)

    def seg_sum(t):
        return jnp.concatenate(
            [_dot_exact_rhs(t[:, q * LANES:(q + 1) * LANES], seg) for q in range(d // LANES)],
            axis=1)

    kk = kk / jnp.maximum(jnp.sqrt(seg_sum(kk * kk)), 1e-12)
    bb = kk * a
    bon_s[...] = seg_sum(r * k2 * rk_ref[...]) * v

    ri = lax.broadcasted_iota(jnp.int32, (tile, tile), 0)
    ci = lax.broadcasted_iota(jnp.int32, (tile, tile), 1)
    ltri = jnp.where(((ri >> 6) == (ci >> 6)) & (ci <= ri), 1.0, 0.0).astype(BF16)
    cum = _dot_exact_lhs(ltri, lw)
    e_neg = jnp.exp(-cum)
    a_s[...] = (-kk * jnp.exp(cum - lw)).astype(BF16)
    b_s[...] = (bb * e_neg).astype(BF16)
    k_s[...] = (k2 * e_neg).astype(BF16)
    r_s[...] = (r * jnp.exp(cum)).astype(BF16)
    v_s[...] = v.astype(BF16)
    bb_s[...] = bb
    k2_s[...] = k2
    cum_s[...] = cum

    lane = lax.broadcasted_iota(jnp.int32, (CHUNK, LANES), 1)
    strict = same64 & (bj < bi)
    incl = same64 & (bj <= bi)
    m16 = (bi >> 4) == (bj >> 4)
    m32 = ((bi >> 5) == (bj >> 5)) & jnp.logical_not(m16)
    m64 = same64 & ((bi >> 5) != (bj >> 5))
    eye = jnp.where(bi == bj, 1.0, 0.0)
    h2 = 2 * CHUNK

    n_pairs = RWKV_HEADS // 2
    units = [(c, pr) for c in range(tile // CHUNK) for pr in range(n_pairs)]

    def rows_of(c):
        return slice(c * CHUNK, (c + 1) * CHUNK)

    def lanes_of(pr):
        return slice(pr * LANES, (pr + 1) * LANES)

    stacked = {}
    for name, ref in (("a", a_s), ("b", b_s), ("k", k_s), ("r", r_s), ("v", v_s)):
        stacked[name] = [_stack_heads(ref[rows_of(c), lanes_of(pr)], lane) for c, pr in units]
    scs = [_dg(jnp.concatenate([a_, r_], axis=0), jnp.concatenate([b_, k_], axis=0), NT)
           for a_, r_, b_, k_ in zip(stacked["a"], stacked["r"], stacked["b"], stacked["k"])]
    abs_ = [jnp.where(strict, sc[:h2, :h2], 0.0) for sc in scs]
    akvs = [_dg(jnp.where(strict, sc[:h2, h2:], 0.0).astype(BF16), v_)
            for sc, v_ in zip(scs, stacked["v"])]
    rbks = [jnp.concatenate([jnp.where(incl, sc[h2:, :h2], 0.0),
                             jnp.where(incl, sc[h2:, h2:], 0.0)], axis=1).astype(BF16)
            for sc in scs]
    tinvs = _inv_unit_lower(abs_, eye, m16, m32, m64)
    wus = [_dg(t.astype(BF16), jnp.concatenate([a_, akv.astype(BF16)], axis=1))
           for t, a_, akv in zip(tinvs, stacked["a"], akvs)]
    wrs = [jnp.concatenate([wu[:, :LANES].astype(BF16), r_], axis=0)
           for wu, r_ in zip(wus, stacked["r"])]

    for c in range(tile // CHUNK):
        idx = [c * n_pairs + pr for pr in range(n_pairs)]
        gts = [st_ref[pr] for pr in range(n_pairs)]
        wrgs = [_dg(wrs[i], g_.astype(BF16), NT) for i, g_ in zip(idx, gts)]
        ums = [(wrg[:h2] + wus[i][:, LANES:]).astype(BF16) for i, wrg in zip(idx, wrgs)]
        uvs = [jnp.concatenate([um, stacked["v"][i]], axis=0) for i, um in zip(idx, ums)]
        ys = [wrg[h2:] + _dg(rbks[i], uv) for i, wrg, uv in zip(idx, wrgs, uvs)]
        for pr in range(n_pairs):
            cumc = cum_s[rows_of(c), lanes_of(pr)]
            cl = cumc[CHUNK - 1:CHUNK, :]
            dec = jnp.exp(cl - cumc)
            bkd = jnp.concatenate([_stack_heads(bb_s[rows_of(c), lanes_of(pr)] * dec, lane),
                                   _stack_heads(k2_s[rows_of(c), lanes_of(pr)] * dec, lane)],
                                  axis=0).astype(BF16)
            st_ref[pr] = gts[pr] * jnp.exp(cl) + _dg(uvs[pr], bkd, TN)
            y_s[rows_of(c), lanes_of(pr)] = ys[pr][:CHUNK] + ys[pr][CHUNK:]

    y = y_s[...]
    inv_n = 1.0 / RWKV_HEAD
    mu_h = seg_sum(y) * inv_n
    yc = y - mu_h
    var = seg_sum(yc * yc) * inv_n
    yn = yc * lax.rsqrt(var + RWKV_GN_EPS) * gng_ref[...] + gnb_ref[...]
    o_ref[0] = (yn + bon_s[...]) * g_s[...]


def _rwkv_layer(x, mod, w, mu, w0, wup, a0, aup, gup, k_k, k_a, r_k, gn_g, gn_b, tile=256):
    b, s, _ = x.shape
    d = RWKV_DIM
    vec = lambda: _const_spec((1, d))
    vmem = (4 * tile * D_MODEL * 4 + 4 * tile * d * 4 + D_MODEL * RWKV_IN * 2
            + 11 * tile * d * 4 + 8 * tile * RWKV_IN * 4 + (12 << 20))
    return pl.pallas_call(
        functools.partial(_rwkv_kernel, tile=tile),
        grid=(b, s // tile),
        in_specs=[pl.BlockSpec((1, tile, D_MODEL), lambda i, j: (i, j, 0)),
                  pl.BlockSpec((1, 6, D_MODEL), lambda i, j: (i, 0, 0)),
                  _const_spec((D_MODEL, RWKV_IN)),
                  _const_spec((1, RWKV_IN)),
                  vec(), _const_spec((LANES, d)), vec(), _const_spec((LANES, d)),
                  _const_spec((LANES, d)), vec(), vec(), vec(), vec(), vec()],
        out_specs=pl.BlockSpec((1, tile, d), lambda i, j: (i, j, 0)),
        out_shape=jax.ShapeDtypeStruct((b, s, d), F32),
        scratch_shapes=[pltpu.VMEM((RWKV_HEADS // 2, LANES, LANES), F32),
                        pltpu.VMEM((8, RWKV_IN), F32)]
                       + [pltpu.VMEM((tile, d), BF16) for _ in range(5)]
                       + [pltpu.VMEM((tile, d), F32) for _ in range(6)],
        compiler_params=_cparams(("parallel", "arbitrary"), vmem),
        name="rwkv7_mixer",
    )(x, mod, w, mu, w0, wup, a0, aup, gup, k_k, k_a, r_k, gn_g, gn_b)


def _qkv_kernel(x_ref, mod_ref, w_ref, *refs, tile):
    out_refs, p_s = refs[:-1], refs[-1]
    x = x_ref[0]
    u = (x * (1.0 + mod_ref[0, 1:2, :]) + mod_ref[0, 0:1, :]).astype(BF16)
    p = _dg(u, w_ref[...])
    n_grp = 3 * ATT_DIM // LANES
    for g in range(n_grp):
        p_s[g] = p[:, g * LANES:(g + 1) * LANES]
    per = ATT_DIM // LANES
    for pi, dil in enumerate(DILATIONS):
        q_ref, k_ref, v_ref = out_refs[3 * pi:3 * pi + 3]
        for r in range(dil):
            rows = pl.ds(r, tile // dil, stride=dil) if dil > 1 else slice(None)
            grp = lambda t: jnp.concatenate([p_s[t * per + g, rows, :] for g in range(per)], axis=1)
            q_ref[0, r] = (grp(0) * (LOG2_E * ATT_HEAD ** -0.5)).astype(BF16)
            k_ref[0, r] = grp(1).astype(BF16)
            v_ref[0, r] = grp(2).astype(BF16)


def _qkv_layer(x, mod, w, tile=512):
    b, s, _ = x.shape
    out_shape, out_specs = [], []
    for dil in DILATIONS:
        for _ in range(3):
            out_shape.append(jax.ShapeDtypeStruct((b, dil, s // dil, ATT_DIM), BF16))
            out_specs.append(pl.BlockSpec((1, dil, tile // dil, ATT_DIM), lambda i, j: (i, 0, j, 0)))
    vmem = (4 * tile * D_MODEL * 4 + D_MODEL * 3 * ATT_DIM * 2 + 4 * tile * 3 * ATT_DIM * 4
            + 2 * 9 * tile * ATT_DIM * 2 + (4 << 20))
    outs = pl.pallas_call(
        functools.partial(_qkv_kernel, tile=tile),
        grid=(b, s // tile),
        in_specs=[pl.BlockSpec((1, tile, D_MODEL), lambda i, j: (i, j, 0)),
                  pl.BlockSpec((1, 6, D_MODEL), lambda i, j: (i, 0, 0)),
                  _const_spec((D_MODEL, 3 * ATT_DIM))],
        out_specs=out_specs,
        out_shape=out_shape,
        scratch_shapes=[pltpu.VMEM((3 * ATT_DIM // LANES, tile, LANES), F32)],
        compiler_params=_cparams(("parallel", "parallel"), vmem),
        name="att_qkv",
    )(x, mod, w)
    return [outs[3 * pi:3 * pi + 3] for pi in range(len(DILATIONS))]


ATT_SPAN = 2048


def _att_kernel(q_ref, kp_ref, kc_ref, vp_ref, vc_ref, o_ref, l_ref, kbuf, vbuf, *, dilation):
    n_sub = ATT_SPAN // dilation
    n_blk = n_sub // ATT_BLOCK
    span = pl.program_id(1)
    kbuf[:, :ATT_BLOCK] = kp_ref[0]
    kbuf[:, ATT_BLOCK:] = kc_ref[0]
    vbuf[:, :ATT_BLOCK] = vp_ref[0]
    vbuf[:, ATT_BLOCK:] = vc_ref[0]

    qi = lax.broadcasted_iota(jnp.int32, (ATT_BLOCK, 2 * ATT_BLOCK), 0)
    kj = lax.broadcasted_iota(jnp.int32, (ATT_BLOCK, 2 * ATT_BLOCK), 1)
    steps = qi + ATT_BLOCK - kj
    window = (steps >= 0) & (steps <= ATT_BLOCK)
    dist = (steps * dilation).astype(F32)
    lane_q = lax.broadcasted_iota(jnp.int32, (ATT_BLOCK, LANES), 1)
    zero = jnp.zeros((), BF16)
    heads = range(ATT_HEADS)
    biases = [jnp.where(window, dist * -(LOG2_E * 2.0 ** (-8.0 * (h + 1) / ATT_HEADS)), -jnp.inf)
              for h in heads]

    lane_k = lax.broadcasted_iota(jnp.int32, (2 * ATT_BLOCK, LANES), 1)
    one = jnp.ones((), BF16)
    pair_of = lambda t, h: t[:, (h // 2) * LANES:(h // 2 + 1) * LANES]
    own = lambda lane, h: (lane >= ATT_HEAD) if h % 2 else (lane < ATT_HEAD)

    def scores(uidx):
        r = uidx >> (n_blk.bit_length() - 1)
        n = uidx & (n_blk - 1)
        row0 = pl.multiple_of(n * ATT_BLOCK, ATT_BLOCK)
        q = q_ref[0, r, pl.ds(row0, ATT_BLOCK), :]
        kk = kbuf[r, pl.ds(row0, 2 * ATT_BLOCK), :]
        ss = [_dg(jnp.where(own(lane_q, h), pair_of(q, h), zero), pair_of(kk, h), NT)
              for h in heads]
        return r, n, row0, ss

    def finish(first_span, r, n, row0, ss):
        vv = vbuf[r, pl.ds(row0, 2 * ATT_BLOCK), :]
        if first_span:
            first_key = jnp.where(n == 0, ATT_BLOCK, 0)
            head_mask = jnp.where(kj >= first_key, 0.0, -jnp.inf)
            ss = [s + (biases[h] + head_mask) for h, s in zip(heads, ss)]
        else:
            ss = [s + biases[h] for h, s in zip(heads, ss)]
        ms = [jnp.max(s, axis=-1, keepdims=True) for s in ss]
        es = [jnp.exp2(s - m).astype(BF16) for s, m in zip(ss, ms)]
        pvs = [_dg(e, jnp.where(own(lane_k, h), pair_of(vv, h), one)) for h, e in zip(heads, es)]
        m_blk = jnp.zeros((ATT_BLOCK, LANES), F32)
        d_blk = jnp.ones((ATT_BLOCK, LANES), F32)
        tok0 = n * (ATT_BLOCK * dilation) + r
        if dilation == 1:
            rows = pl.ds(pl.multiple_of(tok0, ATT_BLOCK), ATT_BLOCK)
        else:
            rows = pl.ds(tok0, ATT_BLOCK, stride=dilation)
        for p in range(ATT_HEADS // 2):
            even, odd = pvs[2 * p], pvs[2 * p + 1]
            num = jnp.where(lane_q < ATT_HEAD, even, odd)
            den_sw = jnp.where(lane_q < ATT_HEAD, odd, even)
            den = pltpu.roll(den_sw, ATT_HEAD, 1)
            o_ref[0, p, rows, :] = num / den
            m_blk = jnp.where(lane_q == 2 * p, ms[2 * p], m_blk)
            m_blk = jnp.where(lane_q == 2 * p + 1, ms[2 * p + 1], m_blk)
            d_blk = jnp.where(lane_q == 2 * p, den, d_blk)
            d_blk = jnp.where(lane_q == 2 * p + 1, den_sw, d_blk)
        l_ref[0, rows, :] = (m_blk + jnp.log2(d_blk)) * LN_2

    def make_body(first_span):
        def body(i, carry):
            u0 = scores(2 * i)
            u1 = scores(2 * i + 1)
            finish(first_span, *u0)
            finish(first_span, *u1)
            return carry
        return body

    n_units = dilation * n_blk

    @pl.when(span == 0)
    def _():
        lax.fori_loop(0, n_units // 2, make_body(True), 0)

    @pl.when(span != 0)
    def _():
        lax.fori_loop(0, n_units // 2, make_body(False), 0)


def _att_pattern(q, k, v, dilation):
    b, _, n_res, _ = q.shape
    s = dilation * n_res
    n_sub = ATT_SPAN // dilation
    n_blk = n_sub // ATT_BLOCK
    cur = pl.BlockSpec((1, dilation, n_sub, ATT_DIM), lambda i, j: (i, 0, j, 0))
    prev = pl.BlockSpec((1, dilation, ATT_BLOCK, ATT_DIM),
                        lambda i, j: (i, 0, jnp.maximum(j * n_blk - 1, 0), 0))
    buf = pltpu.VMEM((dilation, ATT_BLOCK + n_sub, ATT_DIM), BF16)
    vmem = (2 * (3 * ATT_SPAN + 2 * dilation * ATT_BLOCK) * ATT_DIM * 2
            + 2 * dilation * (ATT_BLOCK + n_sub) * ATT_DIM * 2
            + 2 * ATT_SPAN * (ATT_DIM + LANES) * 4 + (8 << 20))
    return pl.pallas_call(
        functools.partial(_att_kernel, dilation=dilation),
        grid=(b, s // ATT_SPAN),
        in_specs=[cur, prev, cur, prev, cur],
        out_specs=[pl.BlockSpec((1, ATT_HEADS // 2, ATT_SPAN, LANES), lambda i, j: (i, 0, j, 0)),
                   pl.BlockSpec((1, ATT_SPAN, LANES), lambda i, j: (i, j, 0))],
        out_shape=[jax.ShapeDtypeStruct((b, ATT_HEADS // 2, s, LANES), F32),
                   jax.ShapeDtypeStruct((b, s, LANES), F32)],
        scratch_shapes=[buf, buf],
        compiler_params=_cparams(("parallel", "parallel"), vmem),
        name=f"dilated_att_d{dilation}",
    )(q, k, k, v, v)


def _merge_kernel(x_ref, mod_ref, oa_ref, ob_ref, o1_ref, o4_ref, o16_ref, l1_ref, l4_ref, l16_ref,
                  wg_ref, wb_ref, wo_ref, g_ref, b_ref, out_ref):
    x = x_ref[0]
    u = (x * (1.0 + mod_ref[0, 1:2, :]) + mod_ref[0, 0:1, :]).astype(BF16)
    gates = _sigmoid(_dg(u, wg_ref[...]))
    l1, l4, l16 = l1_ref[0], l4_ref[0], l16_ref[0]
    m = jnp.maximum(jnp.maximum(l1, l4), l16)
    e1, e4, e16 = jnp.exp(l1 - m), jnp.exp(l4 - m), jnp.exp(l16 - m)
    inv = 1.0 / (e1 + e4 + e16)
    hi = lax.broadcasted_iota(jnp.int32, (LANES, ATT_DIM), 0)
    hj = lax.broadcasted_iota(jnp.int32, (LANES, ATT_DIM), 1)
    expand = jnp.where(hi == (hj >> 6), 1.0, 0.0).astype(BF16)
    lane_cat = lambda ref: jnp.concatenate([ref[0, p] for p in range(ATT_HEADS // 2)], axis=1)
    o_c = (_dot_exact_rhs(e1 * inv, expand, parts=2) * lane_cat(o1_ref)
           + _dot_exact_rhs(e4 * inv, expand, parts=2) * lane_cat(o4_ref)
           + _dot_exact_rhs(e16 * inv, expand, parts=2) * lane_cat(o16_ref))
    merged = (gates[:, :D_MODEL] * _dot1(oa_ref[0], wb_ref[0])
              + gates[:, D_MODEL:2 * D_MODEL] * _dot1(ob_ref[0], wb_ref[1])
              + gates[:, 2 * D_MODEL:] * _dot1(o_c, wb_ref[2]))
    h = _dot1(merged, wo_ref[...])
    z = DEEPNORM_ALPHA * x + (1.0 + mod_ref[0, 2:3, :]) * h
    out_ref[0] = _layer_norm(z, g_ref[...], b_ref[...])


def _merge_layer(x, mod, oa, ob, oc, lses, wg, wb, wo, g, bta, tile=512):
    b, s, _ = x.shape
    xs = pl.BlockSpec((1, tile, D_MODEL), lambda i, j: (i, j, 0))
    bs = pl.BlockSpec((1, tile, ATT_DIM), lambda i, j: (i, j, 0))
    ls = pl.BlockSpec((1, tile, LANES), lambda i, j: (i, j, 0))
    cs = pl.BlockSpec((1, ATT_HEADS // 2, tile, LANES), lambda i, j: (i, 0, j, 0))
    vmem = (4 * tile * D_MODEL * 4 + 10 * tile * ATT_DIM * 4 + 6 * tile * LANES * 4
            + (3 * D_MODEL * D_MODEL + 3 * ATT_DIM * D_MODEL + D_MODEL * D_MODEL) * 2
            + 6 * tile * 3 * D_MODEL * 4 + (4 << 20))
    return pl.pallas_call(
        _merge_kernel,
        grid=(b, s // tile),
        in_specs=[xs, pl.BlockSpec((1, 6, D_MODEL), lambda i, j: (i, 0, 0)),
                  bs, bs, cs, cs, cs, ls, ls, ls,
                  _const_spec((D_MODEL, N_BRANCH * D_MODEL)),
                  _const_spec((N_BRANCH, ATT_DIM, D_MODEL)),
                  _const_spec((D_MODEL, D_MODEL)),
                  _const_spec((1, D_MODEL)), _const_spec((1, D_MODEL))],
        out_specs=xs,
        out_shape=jax.ShapeDtypeStruct((b, s, D_MODEL), F32),
        compiler_params=_cparams(("parallel", "parallel"), vmem),
        name="merge_ln1",
    )(x, mod, oa, ob, oc[0], oc[1], oc[2], lses[0], lses[1], lses[2], wg, wb, wo, g, bta)


def _ffn_kernel(x_ref, mod_ref, w1_ref, w2_ref, g_ref, b_ref, out_ref):
    x = x_ref[0]
    u = (x * (1.0 + mod_ref[0, 4:5, :]) + mod_ref[0, 3:4, :]).astype(BF16)
    h = _dg(u, w1_ref[...])
    act = (_silu(h[:, :FFN_HIDDEN]) * h[:, FFN_HIDDEN:]).astype(BF16)
    y = _dg(act, w2_ref[...])
    z = DEEPNORM_ALPHA * x + (1.0 + mod_ref[0, 5:6, :]) * y
    out_ref[0] = _layer_norm(z, g_ref[...], b_ref[...])


def _ffn_layer(x, mod, w1, w2, g, bta, tile=512):
    b, s, _ = x.shape
    xs = pl.BlockSpec((1, tile, D_MODEL), lambda i, j: (i, j, 0))
    vmem = (4 * tile * D_MODEL * 4 + 3 * D_MODEL * FFN_HIDDEN * 2
            + 4 * tile * 2 * FFN_HIDDEN * 4 + (4 << 20))
    return pl.pallas_call(
        _ffn_kernel,
        grid=(b, s // tile),
        in_specs=[xs, pl.BlockSpec((1, 6, D_MODEL), lambda i, j: (i, 0, 0)),
                  _const_spec((D_MODEL, 2 * FFN_HIDDEN)),
                  _const_spec((FFN_HIDDEN, D_MODEL)),
                  _const_spec((1, D_MODEL)), _const_spec((1, D_MODEL))],
        out_specs=xs,
        out_shape=jax.ShapeDtypeStruct((b, s, D_MODEL), F32),
        compiler_params=_cparams(("parallel", "parallel"), vmem),
        name="ffn_ln2",
    )(x, mod, w1, w2, g, bta)


def _pad_rows(m, rows, offset=0):
    out = jnp.zeros((rows,) + m.shape[1:], m.dtype)
    return out.at[offset:offset + m.shape[0]].set(m)


def _mixer_layer(x, mod, w_in, gla_w_alpha, gla_b_alpha, gla_norm_g, rwkv_mu, rwkv_w0, rwkv_w_up,
                 rwkv_a0, rwkv_a_up, rwkv_g_up, rwkv_k_k, rwkv_k_a, rwkv_r_k, rwkv_gn_g,
                 rwkv_gn_b, w_branch, w_out, ln_g, ln_b):
    row = lambda t: t.reshape(1, -1)
    o0 = 0
    gq_end = 2 * GLA_DK + GLA_DV
    w_gla = jnp.concatenate([w_in[:, :gq_end], w_in[:, gq_end + GLA_GATE_RANK:GLA_IN]],
                            axis=1).astype(BF16)
    w_ga = jnp.zeros((D_MODEL, LANES), F32).at[:, :GLA_GATE_RANK].set(
        w_in[:, gq_end:gq_end + GLA_GATE_RANK]).astype(BF16)
    o0 += GLA_IN
    w_rwkv = w_in[:, o0:o0 + RWKV_IN].astype(BF16)
    o0 += RWKV_IN
    w_att = w_in[:, o0:o0 + 3 * ATT_DIM].astype(BF16)
    o0 += 3 * ATT_DIM
    w_gate = w_in[:, o0:].astype(BF16)

    o_a = _gla_layer(x, mod, w_gla, w_ga, _pad_rows(gla_w_alpha, LANES), row(gla_b_alpha),
                     row(gla_norm_g))
    o_b = _rwkv_layer(x, mod, w_rwkv, row(rwkv_mu), row(rwkv_w0),
                      _pad_rows(rwkv_w_up, LANES), row(rwkv_a0),
                      _pad_rows(rwkv_a_up, LANES, RWKV_DECAY_RANK), rwkv_g_up,
                      row(rwkv_k_k), row(rwkv_k_a), row(rwkv_r_k), row(rwkv_gn_g),
                      row(rwkv_gn_b))
    qkvs = _qkv_layer(x, mod, w_att)
    res = [_att_pattern(*qkv, dil) for qkv, dil in zip(qkvs, DILATIONS)]
    return _merge_layer(x, mod, o_a, o_b, [o for o, _ in res], [l for _, l in res],
                        w_gate, w_branch.astype(BF16), w_out.astype(BF16), row(ln_g), row(ln_b))


def kernel(x, c, w_ada, b_ada, w_in, gla_w_alpha, gla_b_alpha, gla_norm_g, rwkv_mu, rwkv_w0,
           rwkv_w_up, rwkv_a0, rwkv_a_up, rwkv_g_up, rwkv_k_k, rwkv_k_a, rwkv_r_k, rwkv_gn_g,
           rwkv_gn_b, w_branch, w_out, ln1_g, ln1_b, ffn_w1, ffn_w2, ln2_g, ln2_b):
    mod = _modulation(c, w_ada, b_ada)
    for l in range(DEPTH):
        x = _mixer_layer(x, mod[l], w_in[l], gla_w_alpha[l], gla_b_alpha[l], gla_norm_g[l],
                         rwkv_mu[l], rwkv_w0[l], rwkv_w_up[l], rwkv_a0[l], rwkv_a_up[l],
                         rwkv_g_up[l], rwkv_k_k[l], rwkv_k_a[l], rwkv_r_k[l], rwkv_gn_g[l],
                         rwkv_gn_b[l], w_branch[l], w_out[l], ln1_g[l], ln1_b[l])
        x = _ffn_layer(x, mod[l], ffn_w1[l].astype(BF16), ffn_w2[l].astype(BF16),
                       ln2_g[l].reshape(1, -1), ln2_b[l].reshape(1, -1))
    return x
```

```python
import functools
import math

import jax
import jax.numpy as jnp
from jax import lax
from jax.experimental import pallas as pl
from jax.experimental.pallas import tpu as pltpu

F32 = jnp.float32
BF16 = jnp.bfloat16

D_MODEL = 1024
DEPTH = 4
GLA_HEADS, GLA_HEAD_K, GLA_HEAD_V = 4, 64, 128
GLA_DK, GLA_DV = GLA_HEADS * GLA_HEAD_K, GLA_HEADS * GLA_HEAD_V
GLA_GATE_RANK = 16
GLA_GATE_TAU = 16.0
GLA_NORM_EPS = 1e-5
RWKV_HEADS, RWKV_HEAD = 8, 64
RWKV_DIM = RWKV_HEADS * RWKV_HEAD
RWKV_DECAY_RANK, RWKV_ICLR_RANK, RWKV_GATE_RANK = 64, 64, 128
RWKV_IN = 3 * RWKV_DIM + RWKV_DECAY_RANK + RWKV_ICLR_RANK + RWKV_GATE_RANK
RWKV_GN_EPS = 64e-5
ATT_HEADS, ATT_HEAD = 8, 64
ATT_DIM = ATT_HEADS * ATT_HEAD
ATT_BLOCK = 128
DILATIONS = (1, 4, 16)
N_BRANCH = 3
FFN_HIDDEN = 2816
LN_EPS = 1e-5
DEEPNORM_ALPHA = (2 * DEPTH) ** 0.25
GLA_IN = 2 * GLA_DK + GLA_DV + GLA_GATE_RANK + GLA_DV

LOG2_E = math.log2(math.e)
LN_2 = math.log(2.0)
LANES = 128
CHUNK = 64
VMEM_LIMIT_CAP = 60000 * 1024

NN = ((1,), (0,))
NT = ((1,), (1,))
TN = ((0,), (0,))


def _dg(a, b, dims=NN):
    return lax.dot_general(a, b, (dims, ((), ())), preferred_element_type=F32)


def _dot1(a, b, dims=NN):
    return _dg(a.astype(BF16), b.astype(BF16), dims)


def _split2(a):
    hi = a.astype(BF16)
    lo = (a - hi.astype(F32)).astype(BF16)
    return hi, lo


def _dot3(a, b, dims=NN):
    ah, al = _split2(a)
    bh, bl = _split2(b)
    return _dg(ah, bh, dims) + (_dg(ah, bl, dims) + _dg(al, bh, dims))


def _dot_exact_rhs(a, m_bf16, dims=NN, parts=2):
    acc = None
    rem = a
    for _ in range(parts):
        hi = rem.astype(BF16)
        term = _dg(hi, m_bf16, dims)
        acc = term if acc is None else acc + term
        rem = rem - hi.astype(F32)
    return acc


def _dot_exact_lhs(m_bf16, a, parts=2):
    acc = None
    rem = a
    for _ in range(parts):
        hi = rem.astype(BF16)
        term = _dg(m_bf16, hi)
        acc = term if acc is None else acc + term
        rem = rem - hi.astype(F32)
    return acc


def _sigmoid(x):
    return 1.0 / (1.0 + jnp.exp(-x))


def _silu(x):
    return x * _sigmoid(x)


def _log_sigmoid(x):
    return jnp.minimum(x, 0.0) - jnp.log(1.0 + jnp.exp(-jnp.abs(x)))


def _layer_norm(z, g, b):
    mu = jnp.mean(z, axis=-1, keepdims=True)
    zc = z - mu
    var = jnp.mean(zc * zc, axis=-1, keepdims=True)
    return zc * lax.rsqrt(var + LN_EPS) * g + b


def _stack_heads(x, lane):
    lo = jnp.where(lane < RWKV_HEAD, x, 0.0)
    hi = jnp.where(lane >= RWKV_HEAD, x, 0.0)
    return jnp.concatenate([lo, hi], axis=0)


def _cparams(sem, vmem_bytes):
    return pltpu.CompilerParams(dimension_semantics=sem,
                                vmem_limit_bytes=int(min(vmem_bytes, VMEM_LIMIT_CAP)))


def _const_spec(shape):
    zeros = (0,) * len(shape)
    return pl.BlockSpec(shape, lambda *_: zeros, pipeline_mode=pl.Buffered(1))


def _mod_kernel(ct_ref, w_ref, b_ref, o_ref, *, batch):
    s = _silu(ct_ref[...])
    tn = w_ref.shape[2]
    o_ref[0] = jnp.zeros(o_ref.shape[1:], F32)
    for bi in range(batch):
        sb = jnp.broadcast_to(s[:, bi:bi + 1], (D_MODEL, LANES))
        for j in range(tn // LANES):
            sl = slice(j * LANES, (j + 1) * LANES)
            acc = jnp.sum(w_ref[0, :, sl] * sb, axis=0, keepdims=True)
            o_ref[0, bi:bi + 1, sl] = acc + b_ref[0, :, sl]


def _modulation(c, w_ada, b_ada):
    n_l = w_ada.shape[0]
    b = c.shape[0]
    rows = 8
    assert b <= rows
    c_t = jnp.zeros((D_MODEL, rows), F32).at[:, :b].set(c.T)
    tn = 1536
    out = pl.pallas_call(
        functools.partial(_mod_kernel, batch=b),
        grid=(n_l, 6 * D_MODEL // tn),
        in_specs=[pl.BlockSpec((D_MODEL, rows), lambda l, j: (0, 0)),
                  pl.BlockSpec((1, D_MODEL, tn), lambda l, j: (l, 0, j)),
                  pl.BlockSpec((1, 1, tn), lambda l, j: (l, 0, j))],
        out_specs=pl.BlockSpec((1, rows, tn), lambda l, j: (l, 0, j)),
        out_shape=jax.ShapeDtypeStruct((n_l, rows, 6 * D_MODEL), F32),
        compiler_params=_cparams(("parallel", "parallel"), 4 * D_MODEL * tn * 4),
        name="adaln_mod",
    )(c_t, w_ada, b_ada.reshape(n_l, 1, 6 * D_MODEL))
    return out[:, :b].reshape(n_l, b, 6, D_MODEL)


def _gla_kernel(x_ref, mod_ref, wm_ref, wga_ref, wal_ref, bal_ref, ng_ref, o_ref,
                st_ref, p_s, cum_s, o_s, *, tile):
    @pl.when(pl.program_id(1) == 0)
    def _():
        st_ref[...] = jnp.zeros_like(st_ref)

    n_half = 2
    hrows = tile // n_half
    sls = [slice(i * hrows, (i + 1) * hrows) for i in range(n_half)]
    us = [(x_ref[0, sl, :] * (1.0 + mod_ref[0, 1:2, :]) + mod_ref[0, 0:1, :]).astype(BF16)
          for sl in sls]
    a_los = []
    for sl, u in zip(sls, us):
        p_s[sl, :] = _dg(u, wm_ref[...])
        a_los.append(_dg(u, wga_ref[...]))
    ri = lax.broadcasted_iota(jnp.int32, (hrows, hrows), 0)
    ci = lax.broadcasted_iota(jnp.int32, (hrows, hrows), 1)
    ltri = jnp.where(((ri >> 6) == (ci >> 6)) & (ci <= ri), 1.0, 0.0).astype(BF16)
    for sl, a_lo in zip(sls, a_los):
        z = _dot3(a_lo, wal_ref[...]) + bal_ref[...]
        log_a = _log_sigmoid(z) * (1.0 / GLA_GATE_TAU)
        cum_s[sl, :] = _dot_exact_lhs(ltri, log_a)

    lane = lax.broadcasted_iota(jnp.int32, (CHUNK, LANES), 1)
    i2 = lax.broadcasted_iota(jnp.int32, (2 * CHUNK, 2 * CHUNK), 0)
    j2 = lax.broadcasted_iota(jnp.int32, (2 * CHUNK, 2 * CHUNK), 1)
    causal = ((i2 >> 6) == (j2 >> 6)) & (j2 <= i2)
    scale = GLA_HEAD_K ** -0.5

    n_chunks = tile // CHUNK
    n_pairs = GLA_HEADS // 2
    units = [(c, pr) for c in range(n_chunks) for pr in range(n_pairs)]

    qsms, ksms, kdsms, vsts, decs = [], [], [], [], []
    for c, pr in units:
        rows = slice(c * CHUNK, (c + 1) * CHUNK)
        lo = pr * LANES
        cumc = cum_s[rows, lo:lo + LANES]
        qc = p_s[rows, lo:lo + LANES]
        kc = p_s[rows, GLA_DK + lo:GLA_DK + lo + LANES]
        vbase = 2 * GLA_DK + 2 * pr * GLA_HEAD_V
        vsts.append(jnp.concatenate([p_s[rows, vbase:vbase + GLA_HEAD_V],
                                     p_s[rows, vbase + GLA_HEAD_V:vbase + 2 * GLA_HEAD_V]],
                                    axis=0).astype(BF16))
        cl = cumc[CHUNK - 1:CHUNK, :]
        qsms.append(_stack_heads(qc * scale * jnp.exp(cumc), lane).astype(BF16))
        ksms.append(_stack_heads(kc * jnp.exp(-cumc), lane).astype(BF16))
        kdsms.append(_stack_heads(kc * jnp.exp(cl - cumc), lane).astype(BF16))
        decs.append(jnp.exp(cl))
    atts = [jnp.where(causal, _dg(q_, k_, NT), 0.0).astype(BF16) for q_, k_ in zip(qsms, ksms)]
    o_intra = [_dg(a_, v_) for a_, v_ in zip(atts, vsts)]
    d_states = [_dg(v_, kd_, TN) for v_, kd_ in zip(vsts, kdsms)]

    gts = [None] * len(units)
    for pr in range(n_pairs):
        g = st_ref[pr]
        for c in range(n_chunks):
            i = c * n_pairs + pr
            gts[i] = g.astype(BF16)
            g = g * decs[i] + d_states[i]
        st_ref[pr] = g

    for i, (c, pr) in enumerate(units):
        o_st = o_intra[i] + _dg(qsms[i], gts[i], NT)
        rows = slice(c * CHUNK, (c + 1) * CHUNK)
        ob = 2 * pr * GLA_HEAD_V
        o_s[rows, ob:ob + GLA_HEAD_V] = o_st[:CHUNK]
        o_s[rows, ob + GLA_HEAD_V:ob + 2 * GLA_HEAD_V] = o_st[CHUNK:]

    og_base = 2 * GLA_DK + GLA_DV
    for h in range(GLA_HEADS):
        sl = slice(h * GLA_HEAD_V, (h + 1) * GLA_HEAD_V)
        oh = o_s[:, sl]
        on = oh * lax.rsqrt(jnp.mean(oh * oh, axis=-1, keepdims=True) + GLA_NORM_EPS) * ng_ref[...]
        og = p_s[:, og_base + h * GLA_HEAD_V:og_base + (h + 1) * GLA_HEAD_V]
        o_ref[0, :, sl] = on * _silu(og)


def _gla_layer(x, mod, wm, wga, wal, bal, ng, tile=512):
    b, s, _ = x.shape
    n_main = 2 * GLA_DK + 2 * GLA_DV
    vmem = (4 * tile * D_MODEL * 4 + 4 * tile * GLA_DV * 4 + D_MODEL * (n_main + LANES) * 2
            + tile * (n_main + GLA_DK + GLA_DV) * 4 + 6 * tile * n_main * 4 + (8 << 20))
    return pl.pallas_call(
        functools.partial(_gla_kernel, tile=tile),
        grid=(b, s // tile),
        in_specs=[pl.BlockSpec((1, tile, D_MODEL), lambda i, j: (i, j, 0)),
                  pl.BlockSpec((1, 6, D_MODEL), lambda i, j: (i, 0, 0)),
                  _const_spec((D_MODEL, n_main)),
                  _const_spec((D_MODEL, LANES)),
                  _const_spec((LANES, GLA_DK)),
                  _const_spec((1, GLA_DK)),
                  _const_spec((1, GLA_HEAD_V))],
        out_specs=pl.BlockSpec((1, tile, GLA_DV), lambda i, j: (i, j, 0)),
        out_shape=jax.ShapeDtypeStruct((b, s, GLA_DV), F32),
        scratch_shapes=[pltpu.VMEM((GLA_HEADS // 2, GLA_HEAD_V, LANES), F32),
                        pltpu.VMEM((tile, n_main), F32),
                        pltpu.VMEM((tile, GLA_DK), F32),
                        pltpu.VMEM((tile, GLA_DV), F32)],
        compiler_params=_cparams(("parallel", "arbitrary"), vmem),
        name="gla_mixer",
    )(x, mod, wm, wga, wal, bal, ng)


def _inv_unit_lower(ns, eye, m16, m32, m64):
    ds = [jnp.where(m16, n, 0.0) for n in ns]
    xs = [eye + d for d in ds]
    pws = [d.astype(BF16) for d in ds]
    pws = [_dg(p, p).astype(BF16) for p in pws]
    for level in range(2):
        prods = [_dg(p, jnp.concatenate([x.astype(BF16), p], axis=1)) for x, p in zip(xs, pws)]
        xs = [x + pr[:, :LANES] for x, pr in zip(xs, prods)]
        pws = [pr[:, LANES:].astype(BF16) for pr in prods]
    xs = [x + _dg(p, x.astype(BF16)) for x, p in zip(xs, pws)]
    for m in (m32, m64):
        xbs = [x.astype(BF16) for x in xs]
        xos = [_dg(xb, jnp.where(m, n, 0.0).astype(BF16)) for xb, n in zip(xbs, ns)]
        xs = [x + _dg(xo.astype(BF16), xb) for x, xo, xb in zip(xs, xos, xbs)]
    return xs


def _rwkv_kernel(x_ref, mod_ref, w_ref, mu_ref, w0_ref, wup_ref, a0_ref, aup_ref, gup_ref,
                 kk_ref, ka_ref, rk_ref, gng_ref, gnb_ref, o_ref,
                 st_ref, carry_ref, a_s, b_s, k_s, r_s, v_s, bb_s, k2_s, cum_s, y_s, bon_s, g_s,
                 *, tile):
    @pl.when(pl.program_id(1) == 0)
    def _():
        st_ref[...] = jnp.zeros_like(st_ref)
        carry_ref[...] = jnp.zeros_like(carry_ref)

    d = RWKV_DIM
    bi = lax.broadcasted_iota(jnp.int32, (LANES, LANES), 0)
    bj = lax.broadcasted_iota(jnp.int32, (LANES, LANES), 1)
    same64 = (bi >> 6) == (bj >> 6)
    seg = jnp.where(same64, 1.0, 0.0).astype(BF16)
    ltri = jnp.where(same64 & (bj <= bi), 1.0, 0.0).astype(BF16)

    def seg_sum(t):
        return jnp.concatenate(
            [_dot_exact_rhs(t[:, q * LANES:(q + 1) * LANES], seg) for q in range(d // LANES)],
            axis=1)

    n_half = tile // LANES
    sls = [slice(i * LANES, (i + 1) * LANES) for i in range(n_half)]
    projs = [_dg((x_ref[0, sl, :] * (1.0 + mod_ref[0, 1:2, :]) + mod_ref[0, 0:1, :]).astype(BF16),
                 w_ref[...]) for sl in sls]
    row = lax.broadcasted_iota(jnp.int32, (LANES, 1), 0)
    last = carry_ref[0:1, :]
    for sl, p in zip(sls, projs):
        prev = jnp.where(row == 0, last, pltpu.roll(p, 1, 0))
        last = p[LANES - 1:LANES, :]
        ps = p + (prev - p) * mu_ref[...]
        r = ps[:, 0:d]
        k = ps[:, d:2 * d]
        v = ps[:, 2 * d:3 * d]
        wa_lo = ps[:, 3 * d:3 * d + LANES]
        g_lo = ps[:, 3 * d + LANES:3 * d + 2 * LANES]
        wl = w0_ref[...] + _dot3(jnp.tanh(wa_lo), wup_ref[...])
        lw = -_sigmoid(wl) * math.exp(-0.5)
        a = _sigmoid(a0_ref[...] + _dot1(wa_lo, aup_ref[...]))
        g_s[sl, :] = _dot1(_sigmoid(g_lo), gup_ref[...])
        kk = k * kk_ref[...]
        k2 = k * (1.0 + (a - 1.0) * ka_ref[...])
        kk = kk / jnp.maximum(jnp.sqrt(seg_sum(kk * kk)), 1e-12)
        bb = kk * a
        bon_s[sl, :] = seg_sum(r * k2 * rk_ref[...]) * v
        cum = _dot_exact_lhs(ltri, lw)
        e_neg = jnp.exp(-cum)
        a_s[sl, :] = (-kk * jnp.exp(cum - lw)).astype(BF16)
        b_s[sl, :] = (bb * e_neg).astype(BF16)
        k_s[sl, :] = (k2 * e_neg).astype(BF16)
        r_s[sl, :] = (r * jnp.exp(cum)).astype(BF16)
        v_s[sl, :] = v.astype(BF16)
        bb_s[sl, :] = bb
        k2_s[sl, :] = k2
        cum_s[sl, :] = cum
    carry_ref[0:1, :] = last

    lane = lax.broadcasted_iota(jnp.int32, (CHUNK, LANES), 1)
    strict = same64 & (bj < bi)
    incl = same64 & (bj <= bi)
    m16 = (bi >> 4) == (bj >> 4)
    m32 = ((bi >> 5) == (bj >> 5)) & jnp.logical_not(m16)
    m64 = same64 & ((bi >> 5) != (bj >> 5))
    eye = jnp.where(bi == bj, 1.0, 0.0)
    h2 = 2 * CHUNK

    n_pairs = RWKV_HEADS // 2
    units = [(c, pr) for c in range(tile // CHUNK) for pr in range(n_pairs)]

    def rows_of(c):
        return slice(c * CHUNK, (c + 1) * CHUNK)

    def lanes_of(pr):
        return slice(pr * LANES, (pr + 1) * LANES)

    stacked = {}
    for name, ref in (("a", a_s), ("b", b_s), ("k", k_s), ("r", r_s), ("v", v_s)):
        stacked[name] = [_stack_heads(ref[rows_of(c), lanes_of(pr)], lane) for c, pr in units]
    scs = [_dg(jnp.concatenate([a_, r_], axis=0), jnp.concatenate([b_, k_], axis=0), NT)
           for a_, r_, b_, k_ in zip(stacked["a"], stacked["r"], stacked["b"], stacked["k"])]
    abs_ = [jnp.where(strict, sc[:h2, :h2], 0.0) for sc in scs]
    akvs = [_dg(jnp.where(strict, sc[:h2, h2:], 0.0).astype(BF16), v_)
            for sc, v_ in zip(scs, stacked["v"])]
    rbks = [jnp.concatenate([jnp.where(incl, sc[h2:, :h2], 0.0),
                             jnp.where(incl, sc[h2:, h2:], 0.0)], axis=1).astype(BF16)
            for sc in scs]
    tinvs = _inv_unit_lower(abs_, eye, m16, m32, m64)
    wus = [_dg(t.astype(BF16), jnp.concatenate([a_, akv.astype(BF16)], axis=1))
           for t, a_, akv in zip(tinvs, stacked["a"], akvs)]
    wrs = [jnp.concatenate([wu[:, :LANES].astype(BF16), r_], axis=0)
           for wu, r_ in zip(wus, stacked["r"])]

    for c in range(tile // CHUNK):
        idx = [c * n_pairs + pr for pr in range(n_pairs)]
        gts = [st_ref[pr] for pr in range(n_pairs)]
        wrgs = [_dg(wrs[i], g_.astype(BF16), NT) for i, g_ in zip(idx, gts)]
        ums = [(wrg[:h2] + wus[i][:, LANES:]).astype(BF16) for i, wrg in zip(idx, wrgs)]
        uvs = [jnp.concatenate([um, stacked["v"][i]], axis=0) for i, um in zip(idx, ums)]
        ys = [wrg[h2:] + _dg(rbks[i], uv) for i, wrg, uv in zip(idx, wrgs, uvs)]
        for pr in range(n_pairs):
            cumc = cum_s[rows_of(c), lanes_of(pr)]
            cl = cumc[CHUNK - 1:CHUNK, :]
            dec = jnp.exp(cl - cumc)
            bkd = jnp.concatenate([_stack_heads(bb_s[rows_of(c), lanes_of(pr)] * dec, lane),
                                   _stack_heads(k2_s[rows_of(c), lanes_of(pr)] * dec, lane)],
                                  axis=0).astype(BF16)
            st_ref[pr] = gts[pr] * jnp.exp(cl) + _dg(uvs[pr], bkd, TN)
            y_s[rows_of(c), lanes_of(pr)] = ys[pr][:CHUNK] + ys[pr][CHUNK:]

    y = y_s[...]
    inv_n = 1.0 / RWKV_HEAD
    mu_h = seg_sum(y) * inv_n
    yc = y - mu_h
    var = seg_sum(yc * yc) * inv_n
    yn = yc * lax.rsqrt(var + RWKV_GN_EPS) * gng_ref[...] + gnb_ref[...]
    o_ref[0] = (yn + bon_s[...]) * g_s[...]


def _rwkv_layer(x, mod, w, mu, w0, wup, a0, aup, gup, k_k, k_a, r_k, gn_g, gn_b, tile=256):
    b, s, _ = x.shape
    d = RWKV_DIM
    vec = lambda: _const_spec((1, d))
    vmem = (4 * tile * D_MODEL * 4 + 4 * tile * d * 4 + D_MODEL * RWKV_IN * 2
            + 11 * tile * d * 4 + 8 * tile * RWKV_IN * 4 + (12 << 20))
    return pl.pallas_call(
        functools.partial(_rwkv_kernel, tile=tile),
        grid=(b, s // tile),
        in_specs=[pl.BlockSpec((1, tile, D_MODEL), lambda i, j: (i, j, 0)),
                  pl.BlockSpec((1, 6, D_MODEL), lambda i, j: (i, 0, 0)),
                  _const_spec((D_MODEL, RWKV_IN)),
                  _const_spec((1, RWKV_IN)),
                  vec(), _const_spec((LANES, d)), vec(), _const_spec((LANES, d)),
                  _const_spec((LANES, d)), vec(), vec(), vec(), vec(), vec()],
        out_specs=pl.BlockSpec((1, tile, d), lambda i, j: (i, j, 0)),
        out_shape=jax.ShapeDtypeStruct((b, s, d), F32),
        scratch_shapes=[pltpu.VMEM((RWKV_HEADS // 2, LANES, LANES), F32),
                        pltpu.VMEM((8, RWKV_IN), F32)]
                       + [pltpu.VMEM((tile, d), BF16) for _ in range(5)]
                       + [pltpu.VMEM((tile, d), F32) for _ in range(6)],
        compiler_params=_cparams(("parallel", "arbitrary"), vmem),
        name="rwkv7_mixer",
    )(x, mod, w, mu, w0, wup, a0, aup, gup, k_k, k_a, r_k, gn_g, gn_b)


def _qkv_kernel(x_ref, mod_ref, w_ref, *refs, tile):
    out_refs, p_s = refs[:-1], refs[-1]
    n_grp = 3 * ATT_DIM // LANES
    per = ATT_DIM // LANES
    n_half = 2
    half = tile // n_half
    ps = []
    for hf in range(n_half):
        x = x_ref[0, hf * half:(hf + 1) * half, :]
        u = (x * (1.0 + mod_ref[0, 1:2, :]) + mod_ref[0, 0:1, :]).astype(BF16)
        ps.append(_dg(u, w_ref[...]))
    for hf in range(n_half):
        for g in range(n_grp):
            p_s[g, hf * half:(hf + 1) * half, :] = ps[hf][:, g * LANES:(g + 1) * LANES]
        for pi, dil in enumerate(DILATIONS):
            q_ref, k_ref, v_ref = out_refs[3 * pi:3 * pi + 3]
            n_out = half // dil
            dst = slice(hf * n_out, (hf + 1) * n_out)
            for r in range(dil):
                rows = pl.ds(hf * half + r, n_out, stride=dil) if dil > 1 else \
                    slice(hf * half, (hf + 1) * half)
                grp = lambda t: jnp.concatenate([p_s[t * per + g, rows, :] for g in range(per)],
                                                axis=1)
                q_ref[0, r, dst, :] = (grp(0) * (LOG2_E * ATT_HEAD ** -0.5)).astype(BF16)
                k_ref[0, r, dst, :] = grp(1).astype(BF16)
                v_ref[0, r, dst, :] = grp(2).astype(BF16)


def _qkv_layer(x, mod, w, tile=512):
    b, s, _ = x.shape
    out_shape, out_specs = [], []
    for dil in DILATIONS:
        for _ in range(3):
            out_shape.append(jax.ShapeDtypeStruct((b, dil, s // dil, ATT_DIM), BF16))
            out_specs.append(pl.BlockSpec((1, dil, tile // dil, ATT_DIM), lambda i, j: (i, 0, j, 0)))
    vmem = (4 * tile * D_MODEL * 4 + D_MODEL * 3 * ATT_DIM * 2 + 4 * tile * 3 * ATT_DIM * 4
            + 2 * 9 * tile * ATT_DIM * 2 + (4 << 20))
    outs = pl.pallas_call(
        functools.partial(_qkv_kernel, tile=tile),
        grid=(b, s // tile),
        in_specs=[pl.BlockSpec((1, tile, D_MODEL), lambda i, j: (i, j, 0)),
                  pl.BlockSpec((1, 6, D_MODEL), lambda i, j: (i, 0, 0)),
                  _const_spec((D_MODEL, 3 * ATT_DIM))],
        out_specs=out_specs,
        out_shape=out_shape,
        scratch_shapes=[pltpu.VMEM((3 * ATT_DIM // LANES, tile, LANES), F32)],
        compiler_params=_cparams(("parallel", "parallel"), vmem),
        name="att_qkv",
    )(x, mod, w)
    return [outs[3 * pi:3 * pi + 3] for pi in range(len(DILATIONS))]


ATT_SPAN = 2048


def _att_kernel(q_ref, kp_ref, kc_ref, vp_ref, vc_ref, o_ref, l_ref, kbuf, vbuf, *, dilation):
    n_sub = ATT_SPAN // dilation
    n_blk = n_sub // ATT_BLOCK
    span = pl.program_id(1)
    kbuf[:, :ATT_BLOCK] = kp_ref[0]
    kbuf[:, ATT_BLOCK:] = kc_ref[0]
    vbuf[:, :ATT_BLOCK] = vp_ref[0]
    vbuf[:, ATT_BLOCK:] = vc_ref[0]

    qi = lax.broadcasted_iota(jnp.int32, (ATT_BLOCK, 2 * ATT_BLOCK), 0)
    kj = lax.broadcasted_iota(jnp.int32, (ATT_BLOCK, 2 * ATT_BLOCK), 1)
    steps = qi + ATT_BLOCK - kj
    window = (steps >= 0) & (steps <= ATT_BLOCK)
    dist = (steps * dilation).astype(F32)
    lane_q = lax.broadcasted_iota(jnp.int32, (ATT_BLOCK, LANES), 1)
    zero = jnp.zeros((), BF16)
    heads = range(ATT_HEADS)
    biases = [jnp.where(window, dist * -(LOG2_E * 2.0 ** (-8.0 * (h + 1) / ATT_HEADS)), -jnp.inf)
              for h in heads]

    lane_k = lax.broadcasted_iota(jnp.int32, (2 * ATT_BLOCK, LANES), 1)
    one = jnp.ones((), BF16)
    pair_of = lambda t, h: t[:, (h // 2) * LANES:(h // 2 + 1) * LANES]
    own = lambda lane, h: (lane >= ATT_HEAD) if h % 2 else (lane < ATT_HEAD)

    def scores(uidx):
        r = uidx >> (n_blk.bit_length() - 1)
        n = uidx & (n_blk - 1)
        row0 = pl.multiple_of(n * ATT_BLOCK, ATT_BLOCK)
        q = q_ref[0, r, pl.ds(row0, ATT_BLOCK), :]
        kk = kbuf[r, pl.ds(row0, 2 * ATT_BLOCK), :]
        ss = [_dg(jnp.where(own(lane_q, h), pair_of(q, h), zero), pair_of(kk, h), NT)
              for h in heads]
        return r, n, row0, ss

    def finish(first_span, r, n, row0, ss):
        vv = vbuf[r, pl.ds(row0, 2 * ATT_BLOCK), :]
        if first_span:
            first_key = jnp.where(n == 0, ATT_BLOCK, 0)
            head_mask = jnp.where(kj >= first_key, 0.0, -jnp.inf)
            ss = [s + (biases[h] + head_mask) for h, s in zip(heads, ss)]
        else:
            ss = [s + biases[h] for h, s in zip(heads, ss)]
        ms = [jnp.max(s, axis=-1, keepdims=True) for s in ss]
        es = [jnp.exp2(s - m).astype(BF16) for s, m in zip(ss, ms)]
        pvs = [_dg(e, jnp.where(own(lane_k, h), pair_of(vv, h), one)) for h, e in zip(heads, es)]
        m_blk = jnp.zeros((ATT_BLOCK, LANES), F32)
        d_blk = jnp.ones((ATT_BLOCK, LANES), F32)
        tok0 = n * (ATT_BLOCK * dilation) + r
        if dilation == 1:
            rows = pl.ds(pl.multiple_of(tok0, ATT_BLOCK), ATT_BLOCK)
        else:
            rows = pl.ds(tok0, ATT_BLOCK, stride=dilation)
        for p in range(ATT_HEADS // 2):
            even, odd = pvs[2 * p], pvs[2 * p + 1]
            num = jnp.where(lane_q < ATT_HEAD, even, odd)
            den_sw = jnp.where(lane_q < ATT_HEAD, odd, even)
            den = pltpu.roll(den_sw, ATT_HEAD, 1)
            o_ref[0, p, rows, :] = num / den
            m_blk = jnp.where(lane_q == 2 * p, ms[2 * p], m_blk)
            m_blk = jnp.where(lane_q == 2 * p + 1, ms[2 * p + 1], m_blk)
            d_blk = jnp.where(lane_q == 2 * p, den, d_blk)
            d_blk = jnp.where(lane_q == 2 * p + 1, den_sw, d_blk)
        l_ref[0, rows, :] = (m_blk + jnp.log2(d_blk)) * LN_2

    per_trip = 4

    def make_body(first_span):
        def body(i, carry):
            us = [scores(per_trip * i + t) for t in range(2)]
            for t in range(per_trip):
                if t + 2 < per_trip:
                    us.append(scores(per_trip * i + t + 2))
                finish(first_span, *us[t])
            return carry
        return body

    n_units = dilation * n_blk

    @pl.when(span == 0)
    def _():
        lax.fori_loop(0, n_units // per_trip, make_body(True), 0)

    @pl.when(span != 0)
    def _():
        lax.fori_loop(0, n_units // per_trip, make_body(False), 0)


def _att_pattern(q, k, v, dilation):
    b, _, n_res, _ = q.shape
    s = dilation * n_res
    n_sub = ATT_SPAN // dilation
    n_blk = n_sub // ATT_BLOCK
    cur = pl.BlockSpec((1, dilation, n_sub, ATT_DIM), lambda i, j: (i, 0, j, 0))
    prev = pl.BlockSpec((1, dilation, ATT_BLOCK, ATT_DIM),
                        lambda i, j: (i, 0, jnp.maximum(j * n_blk - 1, 0), 0))
    buf = pltpu.VMEM((dilation, ATT_BLOCK + n_sub, ATT_DIM), BF16)
    vmem = (2 * (3 * ATT_SPAN + 2 * dilation * ATT_BLOCK) * ATT_DIM * 2
            + 2 * dilation * (ATT_BLOCK + n_sub) * ATT_DIM * 2
            + 2 * ATT_SPAN * (ATT_DIM + LANES) * 4 + (8 << 20))
    return pl.pallas_call(
        functools.partial(_att_kernel, dilation=dilation),
        grid=(b, s // ATT_SPAN),
        in_specs=[cur, prev, cur, prev, cur],
        out_specs=[pl.BlockSpec((1, ATT_HEADS // 2, ATT_SPAN, LANES), lambda i, j: (i, 0, j, 0)),
                   pl.BlockSpec((1, ATT_SPAN, LANES), lambda i, j: (i, j, 0))],
        out_shape=[jax.ShapeDtypeStruct((b, ATT_HEADS // 2, s, LANES), F32),
                   jax.ShapeDtypeStruct((b, s, LANES), F32)],
        scratch_shapes=[buf, buf],
        compiler_params=_cparams(("parallel", "parallel"), vmem),
        name=f"dilated_att_d{dilation}",
    )(q, k, k, v, v)


def _merge_kernel(x_ref, mod_ref, oa_ref, ob_ref, o1_ref, o4_ref, o16_ref, l1_ref, l4_ref, l16_ref,
                  wg_ref, wb_ref, wo_ref, g_ref, b_ref, out_ref):
    hi = lax.broadcasted_iota(jnp.int32, (LANES, ATT_DIM), 0)
    hj = lax.broadcasted_iota(jnp.int32, (LANES, ATT_DIM), 1)
    expand = jnp.where(hi == (hj >> 6), 1.0, 0.0).astype(BF16)
    n_half = 2
    rows = x_ref.shape[1] // n_half
    sls = [slice(i * rows, (i + 1) * rows) for i in range(n_half)]
    lane_cat = lambda ref, sl: jnp.concatenate([ref[0, p, sl, :] for p in range(ATT_HEADS // 2)],
                                               axis=1)
    xs = [x_ref[0, sl, :] for sl in sls]
    us = [(x * (1.0 + mod_ref[0, 1:2, :]) + mod_ref[0, 0:1, :]).astype(BF16) for x in xs]
    gates = [_sigmoid(_dg(u, wg_ref[...])) for u in us]
    o_cs = []
    for sl in sls:
        l1, l4, l16 = l1_ref[0, sl, :], l4_ref[0, sl, :], l16_ref[0, sl, :]
        m = jnp.maximum(jnp.maximum(l1, l4), l16)
        e1, e4, e16 = jnp.exp(l1 - m), jnp.exp(l4 - m), jnp.exp(l16 - m)
        inv = 1.0 / (e1 + e4 + e16)
        o_cs.append(_dot_exact_rhs(e1 * inv, expand) * lane_cat(o1_ref, sl)
                    + _dot_exact_rhs(e4 * inv, expand) * lane_cat(o4_ref, sl)
                    + _dot_exact_rhs(e16 * inv, expand) * lane_cat(o16_ref, sl))
    pa = [_dot1(oa_ref[0, sl, :], wb_ref[0]) for sl in sls]
    pb = [_dot1(ob_ref[0, sl, :], wb_ref[1]) for sl in sls]
    pc = [_dot1(o_c, wb_ref[2]) for o_c in o_cs]
    merged = [g_[:, :D_MODEL] * a_ + g_[:, D_MODEL:2 * D_MODEL] * b_ + g_[:, 2 * D_MODEL:] * c_
              for g_, a_, b_, c_ in zip(gates, pa, pb, pc)]
    hs = [_dot1(mg, wo_ref[...]) for mg in merged]
    for sl, x, h in zip(sls, xs, hs):
        z = DEEPNORM_ALPHA * x + (1.0 + mod_ref[0, 2:3, :]) * h
        out_ref[0, sl, :] = _layer_norm(z, g_ref[...], b_ref[...])


def _merge_layer(x, mod, oa, ob, oc, lses, wg, wb, wo, g, bta, tile=512):
    b, s, _ = x.shape
    xs = pl.BlockSpec((1, tile, D_MODEL), lambda i, j: (i, j, 0))
    bs = pl.BlockSpec((1, tile, ATT_DIM), lambda i, j: (i, j, 0))
    ls = pl.BlockSpec((1, tile, LANES), lambda i, j: (i, j, 0))
    cs = pl.BlockSpec((1, ATT_HEADS // 2, tile, LANES), lambda i, j: (i, 0, j, 0))
    vmem = (4 * tile * D_MODEL * 4 + 10 * tile * ATT_DIM * 4 + 6 * tile * LANES * 4
            + (3 * D_MODEL * D_MODEL + 3 * ATT_DIM * D_MODEL + D_MODEL * D_MODEL) * 2
            + 6 * tile * 3 * D_MODEL * 4 + (4 << 20))
    return pl.pallas_call(
        _merge_kernel,
        grid=(b, s // tile),
        in_specs=[xs, pl.BlockSpec((1, 6, D_MODEL), lambda i, j: (i, 0, 0)),
                  bs, bs, cs, cs, cs, ls, ls, ls,
                  _const_spec((D_MODEL, N_BRANCH * D_MODEL)),
                  _const_spec((N_BRANCH, ATT_DIM, D_MODEL)),
                  _const_spec((D_MODEL, D_MODEL)),
                  _const_spec((1, D_MODEL)), _const_spec((1, D_MODEL))],
        out_specs=xs,
        out_shape=jax.ShapeDtypeStruct((b, s, D_MODEL), F32),
        compiler_params=_cparams(("parallel", "parallel"), vmem),
        name="merge_ln1",
    )(x, mod, oa, ob, oc[0], oc[1], oc[2], lses[0], lses[1], lses[2], wg, wb, wo, g, bta)


def _ffn_kernel(x_ref, mod_ref, w1_ref, w2_ref, g_ref, b_ref, out_ref):
    n_half = 2
    rows = x_ref.shape[1] // n_half
    sls = [slice(i * rows, (i + 1) * rows) for i in range(n_half)]
    xs = [x_ref[0, sl, :] for sl in sls]
    us = [(x * (1.0 + mod_ref[0, 4:5, :]) + mod_ref[0, 3:4, :]).astype(BF16) for x in xs]
    hs = [_dg(u, w1_ref[...]) for u in us]
    acts = [(_silu(h[:, :FFN_HIDDEN]) * h[:, FFN_HIDDEN:]).astype(BF16) for h in hs]
    ys = [_dg(act, w2_ref[...]) for act in acts]
    for sl, x, y in zip(sls, xs, ys):
        z = DEEPNORM_ALPHA * x + (1.0 + mod_ref[0, 5:6, :]) * y
        out_ref[0, sl, :] = _layer_norm(z, g_ref[...], b_ref[...])


def _ffn_layer(x, mod, w1, w2, g, bta, tile=512):
    b, s, _ = x.shape
    xs = pl.BlockSpec((1, tile, D_MODEL), lambda i, j: (i, j, 0))
    vmem = (4 * tile * D_MODEL * 4 + 3 * D_MODEL * FFN_HIDDEN * 2
            + 4 * tile * 2 * FFN_HIDDEN * 4 + (4 << 20))
    return pl.pallas_call(
        _ffn_kernel,
        grid=(b, s // tile),
        in_specs=[xs, pl.BlockSpec((1, 6, D_MODEL), lambda i, j: (i, 0, 0)),
                  _const_spec((D_MODEL, 2 * FFN_HIDDEN)),
                  _const_spec((FFN_HIDDEN, D_MODEL)),
                  _const_spec((1, D_MODEL)), _const_spec((1, D_MODEL))],
        out_specs=xs,
        out_shape=jax.ShapeDtypeStruct((b, s, D_MODEL), F32),
        compiler_params=_cparams(("parallel", "parallel"), vmem),
        name="ffn_ln2",
    )(x, mod, w1, w2, g, bta)


def _pad_rows(m, rows, offset=0):
    out = jnp.zeros((rows,) + m.shape[1:], m.dtype)
    return out.at[offset:offset + m.shape[0]].set(m)


def _mixer_layer(x, mod, w_in, gla_w_alpha, gla_b_alpha, gla_norm_g, rwkv_mu, rwkv_w0, rwkv_w_up,
                 rwkv_a0, rwkv_a_up, rwkv_g_up, rwkv_k_k, rwkv_k_a, rwkv_r_k, rwkv_gn_g,
                 rwkv_gn_b, w_branch, w_out, ln_g, ln_b):
    row = lambda t: t.reshape(1, -1)
    o0 = 0
    gq_end = 2 * GLA_DK + GLA_DV
    w_gla = jnp.concatenate([w_in[:, :gq_end], w_in[:, gq_end + GLA_GATE_RANK:GLA_IN]],
                            axis=1).astype(BF16)
    w_ga = jnp.zeros((D_MODEL, LANES), F32).at[:, :GLA_GATE_RANK].set(
        w_in[:, gq_end:gq_end + GLA_GATE_RANK]).astype(BF16)
    o0 += GLA_IN
    w_rwkv = w_in[:, o0:o0 + RWKV_IN].astype(BF16)
    o0 += RWKV_IN
    w_att = w_in[:, o0:o0 + 3 * ATT_DIM].astype(BF16)
    o0 += 3 * ATT_DIM
    w_gate = w_in[:, o0:].astype(BF16)

    o_a = _gla_layer(x, mod, w_gla, w_ga, _pad_rows(gla_w_alpha, LANES), row(gla_b_alpha),
                     row(gla_norm_g))
    o_b = _rwkv_layer(x, mod, w_rwkv, row(rwkv_mu), row(rwkv_w0),
                      _pad_rows(rwkv_w_up, LANES), row(rwkv_a0),
                      _pad_rows(rwkv_a_up, LANES, RWKV_DECAY_RANK), rwkv_g_up,
                      row(rwkv_k_k), row(rwkv_k_a), row(rwkv_r_k), row(rwkv_gn_g),
                      row(rwkv_gn_b))
    qkvs = _qkv_layer(x, mod, w_att)
    res = [_att_pattern(*qkv, dil) for qkv, dil in zip(qkvs, DILATIONS)]
    return _merge_layer(x, mod, o_a, o_b, [o for o, _ in res], [l for _, l in res],
                        w_gate, w_branch.astype(BF16), w_out.astype(BF16), row(ln_g), row(ln_b))


def kernel(x, c, w_ada, b_ada, w_in, gla_w_alpha, gla_b_alpha, gla_norm_g, rwkv_mu, rwkv_w0,
           rwkv_w_up, rwkv_a0, rwkv_a_up, rwkv_g_up, rwkv_k_k, rwkv_k_a, rwkv_r_k, rwkv_gn_g,
           rwkv_gn_b, w_branch, w_out, ln1_g, ln1_b, ffn_w1, ffn_w2, ln2_g, ln2_b):
    mod = _modulation(c, w_ada, b_ada)
    for l in range(DEPTH):
        x = _mixer_layer(x, mod[l], w_in[l], gla_w_alpha[l], gla_b_alpha[l], gla_norm_g[l],
                         rwkv_mu[l], rwkv_w0[l], rwkv_w_up[l], rwkv_a0[l], rwkv_a_up[l],
                         rwkv_g_up[l], rwkv_k_k[l], rwkv_k_a[l], rwkv_r_k[l], rwkv_gn_g[l],
                         rwkv_gn_b[l], w_branch[l], w_out[l], ln1_g[l], ln1_b[l])
        x = _ffn_layer(x, mod[l], ffn_w1[l].astype(BF16), ffn_w2[l].astype(BF16),
                       ln2_g[l].reshape(1, -1), ln2_b[l].reshape(1, -1))
    return x
```

```python
import functools
import math

import jax
import jax.numpy as jnp
from jax import lax
from jax.experimental import pallas as pl
from jax.experimental.pallas import tpu as pltpu

F32 = jnp.float32
BF16 = jnp.bfloat16

D_MODEL = 1024
DEPTH = 4
GLA_HEADS, GLA_HEAD_K, GLA_HEAD_V = 4, 64, 128
GLA_DK, GLA_DV = GLA_HEADS * GLA_HEAD_K, GLA_HEADS * GLA_HEAD_V
GLA_GATE_RANK = 16
GLA_GATE_TAU = 16.0
GLA_NORM_EPS = 1e-5
RWKV_HEADS, RWKV_HEAD = 8, 64
RWKV_DIM = RWKV_HEADS * RWKV_HEAD
RWKV_DECAY_RANK, RWKV_ICLR_RANK, RWKV_GATE_RANK = 64, 64, 128
RWKV_IN = 3 * RWKV_DIM + RWKV_DECAY_RANK + RWKV_ICLR_RANK + RWKV_GATE_RANK
RWKV_GN_EPS = 64e-5
ATT_HEADS, ATT_HEAD = 8, 64
ATT_DIM = ATT_HEADS * ATT_HEAD
ATT_BLOCK = 128
DILATIONS = (1, 4, 16)
N_BRANCH = 3
FFN_HIDDEN = 2816
LN_EPS = 1e-5
DEEPNORM_ALPHA = (2 * DEPTH) ** 0.25
GLA_IN = 2 * GLA_DK + GLA_DV + GLA_GATE_RANK + GLA_DV

LOG2_E = math.log2(math.e)
LN_2 = math.log(2.0)
LANES = 128
CHUNK = 64
VMEM_LIMIT_CAP = 60000 * 1024

NN = ((1,), (0,))
NT = ((1,), (1,))
TN = ((0,), (0,))


def _dg(a, b, dims=NN):
    return lax.dot_general(a, b, (dims, ((), ())), preferred_element_type=F32)


def _dot1(a, b, dims=NN):
    return _dg(a.astype(BF16), b.astype(BF16), dims)


def _split2(a):
    hi = a.astype(BF16)
    lo = (a - hi.astype(F32)).astype(BF16)
    return hi, lo


def _dot3(a, b, dims=NN):
    ah, al = _split2(a)
    bh, bl = _split2(b)
    return _dg(ah, bh, dims) + (_dg(ah, bl, dims) + _dg(al, bh, dims))


def _dot_exact_rhs(a, m_bf16, dims=NN, parts=2):
    acc = None
    rem = a
    for _ in range(parts):
        hi = rem.astype(BF16)
        term = _dg(hi, m_bf16, dims)
        acc = term if acc is None else acc + term
        rem = rem - hi.astype(F32)
    return acc


def _dot_exact_lhs(m_bf16, a, parts=2):
    acc = None
    rem = a
    for _ in range(parts):
        hi = rem.astype(BF16)
        term = _dg(m_bf16, hi)
        acc = term if acc is None else acc + term
        rem = rem - hi.astype(F32)
    return acc


def _sigmoid(x):
    return 1.0 / (1.0 + jnp.exp(-x))


def _silu(x):
    return x * _sigmoid(x)


def _log_sigmoid(x):
    return jnp.minimum(x, 0.0) - jnp.log(1.0 + jnp.exp(-jnp.abs(x)))


def _layer_norm(z, g, b):
    mu = jnp.mean(z, axis=-1, keepdims=True)
    zc = z - mu
    var = jnp.mean(zc * zc, axis=-1, keepdims=True)
    return zc * lax.rsqrt(var + LN_EPS) * g + b


def _stack_heads(x, lane):
    lo = jnp.where(lane < RWKV_HEAD, x, 0.0)
    hi = jnp.where(lane >= RWKV_HEAD, x, 0.0)
    return jnp.concatenate([lo, hi], axis=0)


def _cparams(sem, vmem_bytes):
    return pltpu.CompilerParams(dimension_semantics=sem,
                                vmem_limit_bytes=int(min(vmem_bytes, VMEM_LIMIT_CAP)))


def _layer_param(l, shape):
    zeros = (0,) * len(shape)
    return pl.BlockSpec((None,) + tuple(shape), lambda *_: (l,) + zeros,
                        pipeline_mode=pl.Buffered(1))


def _mod_spec(l):
    return pl.BlockSpec((None, 1, 6, D_MODEL), lambda i, j: (l, i, 0, 0))


def _mod_kernel(ct_ref, w_ref, b_ref, o_ref, *, batch):
    s = _silu(ct_ref[...])
    tn = w_ref.shape[2]
    o_ref[0] = jnp.zeros(o_ref.shape[1:], F32)
    for bi in range(batch):
        sb = jnp.broadcast_to(s[:, bi:bi + 1], (D_MODEL, LANES))
        for j in range(tn // LANES):
            sl = slice(j * LANES, (j + 1) * LANES)
            acc = jnp.sum(w_ref[0, :, sl] * sb, axis=0, keepdims=True)
            o_ref[0, bi:bi + 1, sl] = acc + b_ref[0, :, sl]


def _modulation(c, w_ada, b_ada):
    n_l = w_ada.shape[0]
    b = c.shape[0]
    rows = 8
    assert b <= rows
    c_t = jnp.zeros((D_MODEL, rows), F32).at[:, :b].set(c.T)
    tn = 1536
    out = pl.pallas_call(
        functools.partial(_mod_kernel, batch=b),
        grid=(n_l, 6 * D_MODEL // tn),
        in_specs=[pl.BlockSpec((D_MODEL, rows), lambda l, j: (0, 0)),
                  pl.BlockSpec((1, D_MODEL, tn), lambda l, j: (l, 0, j)),
                  pl.BlockSpec((1, 1, tn), lambda l, j: (l, 0, j))],
        out_specs=pl.BlockSpec((1, rows, tn), lambda l, j: (l, 0, j)),
        out_shape=jax.ShapeDtypeStruct((n_l, rows, 6 * D_MODEL), F32),
        compiler_params=_cparams(("parallel", "parallel"), 4 * D_MODEL * tn * 4),
        name="adaln_mod",
    )(c_t, w_ada, b_ada.reshape(n_l, 1, 6 * D_MODEL))
    return out[:, :b].reshape(n_l, b, 6, D_MODEL)


def _gla_kernel(x_ref, mod_ref, wm_ref, wog_ref, wga_ref, wal_ref, bal_ref, ng_ref, o_ref,
                st_ref, p_s, cum_s, o_s, *, tile):
    @pl.when(pl.program_id(1) == 0)
    def _():
        st_ref[...] = jnp.zeros_like(st_ref)

    n_qkv = 2 * GLA_DK + GLA_DV
    n_half = 2
    hrows = tile // n_half
    sls = [slice(i * hrows, (i + 1) * hrows) for i in range(n_half)]
    us = [(x_ref[0, sl, :] * (1.0 + mod_ref[0, 1:2, :]) + mod_ref[0, 0:1, :]).astype(BF16)
          for sl in sls]
    a_los = []
    for sl, u in zip(sls, us):
        p_s[sl, :n_qkv] = _dg(u, wm_ref[...])
        p_s[sl, n_qkv:] = _dg(u, wog_ref[...])
        a_los.append(_dg(u, wga_ref[...]))
    ri = lax.broadcasted_iota(jnp.int32, (hrows, hrows), 0)
    ci = lax.broadcasted_iota(jnp.int32, (hrows, hrows), 1)
    ltri = jnp.where(((ri >> 6) == (ci >> 6)) & (ci <= ri), 1.0, 0.0).astype(BF16)
    for sl, a_lo in zip(sls, a_los):
        z = _dot3(a_lo, wal_ref[...]) + bal_ref[...]
        log_a = _log_sigmoid(z) * (1.0 / GLA_GATE_TAU)
        cum_s[sl, :] = _dot_exact_lhs(ltri, log_a)

    lane = lax.broadcasted_iota(jnp.int32, (CHUNK, LANES), 1)
    i2 = lax.broadcasted_iota(jnp.int32, (2 * CHUNK, 2 * CHUNK), 0)
    j2 = lax.broadcasted_iota(jnp.int32, (2 * CHUNK, 2 * CHUNK), 1)
    causal = ((i2 >> 6) == (j2 >> 6)) & (j2 <= i2)
    scale = GLA_HEAD_K ** -0.5

    n_chunks = tile // CHUNK
    n_pairs = GLA_HEADS // 2
    units = [(c, pr) for c in range(n_chunks) for pr in range(n_pairs)]

    qsms, ksms, kdsms, vsts, decs = [], [], [], [], []
    for c, pr in units:
        rows = slice(c * CHUNK, (c + 1) * CHUNK)
        lo = pr * LANES
        cumc = cum_s[rows, lo:lo + LANES]
        qc = p_s[rows, lo:lo + LANES]
        kc = p_s[rows, GLA_DK + lo:GLA_DK + lo + LANES]
        vbase = 2 * GLA_DK + 2 * pr * GLA_HEAD_V
        vsts.append(jnp.concatenate([p_s[rows, vbase:vbase + GLA_HEAD_V],
                                     p_s[rows, vbase + GLA_HEAD_V:vbase + 2 * GLA_HEAD_V]],
                                    axis=0).astype(BF16))
        cl = cumc[CHUNK - 1:CHUNK, :]
        qsms.append(_stack_heads(qc * scale * jnp.exp(cumc), lane).astype(BF16))
        ksms.append(_stack_heads(kc * jnp.exp(-cumc), lane).astype(BF16))
        kdsms.append(_stack_heads(kc * jnp.exp(cl - cumc), lane).astype(BF16))
        decs.append(jnp.exp(cl))
    atts = [jnp.where(causal, _dg(q_, k_, NT), 0.0).astype(BF16) for q_, k_ in zip(qsms, ksms)]
    o_intra = [_dg(a_, v_) for a_, v_ in zip(atts, vsts)]
    d_states = [_dg(v_, kd_, TN) for v_, kd_ in zip(vsts, kdsms)]

    gts = [None] * len(units)
    for pr in range(n_pairs):
        g = st_ref[pr]
        for c in range(n_chunks):
            i = c * n_pairs + pr
            gts[i] = g.astype(BF16)
            g = g * decs[i] + d_states[i]
        st_ref[pr] = g

    for i, (c, pr) in enumerate(units):
        o_st = o_intra[i] + _dg(qsms[i], gts[i], NT)
        rows = slice(c * CHUNK, (c + 1) * CHUNK)
        ob = 2 * pr * GLA_HEAD_V
        o_s[rows, ob:ob + GLA_HEAD_V] = o_st[:CHUNK]
        o_s[rows, ob + GLA_HEAD_V:ob + 2 * GLA_HEAD_V] = o_st[CHUNK:]

    og_base = 2 * GLA_DK + GLA_DV
    for h in range(GLA_HEADS):
        sl = slice(h * GLA_HEAD_V, (h + 1) * GLA_HEAD_V)
        oh = o_s[:, sl]
        on = oh * lax.rsqrt(jnp.mean(oh * oh, axis=-1, keepdims=True) + GLA_NORM_EPS) * ng_ref[...]
        og = p_s[:, og_base + h * GLA_HEAD_V:og_base + (h + 1) * GLA_HEAD_V]
        o_ref[0, :, sl] = on * _silu(og)


def _gla_layer(l, x, mod, wm, wog, wga, wal, bal, ng, tile=512):
    b, s, _ = x.shape
    n_qkv = 2 * GLA_DK + GLA_DV
    n_main = n_qkv + GLA_DV
    vmem = (4 * tile * D_MODEL * 4 + 4 * tile * GLA_DV * 4 + D_MODEL * (n_main + LANES) * 2
            + tile * (n_main + GLA_DK + GLA_DV) * 4 + 6 * tile * n_main * 4 + (8 << 20))
    return pl.pallas_call(
        functools.partial(_gla_kernel, tile=tile),
        grid=(b, s // tile),
        in_specs=[pl.BlockSpec((1, tile, D_MODEL), lambda i, j: (i, j, 0)),
                  _mod_spec(l),
                  _layer_param(l,(D_MODEL, n_qkv)),
                  _layer_param(l,(D_MODEL, GLA_DV)),
                  _layer_param(l,(D_MODEL, LANES)),
                  _layer_param(l,(LANES, GLA_DK)),
                  _layer_param(l,(1, GLA_DK)),
                  _layer_param(l,(1, GLA_HEAD_V))],
        out_specs=pl.BlockSpec((1, tile, GLA_DV), lambda i, j: (i, j, 0)),
        out_shape=jax.ShapeDtypeStruct((b, s, GLA_DV), F32),
        scratch_shapes=[pltpu.VMEM((GLA_HEADS // 2, GLA_HEAD_V, LANES), F32),
                        pltpu.VMEM((tile, n_main), F32),
                        pltpu.VMEM((tile, GLA_DK), F32),
                        pltpu.VMEM((tile, GLA_DV), F32)],
        compiler_params=_cparams(("parallel", "arbitrary"), vmem),
        name="gla_mixer",
    )(x, mod, wm, wog, wga, wal, bal, ng)


def _inv_unit_lower(ns, eye, m16, m32, m64):
    ds = [jnp.where(m16, n, 0.0) for n in ns]
    xs = [eye + d for d in ds]
    pws = [d.astype(BF16) for d in ds]
    pws = [_dg(p, p).astype(BF16) for p in pws]
    for level in range(2):
        prods = [_dg(p, jnp.concatenate([x.astype(BF16), p], axis=1)) for x, p in zip(xs, pws)]
        xs = [x + pr[:, :LANES] for x, pr in zip(xs, prods)]
        pws = [pr[:, LANES:].astype(BF16) for pr in prods]
    xs = [x + _dg(p, x.astype(BF16)) for x, p in zip(xs, pws)]
    for m, blk in ((m32, 16), (m64, 32)):
        n_rows = xs[0].shape[0]
        lower = [slice(s0, s0 + blk) for s0 in range(blk, n_rows, 2 * blk)]
        upper = [slice(s0, s0 + blk) for s0 in range(0, n_rows, 2 * blk)]
        xbs = [x.astype(BF16) for x in xs]
        xls = [jnp.concatenate([xb[sl] for sl in lower], axis=0) for xb in xbs]
        xos = [_dg(xl, jnp.where(m, n, 0.0).astype(BF16)) for xl, n in zip(xls, ns)]
        upd = [_dg(xo.astype(BF16), xb) for xo, xb in zip(xos, xbs)]
        new = []
        for x, up in zip(xs, upd):
            parts = []
            for i, (su, sl) in enumerate(zip(upper, lower)):
                parts += [x[su], x[sl] + up[i * blk:(i + 1) * blk]]
            new.append(jnp.concatenate(parts, axis=0))
        xs = new
    return xs


def _rwkv_kernel(x_ref, mod_ref, w_ref, mu_ref, w0_ref, wup_ref, a0_ref, aup_ref, gup_ref,
                 kk_ref, ka_ref, rk_ref, gng_ref, gnb_ref, o_ref,
                 st_ref, carry_ref, a_s, b_s, k_s, r_s, v_s, bb_s, k2_s, cum_s, y_s, bon_s, g_s,
                 *, tile):
    @pl.when(pl.program_id(1) == 0)
    def _():
        st_ref[...] = jnp.zeros_like(st_ref)
        carry_ref[...] = jnp.zeros_like(carry_ref)

    d = RWKV_DIM
    bi = lax.broadcasted_iota(jnp.int32, (LANES, LANES), 0)
    bj = lax.broadcasted_iota(jnp.int32, (LANES, LANES), 1)
    same64 = (bi >> 6) == (bj >> 6)
    seg = jnp.where(same64, 1.0, 0.0).astype(BF16)
    ltri = jnp.where(same64 & (bj <= bi), 1.0, 0.0).astype(BF16)

    def seg_sum(t):
        return jnp.concatenate(
            [_dot_exact_rhs(t[:, q * LANES:(q + 1) * LANES], seg) for q in range(d // LANES)],
            axis=1)

    n_half = tile // LANES
    sls = [slice(i * LANES, (i + 1) * LANES) for i in range(n_half)]
    projs = [_dg((x_ref[0, sl, :] * (1.0 + mod_ref[0, 1:2, :]) + mod_ref[0, 0:1, :]).astype(BF16),
                 w_ref[...]) for sl in sls]
    row = lax.broadcasted_iota(jnp.int32, (LANES, 1), 0)
    last = carry_ref[0:1, :]
    for sl, p in zip(sls, projs):
        prev = jnp.where(row == 0, last, pltpu.roll(p, 1, 0))
        last = p[LANES - 1:LANES, :]
        ps = p + (prev - p) * mu_ref[...]
        r = ps[:, 0:d]
        k = ps[:, d:2 * d]
        v = ps[:, 2 * d:3 * d]
        wa_lo = ps[:, 3 * d:3 * d + LANES]
        g_lo = ps[:, 3 * d + LANES:3 * d + 2 * LANES]
        wl = w0_ref[...] + _dot3(jnp.tanh(wa_lo), wup_ref[...])
        lw = -_sigmoid(wl) * math.exp(-0.5)
        a = _sigmoid(a0_ref[...] + _dot1(wa_lo, aup_ref[...]))
        g_s[sl, :] = _dot1(_sigmoid(g_lo), gup_ref[...])
        kk = k * kk_ref[...]
        k2 = k * (1.0 + (a - 1.0) * ka_ref[...])
        kk = kk / jnp.maximum(jnp.sqrt(seg_sum(kk * kk)), 1e-12)
        bb = kk * a
        bon_s[sl, :] = seg_sum(r * k2 * rk_ref[...]) * v
        cum = _dot_exact_lhs(ltri, lw)
        e_neg = jnp.exp(-cum)
        a_s[sl, :] = (-kk * jnp.exp(cum - lw)).astype(BF16)
        b_s[sl, :] = (bb * e_neg).astype(BF16)
        k_s[sl, :] = (k2 * e_neg).astype(BF16)
        r_s[sl, :] = (r * jnp.exp(cum)).astype(BF16)
        v_s[sl, :] = v.astype(BF16)
        bb_s[sl, :] = bb
        k2_s[sl, :] = k2
        cum_s[sl, :] = cum
    carry_ref[0:1, :] = last

    lane = lax.broadcasted_iota(jnp.int32, (CHUNK, LANES), 1)
    strict = same64 & (bj < bi)
    incl = same64 & (bj <= bi)
    m16 = (bi >> 4) == (bj >> 4)
    m32 = ((bi >> 5) == (bj >> 5)) & jnp.logical_not(m16)
    m64 = same64 & ((bi >> 5) != (bj >> 5))
    eye = jnp.where(bi == bj, 1.0, 0.0)
    h2 = 2 * CHUNK

    n_pairs = RWKV_HEADS // 2
    units = [(c, pr) for c in range(tile // CHUNK) for pr in range(n_pairs)]

    def rows_of(c):
        return slice(c * CHUNK, (c + 1) * CHUNK)

    def lanes_of(pr):
        return slice(pr * LANES, (pr + 1) * LANES)

    stacked = {}
    for name, ref in (("a", a_s), ("b", b_s), ("k", k_s), ("r", r_s), ("v", v_s)):
        stacked[name] = [_stack_heads(ref[rows_of(c), lanes_of(pr)], lane) for c, pr in units]
    scs = [_dg(jnp.concatenate([a_, r_], axis=0), jnp.concatenate([b_, k_], axis=0), NT)
           for a_, r_, b_, k_ in zip(stacked["a"], stacked["r"], stacked["b"], stacked["k"])]
    abs_ = [jnp.where(strict, sc[:h2, :h2], 0.0) for sc in scs]
    akvs = [_dg(jnp.where(strict, sc[:h2, h2:], 0.0).astype(BF16), v_)
            for sc, v_ in zip(scs, stacked["v"])]
    rbks = [jnp.concatenate([jnp.where(incl, sc[h2:, :h2], 0.0),
                             jnp.where(incl, sc[h2:, h2:], 0.0)], axis=1).astype(BF16)
            for sc in scs]
    tinvs = _inv_unit_lower(abs_, eye, m16, m32, m64)
    wus = [_dg(t.astype(BF16), jnp.concatenate([a_, akv.astype(BF16)], axis=1))
           for t, a_, akv in zip(tinvs, stacked["a"], akvs)]
    wrs = [jnp.concatenate([wu[:, :LANES].astype(BF16), r_], axis=0)
           for wu, r_ in zip(wus, stacked["r"])]

    for c in range(tile // CHUNK):
        idx = [c * n_pairs + pr for pr in range(n_pairs)]
        gts = [st_ref[pr] for pr in range(n_pairs)]
        wrgs = [_dg(wrs[i], g_.astype(BF16), NT) for i, g_ in zip(idx, gts)]
        ums = [(wrg[:h2] + wus[i][:, LANES:]).astype(BF16) for i, wrg in zip(idx, wrgs)]
        uvs = [jnp.concatenate([um, stacked["v"][i]], axis=0) for i, um in zip(idx, ums)]
        ys = [wrg[h2:] + _dg(rbks[i], uv) for i, wrg, uv in zip(idx, wrgs, uvs)]
        for pr in range(n_pairs):
            cumc = cum_s[rows_of(c), lanes_of(pr)]
            cl = cumc[CHUNK - 1:CHUNK, :]
            dec = jnp.exp(cl - cumc)
            bkd = jnp.concatenate([_stack_heads(bb_s[rows_of(c), lanes_of(pr)] * dec, lane),
                                   _stack_heads(k2_s[rows_of(c), lanes_of(pr)] * dec, lane)],
                                  axis=0).astype(BF16)
            st_ref[pr] = gts[pr] * jnp.exp(cl) + _dg(uvs[pr], bkd, TN)
            y_s[rows_of(c), lanes_of(pr)] = ys[pr][:CHUNK] + ys[pr][CHUNK:]

    y = y_s[...]
    inv_n = 1.0 / RWKV_HEAD
    mu_h = seg_sum(y) * inv_n
    yc = y - mu_h
    var = seg_sum(yc * yc) * inv_n
    yn = yc * lax.rsqrt(var + RWKV_GN_EPS) * gng_ref[...] + gnb_ref[...]
    o_ref[0] = (yn + bon_s[...]) * g_s[...]


def _rwkv_layer(l, x, mod, w, mu, w0, wup, a0, aup, gup, k_k, k_a, r_k, gn_g, gn_b, tile=256):
    b, s, _ = x.shape
    d = RWKV_DIM
    vec = lambda: _layer_param(l,(1, d))
    vmem = (4 * tile * D_MODEL * 4 + 4 * tile * d * 4 + D_MODEL * RWKV_IN * 2
            + 11 * tile * d * 4 + 8 * tile * RWKV_IN * 4 + (12 << 20))
    return pl.pallas_call(
        functools.partial(_rwkv_kernel, tile=tile),
        grid=(b, s // tile),
        in_specs=[pl.BlockSpec((1, tile, D_MODEL), lambda i, j: (i, j, 0)),
                  _mod_spec(l),
                  _layer_param(l,(D_MODEL, RWKV_IN)),
                  _layer_param(l,(1, RWKV_IN)),
                  vec(), _layer_param(l,(LANES, d)), vec(), _layer_param(l,(LANES, d)),
                  _layer_param(l,(LANES, d)), vec(), vec(), vec(), vec(), vec()],
        out_specs=pl.BlockSpec((1, tile, d), lambda i, j: (i, j, 0)),
        out_shape=jax.ShapeDtypeStruct((b, s, d), F32),
        scratch_shapes=[pltpu.VMEM((RWKV_HEADS // 2, LANES, LANES), F32),
                        pltpu.VMEM((8, RWKV_IN), F32)]
                       + [pltpu.VMEM((tile, d), BF16) for _ in range(5)]
                       + [pltpu.VMEM((tile, d), F32) for _ in range(6)],
        compiler_params=_cparams(("parallel", "arbitrary"), vmem),
        name="rwkv7_mixer",
    )(x, mod, w, mu, w0, wup, a0, aup, gup, k_k, k_a, r_k, gn_g, gn_b)


def _qkv_kernel(x_ref, mod_ref, w_ref, *refs, tile):
    out_refs, p_s = refs[:-1], refs[-1]
    n_grp = 3 * ATT_DIM // LANES
    per = ATT_DIM // LANES
    n_half = 2
    half = tile // n_half
    ps = []
    for hf in range(n_half):
        x = x_ref[0, hf * half:(hf + 1) * half, :]
        u = (x * (1.0 + mod_ref[0, 1:2, :]) + mod_ref[0, 0:1, :]).astype(BF16)
        ps.append(_dg(u, w_ref[...]))
    for hf in range(n_half):
        for g in range(n_grp):
            p_s[g, hf * half:(hf + 1) * half, :] = ps[hf][:, g * LANES:(g + 1) * LANES]
        for pi, dil in enumerate(DILATIONS):
            q_ref, k_ref, v_ref = out_refs[3 * pi:3 * pi + 3]
            n_out = half // dil
            dst = slice(hf * n_out, (hf + 1) * n_out)
            for r in range(dil):
                rows = pl.ds(hf * half + r, n_out, stride=dil) if dil > 1 else \
                    slice(hf * half, (hf + 1) * half)
                grp = lambda t: jnp.concatenate([p_s[t * per + g, rows, :] for g in range(per)],
                                                axis=1)
                q_ref[0, r, dst, :] = (grp(0) * (LOG2_E * ATT_HEAD ** -0.5)).astype(BF16)
                k_ref[0, r, dst, :] = grp(1).astype(BF16)
                v_ref[0, r, dst, :] = grp(2).astype(BF16)


def _qkv_layer(l, x, mod, w, tile=512):
    b, s, _ = x.shape
    out_shape, out_specs = [], []
    for dil in DILATIONS:
        for _ in range(3):
            out_shape.append(jax.ShapeDtypeStruct((b, dil, s // dil, ATT_DIM), BF16))
            out_specs.append(pl.BlockSpec((1, dil, tile // dil, ATT_DIM), lambda i, j: (i, 0, j, 0)))
    vmem = (4 * tile * D_MODEL * 4 + D_MODEL * 3 * ATT_DIM * 2 + 4 * tile * 3 * ATT_DIM * 4
            + 2 * 9 * tile * ATT_DIM * 2 + (4 << 20))
    outs = pl.pallas_call(
        functools.partial(_qkv_kernel, tile=tile),
        grid=(b, s // tile),
        in_specs=[pl.BlockSpec((1, tile, D_MODEL), lambda i, j: (i, j, 0)),
                  _mod_spec(l),
                  _layer_param(l,(D_MODEL, 3 * ATT_DIM))],
        out_specs=out_specs,
        out_shape=out_shape,
        scratch_shapes=[pltpu.VMEM((3 * ATT_DIM // LANES, tile, LANES), F32)],
        compiler_params=_cparams(("parallel", "parallel"), vmem),
        name="att_qkv",
    )(x, mod, w)
    return [outs[3 * pi:3 * pi + 3] for pi in range(len(DILATIONS))]


ATT_SPAN = 2048


def _att_kernel(q_ref, kp_ref, kc_ref, vp_ref, vc_ref, o_ref, l_ref, kbuf, vbuf, *, dilation):
    n_sub = ATT_SPAN // dilation
    n_blk = n_sub // ATT_BLOCK
    span = pl.program_id(1)
    kbuf[:, :ATT_BLOCK] = kp_ref[0]
    kbuf[:, ATT_BLOCK:] = kc_ref[0]
    vbuf[:, :ATT_BLOCK] = vp_ref[0]
    vbuf[:, ATT_BLOCK:] = vc_ref[0]

    qi = lax.broadcasted_iota(jnp.int32, (ATT_BLOCK, 2 * ATT_BLOCK), 0)
    kj = lax.broadcasted_iota(jnp.int32, (ATT_BLOCK, 2 * ATT_BLOCK), 1)
    steps = qi + ATT_BLOCK - kj
    window = (steps >= 0) & (steps <= ATT_BLOCK)
    dist = (steps * dilation).astype(F32)
    lane_q = lax.broadcasted_iota(jnp.int32, (ATT_BLOCK, LANES), 1)
    zero = jnp.zeros((), BF16)
    heads = range(ATT_HEADS)
    biases = [jnp.where(window, dist * -(LOG2_E * 2.0 ** (-8.0 * (h + 1) / ATT_HEADS)), -jnp.inf)
              for h in heads]

    lane_k = lax.broadcasted_iota(jnp.int32, (2 * ATT_BLOCK, LANES), 1)
    one = jnp.ones((), BF16)
    pair_of = lambda t, h: t[:, (h // 2) * LANES:(h // 2 + 1) * LANES]
    own = lambda lane, h: (lane >= ATT_HEAD) if h % 2 else (lane < ATT_HEAD)

    def scores(uidx):
        r = uidx >> (n_blk.bit_length() - 1)
        n = uidx & (n_blk - 1)
        row0 = pl.multiple_of(n * ATT_BLOCK, ATT_BLOCK)
        q = q_ref[0, r, pl.ds(row0, ATT_BLOCK), :]
        kk = kbuf[r, pl.ds(row0, 2 * ATT_BLOCK), :]
        ss = [_dg(jnp.where(own(lane_q, h), pair_of(q, h), zero), pair_of(kk, h), NT)
              for h in heads]
        return r, n, row0, ss

    def finish(first_span, r, n, row0, ss):
        vv = vbuf[r, pl.ds(row0, 2 * ATT_BLOCK), :]
        if first_span:
            first_key = jnp.where(n == 0, ATT_BLOCK, 0)
            head_mask = jnp.where(kj >= first_key, 0.0, -jnp.inf)
            ss = [s + (biases[h] + head_mask) for h, s in zip(heads, ss)]
        else:
            ss = [s + biases[h] for h, s in zip(heads, ss)]
        ms = [jnp.max(s, axis=-1, keepdims=True) for s in ss]
        es = [jnp.exp2(s - m).astype(BF16) for s, m in zip(ss, ms)]
        pvs = [_dg(e, jnp.where(own(lane_k, h), pair_of(vv, h), one)) for h, e in zip(heads, es)]
        m_blk = jnp.zeros((ATT_BLOCK, LANES), F32)
        d_blk = jnp.ones((ATT_BLOCK, LANES), F32)
        tok0 = n * (ATT_BLOCK * dilation) + r
        if dilation == 1:
            rows = pl.ds(pl.multiple_of(tok0, ATT_BLOCK), ATT_BLOCK)
        else:
            rows = pl.ds(tok0, ATT_BLOCK, stride=dilation)
        for p in range(ATT_HEADS // 2):
            even, odd = pvs[2 * p], pvs[2 * p + 1]
            num = jnp.where(lane_q < ATT_HEAD, even, odd)
            den_sw = jnp.where(lane_q < ATT_HEAD, odd, even)
            den = pltpu.roll(den_sw, ATT_HEAD, 1)
            o_ref[0, p, rows, :] = num / den
            m_blk = jnp.where(lane_q == 2 * p, ms[2 * p], m_blk)
            m_blk = jnp.where(lane_q == 2 * p + 1, ms[2 * p + 1], m_blk)
            d_blk = jnp.where(lane_q == 2 * p, den, d_blk)
            d_blk = jnp.where(lane_q == 2 * p + 1, den_sw, d_blk)
        l_ref[0, rows, :] = (m_blk + jnp.log2(d_blk)) * LN_2

    per_trip = 4

    def make_body(first_span):
        def body(i, carry):
            us = [scores(per_trip * i + t) for t in range(2)]
            for t in range(per_trip):
                if t + 2 < per_trip:
                    us.append(scores(per_trip * i + t + 2))
                finish(first_span, *us[t])
            return carry
        return body

    n_units = dilation * n_blk

    @pl.when(span == 0)
    def _():
        lax.fori_loop(0, n_units // per_trip, make_body(True), 0)

    @pl.when(span != 0)
    def _():
        lax.fori_loop(0, n_units // per_trip, make_body(False), 0)


def _att_pattern(q, k, v, dilation):
    b, _, n_res, _ = q.shape
    s = dilation * n_res
    n_sub = ATT_SPAN // dilation
    n_blk = n_sub // ATT_BLOCK
    cur = pl.BlockSpec((1, dilation, n_sub, ATT_DIM), lambda i, j: (i, 0, j, 0))
    prev = pl.BlockSpec((1, dilation, ATT_BLOCK, ATT_DIM),
                        lambda i, j: (i, 0, jnp.maximum(j * n_blk - 1, 0), 0))
    buf = pltpu.VMEM((dilation, ATT_BLOCK + n_sub, ATT_DIM), BF16)
    vmem = (2 * (3 * ATT_SPAN + 2 * dilation * ATT_BLOCK) * ATT_DIM * 2
            + 2 * dilation * (ATT_BLOCK + n_sub) * ATT_DIM * 2
            + 2 * ATT_SPAN * (ATT_DIM + LANES) * 4 + (8 << 20))
    return pl.pallas_call(
        functools.partial(_att_kernel, dilation=dilation),
        grid=(b, s // ATT_SPAN),
        in_specs=[cur, prev, cur, prev, cur],
        out_specs=[pl.BlockSpec((1, ATT_HEADS // 2, ATT_SPAN, LANES), lambda i, j: (i, 0, j, 0)),
                   pl.BlockSpec((1, ATT_SPAN, LANES), lambda i, j: (i, j, 0))],
        out_shape=[jax.ShapeDtypeStruct((b, ATT_HEADS // 2, s, LANES), F32),
                   jax.ShapeDtypeStruct((b, s, LANES), F32)],
        scratch_shapes=[buf, buf],
        compiler_params=_cparams(("parallel", "parallel"), vmem),
        name=f"dilated_att_d{dilation}",
    )(q, k, k, v, v)


def _merge_kernel(x_ref, mod_ref, oa_ref, ob_ref, o1_ref, o4_ref, o16_ref, l1_ref, l4_ref, l16_ref,
                  wg_ref, wb_ref, wo_ref, g_ref, b_ref, out_ref):
    hi = lax.broadcasted_iota(jnp.int32, (LANES, ATT_DIM), 0)
    hj = lax.broadcasted_iota(jnp.int32, (LANES, ATT_DIM), 1)
    expand = jnp.where(hi == (hj >> 6), 1.0, 0.0).astype(BF16)
    n_half = 2
    rows = x_ref.shape[1] // n_half
    sls = [slice(i * rows, (i + 1) * rows) for i in range(n_half)]
    lane_cat = lambda ref, sl: jnp.concatenate([ref[0, p, sl, :] for p in range(ATT_HEADS // 2)],
                                               axis=1)
    xs = [x_ref[0, sl, :] for sl in sls]
    us = [(x * (1.0 + mod_ref[0, 1:2, :]) + mod_ref[0, 0:1, :]).astype(BF16) for x in xs]
    gates = [_sigmoid(_dg(u, wg_ref[...])) for u in us]
    o_cs = []
    for sl in sls:
        l1, l4, l16 = l1_ref[0, sl, :], l4_ref[0, sl, :], l16_ref[0, sl, :]
        m = jnp.maximum(jnp.maximum(l1, l4), l16)
        e1, e4, e16 = jnp.exp(l1 - m), jnp.exp(l4 - m), jnp.exp(l16 - m)
        inv = 1.0 / (e1 + e4 + e16)
        o_cs.append(_dot_exact_rhs(e1 * inv, expand) * lane_cat(o1_ref, sl)
                    + _dot_exact_rhs(e4 * inv, expand) * lane_cat(o4_ref, sl)
                    + _dot_exact_rhs(e16 * inv, expand) * lane_cat(o16_ref, sl))
    pa = [_dot1(oa_ref[0, sl, :], wb_ref[0]) for sl in sls]
    pb = [_dot1(ob_ref[0, sl, :], wb_ref[1]) for sl in sls]
    pc = [_dot1(o_c, wb_ref[2]) for o_c in o_cs]
    merged = [g_[:, :D_MODEL] * a_ + g_[:, D_MODEL:2 * D_MODEL] * b_ + g_[:, 2 * D_MODEL:] * c_
              for g_, a_, b_, c_ in zip(gates, pa, pb, pc)]
    hs = [_dot1(mg, wo_ref[...]) for mg in merged]
    for sl, x, h in zip(sls, xs, hs):
        z = DEEPNORM_ALPHA * x + (1.0 + mod_ref[0, 2:3, :]) * h
        out_ref[0, sl, :] = _layer_norm(z, g_ref[...], b_ref[...])


def _merge_layer(l, x, mod, oa, ob, oc, lses, wg, wb, wo, g, bta, tile=512):
    b, s, _ = x.shape
    xs = pl.BlockSpec((1, tile, D_MODEL), lambda i, j: (i, j, 0))
    bs = pl.BlockSpec((1, tile, ATT_DIM), lambda i, j: (i, j, 0))
    ls = pl.BlockSpec((1, tile, LANES), lambda i, j: (i, j, 0))
    cs = pl.BlockSpec((1, ATT_HEADS // 2, tile, LANES), lambda i, j: (i, 0, j, 0))
    vmem = (4 * tile * D_MODEL * 4 + 10 * tile * ATT_DIM * 4 + 6 * tile * LANES * 4
            + (3 * D_MODEL * D_MODEL + 3 * ATT_DIM * D_MODEL + D_MODEL * D_MODEL) * 2
            + 6 * tile * 3 * D_MODEL * 4 + (4 << 20))
    return pl.pallas_call(
        _merge_kernel,
        grid=(b, s // tile),
        in_specs=[xs, _mod_spec(l),
                  bs, bs, cs, cs, cs, ls, ls, ls,
                  _layer_param(l,(D_MODEL, N_BRANCH * D_MODEL)),
                  _layer_param(l,(N_BRANCH, ATT_DIM, D_MODEL)),
                  _layer_param(l,(D_MODEL, D_MODEL)),
                  _layer_param(l,(1, D_MODEL)), _layer_param(l,(1, D_MODEL))],
        out_specs=xs,
        out_shape=jax.ShapeDtypeStruct((b, s, D_MODEL), F32),
        compiler_params=_cparams(("parallel", "parallel"), vmem),
        name="merge_ln1",
    )(x, mod, oa, ob, oc[0], oc[1], oc[2], lses[0], lses[1], lses[2], wg, wb, wo, g, bta)


def _ffn_kernel(x_ref, mod_ref, w1_ref, w2_ref, g_ref, b_ref, out_ref):
    n_half = 2
    rows = x_ref.shape[1] // n_half
    sls = [slice(i * rows, (i + 1) * rows) for i in range(n_half)]
    xs = [x_ref[0, sl, :] for sl in sls]
    us = [(x * (1.0 + mod_ref[0, 4:5, :]) + mod_ref[0, 3:4, :]).astype(BF16) for x in xs]
    hs = [_dg(u, w1_ref[...]) for u in us]
    acts = [(_silu(h[:, :FFN_HIDDEN]) * h[:, FFN_HIDDEN:]).astype(BF16) for h in hs]
    ys = [_dg(act, w2_ref[...]) for act in acts]
    for sl, x, y in zip(sls, xs, ys):
        z = DEEPNORM_ALPHA * x + (1.0 + mod_ref[0, 5:6, :]) * y
        out_ref[0, sl, :] = _layer_norm(z, g_ref[...], b_ref[...])


def _ffn_layer(l, x, mod, w1, w2, g, bta, tile=512):
    b, s, _ = x.shape
    xs = pl.BlockSpec((1, tile, D_MODEL), lambda i, j: (i, j, 0))
    vmem = (4 * tile * D_MODEL * 4 + 3 * D_MODEL * FFN_HIDDEN * 2
            + 4 * tile * 2 * FFN_HIDDEN * 4 + (4 << 20))
    return pl.pallas_call(
        _ffn_kernel,
        grid=(b, s // tile),
        in_specs=[xs, _mod_spec(l),
                  _layer_param(l,(D_MODEL, 2 * FFN_HIDDEN)),
                  _layer_param(l,(FFN_HIDDEN, D_MODEL)),
                  _layer_param(l,(1, D_MODEL)), _layer_param(l,(1, D_MODEL))],
        out_specs=xs,
        out_shape=jax.ShapeDtypeStruct((b, s, D_MODEL), F32),
        compiler_params=_cparams(("parallel", "parallel"), vmem),
        name="ffn_ln2",
    )(x, mod, w1, w2, g, bta)


def _pad_rows(m, rows, offset=0):
    out = jnp.zeros((m.shape[0], rows) + m.shape[2:], m.dtype)
    return out.at[:, offset:offset + m.shape[1]].set(m)


def _mixer_params(w_in, gla_w_alpha, gla_b_alpha, gla_norm_g, rwkv_mu, rwkv_w0, rwkv_w_up,
                  rwkv_a0, rwkv_a_up, rwkv_g_up, rwkv_k_k, rwkv_k_a, rwkv_r_k, rwkv_gn_g,
                  rwkv_gn_b):
    n_l = w_in.shape[0]
    row = lambda t: t.reshape(n_l, 1, -1)
    gq_end = 2 * GLA_DK + GLA_DV
    w_gla = w_in[:, :, :gq_end].astype(BF16)
    w_og = w_in[:, :, gq_end + GLA_GATE_RANK:GLA_IN].astype(BF16)
    w_ga = jnp.zeros((n_l, D_MODEL, LANES), F32).at[:, :, :GLA_GATE_RANK].set(
        w_in[:, :, gq_end:gq_end + GLA_GATE_RANK]).astype(BF16)
    o0 = GLA_IN
    w_rwkv = w_in[:, :, o0:o0 + RWKV_IN].astype(BF16)
    o0 += RWKV_IN
    w_att = w_in[:, :, o0:o0 + 3 * ATT_DIM].astype(BF16)
    o0 += 3 * ATT_DIM
    w_gate = w_in[:, :, o0:].astype(BF16)
    gla = (w_gla, w_og, w_ga, _pad_rows(gla_w_alpha, LANES), row(gla_b_alpha), row(gla_norm_g))
    rwkv = (w_rwkv, row(rwkv_mu), row(rwkv_w0), _pad_rows(rwkv_w_up, LANES), row(rwkv_a0),
            _pad_rows(rwkv_a_up, LANES, RWKV_DECAY_RANK), rwkv_g_up, row(rwkv_k_k),
            row(rwkv_k_a), row(rwkv_r_k), row(rwkv_gn_g), row(rwkv_gn_b))
    return gla, rwkv, w_att, w_gate


def kernel(x, c, w_ada, b_ada, w_in, gla_w_alpha, gla_b_alpha, gla_norm_g, rwkv_mu, rwkv_w0,
           rwkv_w_up, rwkv_a0, rwkv_a_up, rwkv_g_up, rwkv_k_k, rwkv_k_a, rwkv_r_k, rwkv_gn_g,
           rwkv_gn_b, w_branch, w_out, ln1_g, ln1_b, ffn_w1, ffn_w2, ln2_g, ln2_b):
    n_l = w_in.shape[0]
    row = lambda t: t.reshape(n_l, 1, -1)
    mod = _modulation(c, w_ada, b_ada)
    gla, rwkv, w_att, w_gate = _mixer_params(
        w_in, gla_w_alpha, gla_b_alpha, gla_norm_g, rwkv_mu, rwkv_w0, rwkv_w_up, rwkv_a0,
        rwkv_a_up, rwkv_g_up, rwkv_k_k, rwkv_k_a, rwkv_r_k, rwkv_gn_g, rwkv_gn_b)
    w_branch, w_out = w_branch.astype(BF16), w_out.astype(BF16)
    ffn_w1, ffn_w2 = ffn_w1.astype(BF16), ffn_w2.astype(BF16)
    ln1_g, ln1_b, ln2_g, ln2_b = row(ln1_g), row(ln1_b), row(ln2_g), row(ln2_b)
    for l in range(n_l):
        o_a = _gla_layer(l, x, mod, *gla)
        o_b = _rwkv_layer(l, x, mod, *rwkv)
        qkvs = _qkv_layer(l, x, mod, w_att)
        res = [_att_pattern(*qkv, dil) for qkv, dil in zip(qkvs, DILATIONS)]
        x = _merge_layer(l, x, mod, o_a, o_b, [o for o, _ in res], [lse for _, lse in res],
                         w_gate, w_branch, w_out, ln1_g, ln1_b)
        x = _ffn_layer(l, x, mod, ffn_w1, ffn_w2, ln2_g, ln2_b)
    return x
```

```python
import functools
import math

import jax
import jax.numpy as jnp
from jax import lax
from jax.experimental import pallas as pl
from jax.experimental.pallas import tpu as pltpu

F32 = jnp.float32
BF16 = jnp.bfloat16

D_MODEL = 1024
DEPTH = 4
GLA_HEADS, GLA_HEAD_K, GLA_HEAD_V = 4, 64, 128
GLA_DK, GLA_DV = GLA_HEADS * GLA_HEAD_K, GLA_HEADS * GLA_HEAD_V
GLA_GATE_RANK = 16
GLA_GATE_TAU = 16.0
GLA_NORM_EPS = 1e-5
RWKV_HEADS, RWKV_HEAD = 8, 64
RWKV_DIM = RWKV_HEADS * RWKV_HEAD
RWKV_DECAY_RANK, RWKV_ICLR_RANK, RWKV_GATE_RANK = 64, 64, 128
RWKV_IN = 3 * RWKV_DIM + RWKV_DECAY_RANK + RWKV_ICLR_RANK + RWKV_GATE_RANK
RWKV_GN_EPS = 64e-5
ATT_HEADS, ATT_HEAD = 8, 64
ATT_DIM = ATT_HEADS * ATT_HEAD
ATT_BLOCK = 128
DILATIONS = (1, 4, 16)
N_BRANCH = 3
FFN_HIDDEN = 2816
LN_EPS = 1e-5
DEEPNORM_ALPHA = (2 * DEPTH) ** 0.25
GLA_IN = 2 * GLA_DK + GLA_DV + GLA_GATE_RANK + GLA_DV

LOG2_E = math.log2(math.e)
LN_2 = math.log(2.0)
LANES = 128
CHUNK = 64
VMEM_LIMIT_CAP = 60000 * 1024

NN = ((1,), (0,))
NT = ((1,), (1,))
TN = ((0,), (0,))


def _dg(a, b, dims=NN):
    return lax.dot_general(a, b, (dims, ((), ())), preferred_element_type=F32)


def _dot1(a, b, dims=NN):
    return _dg(a.astype(BF16), b.astype(BF16), dims)


def _split2(a):
    hi = a.astype(BF16)
    lo = (a - hi.astype(F32)).astype(BF16)
    return hi, lo


def _dot3(a, b, dims=NN):
    ah, al = _split2(a)
    bh, bl = _split2(b)
    return _dg(ah, bh, dims) + (_dg(ah, bl, dims) + _dg(al, bh, dims))


def _dot_exact_rhs(a, m_bf16, dims=NN, parts=2):
    acc = None
    rem = a
    for _ in range(parts):
        hi = rem.astype(BF16)
        term = _dg(hi, m_bf16, dims)
        acc = term if acc is None else acc + term
        rem = rem - hi.astype(F32)
    return acc


def _dot_exact_lhs(m_bf16, a, parts=2):
    acc = None
    rem = a
    for _ in range(parts):
        hi = rem.astype(BF16)
        term = _dg(m_bf16, hi)
        acc = term if acc is None else acc + term
        rem = rem - hi.astype(F32)
    return acc


def _sigmoid(x):
    return 1.0 / (1.0 + jnp.exp(-x))


def _silu(x):
    return x * _sigmoid(x)


def _log_sigmoid(x):
    return jnp.minimum(x, 0.0) - jnp.log(1.0 + jnp.exp(-jnp.abs(x)))


def _layer_norm(z, g, b):
    mu = jnp.mean(z, axis=-1, keepdims=True)
    zc = z - mu
    var = jnp.mean(zc * zc, axis=-1, keepdims=True)
    return zc * lax.rsqrt(var + LN_EPS) * g + b


def _stack_heads(x, lane):
    lo = jnp.where(lane < RWKV_HEAD, x, 0.0)
    hi = jnp.where(lane >= RWKV_HEAD, x, 0.0)
    return jnp.concatenate([lo, hi], axis=0)


def _cparams(sem, vmem_bytes):
    return pltpu.CompilerParams(dimension_semantics=sem,
                                vmem_limit_bytes=int(min(vmem_bytes, VMEM_LIMIT_CAP)))


def _layer_param(l, shape):
    zeros = (0,) * len(shape)
    return pl.BlockSpec((None,) + tuple(shape), lambda *_: (l,) + zeros,
                        pipeline_mode=pl.Buffered(1))


def _mod_spec(l):
    return pl.BlockSpec((None, 1, 6, D_MODEL), lambda i, j: (l, i, 0, 0))


def _mod_kernel(ct_ref, w_ref, b_ref, o_ref, *, batch):
    s = _silu(ct_ref[...])
    tn = w_ref.shape[2]
    o_ref[0] = jnp.zeros(o_ref.shape[1:], F32)
    for bi in range(batch):
        sb = jnp.broadcast_to(s[:, bi:bi + 1], (D_MODEL, LANES))
        for j in range(tn // LANES):
            sl = slice(j * LANES, (j + 1) * LANES)
            acc = jnp.sum(w_ref[0, :, sl] * sb, axis=0, keepdims=True)
            o_ref[0, bi:bi + 1, sl] = acc + b_ref[0, :, sl]


def _modulation(c, w_ada, b_ada):
    n_l = w_ada.shape[0]
    b = c.shape[0]
    rows = 8
    assert b <= rows
    c_t = jnp.zeros((D_MODEL, rows), F32).at[:, :b].set(c.T)
    tn = 1536
    out = pl.pallas_call(
        functools.partial(_mod_kernel, batch=b),
        grid=(n_l, 6 * D_MODEL // tn),
        in_specs=[pl.BlockSpec((D_MODEL, rows), lambda l, j: (0, 0)),
                  pl.BlockSpec((1, D_MODEL, tn), lambda l, j: (l, 0, j)),
                  pl.BlockSpec((1, 1, tn), lambda l, j: (l, 0, j))],
        out_specs=pl.BlockSpec((1, rows, tn), lambda l, j: (l, 0, j)),
        out_shape=jax.ShapeDtypeStruct((n_l, rows, 6 * D_MODEL), F32),
        compiler_params=_cparams(("parallel", "parallel"), 4 * D_MODEL * tn * 4),
        name="adaln_mod",
    )(c_t, w_ada, b_ada.reshape(n_l, 1, 6 * D_MODEL))
    return out[:, :b].reshape(n_l, b, 6, D_MODEL)


def _gla_kernel(x_ref, mod_ref, wm_ref, wog_ref, wga_ref, wal_ref, bal_ref, ng_ref, o_ref,
                st_ref, p_s, cum_s, o_s, *, tile):
    @pl.when(pl.program_id(1) == 0)
    def _():
        st_ref[...] = jnp.zeros_like(st_ref)

    n_qkv = 2 * GLA_DK + GLA_DV
    n_half = 2
    hrows = tile // n_half
    sls = [slice(i * hrows, (i + 1) * hrows) for i in range(n_half)]
    us = [(x_ref[0, sl, :] * (1.0 + mod_ref[0, 1:2, :]) + mod_ref[0, 0:1, :]).astype(BF16)
          for sl in sls]
    a_los = []
    for sl, u in zip(sls, us):
        p_s[sl, :n_qkv] = _dg(u, wm_ref[...])
        p_s[sl, n_qkv:] = _dg(u, wog_ref[...])
        a_los.append(_dg(u, wga_ref[...]))
    ri = lax.broadcasted_iota(jnp.int32, (hrows, hrows), 0)
    ci = lax.broadcasted_iota(jnp.int32, (hrows, hrows), 1)
    ltri = jnp.where(((ri >> 6) == (ci >> 6)) & (ci <= ri), 1.0, 0.0).astype(BF16)
    for sl, a_lo in zip(sls, a_los):
        z = _dot3(a_lo, wal_ref[...]) + bal_ref[...]
        log_a = _log_sigmoid(z) * (1.0 / GLA_GATE_TAU)
        cum_s[sl, :] = _dot_exact_lhs(ltri, log_a)

    lane = lax.broadcasted_iota(jnp.int32, (CHUNK, LANES), 1)
    i2 = lax.broadcasted_iota(jnp.int32, (2 * CHUNK, 2 * CHUNK), 0)
    j2 = lax.broadcasted_iota(jnp.int32, (2 * CHUNK, 2 * CHUNK), 1)
    causal = ((i2 >> 6) == (j2 >> 6)) & (j2 <= i2)
    scale = GLA_HEAD_K ** -0.5

    n_chunks = tile // CHUNK
    n_pairs = GLA_HEADS // 2
    units = [(c, pr) for c in range(n_chunks) for pr in range(n_pairs)]

    qsms, ksms, kdsms, vsts, decs = [], [], [], [], []
    for c, pr in units:
        rows = slice(c * CHUNK, (c + 1) * CHUNK)
        lo = pr * LANES
        cumc = cum_s[rows, lo:lo + LANES]
        qc = p_s[rows, lo:lo + LANES]
        kc = p_s[rows, GLA_DK + lo:GLA_DK + lo + LANES]
        vbase = 2 * GLA_DK + 2 * pr * GLA_HEAD_V
        vsts.append(jnp.concatenate([p_s[rows, vbase:vbase + GLA_HEAD_V],
                                     p_s[rows, vbase + GLA_HEAD_V:vbase + 2 * GLA_HEAD_V]],
                                    axis=0).astype(BF16))
        cl = cumc[CHUNK - 1:CHUNK, :]
        qsms.append(_stack_heads(qc * scale * jnp.exp(cumc), lane).astype(BF16))
        ksms.append(_stack_heads(kc * jnp.exp(-cumc), lane).astype(BF16))
        kdsms.append(_stack_heads(kc * jnp.exp(cl - cumc), lane).astype(BF16))
        decs.append(jnp.exp(cl))
    atts = [jnp.where(causal, _dg(q_, k_, NT), 0.0).astype(BF16) for q_, k_ in zip(qsms, ksms)]
    o_intra = [_dg(a_, v_) for a_, v_ in zip(atts, vsts)]
    d_states = [_dg(v_, kd_, TN) for v_, kd_ in zip(vsts, kdsms)]

    gts = [None] * len(units)
    for pr in range(n_pairs):
        g = st_ref[pr]
        for c in range(n_chunks):
            i = c * n_pairs + pr
            gts[i] = g.astype(BF16)
            g = g * decs[i] + d_states[i]
        st_ref[pr] = g

    for i, (c, pr) in enumerate(units):
        o_st = o_intra[i] + _dg(qsms[i], gts[i], NT)
        rows = slice(c * CHUNK, (c + 1) * CHUNK)
        ob = 2 * pr * GLA_HEAD_V
        o_s[rows, ob:ob + GLA_HEAD_V] = o_st[:CHUNK]
        o_s[rows, ob + GLA_HEAD_V:ob + 2 * GLA_HEAD_V] = o_st[CHUNK:]

    og_base = 2 * GLA_DK + GLA_DV
    for h in range(GLA_HEADS):
        sl = slice(h * GLA_HEAD_V, (h + 1) * GLA_HEAD_V)
        oh = o_s[:, sl]
        on = oh * lax.rsqrt(jnp.mean(oh * oh, axis=-1, keepdims=True) + GLA_NORM_EPS) * ng_ref[...]
        og = p_s[:, og_base + h * GLA_HEAD_V:og_base + (h + 1) * GLA_HEAD_V]
        o_ref[0, :, sl] = on * _silu(og)


def _gla_layer(l, x, mod, wm, wog, wga, wal, bal, ng, tile=512):
    b, s, _ = x.shape
    n_qkv = 2 * GLA_DK + GLA_DV
    n_main = n_qkv + GLA_DV
    vmem = (4 * tile * D_MODEL * 4 + 4 * tile * GLA_DV * 4 + D_MODEL * (n_main + LANES) * 2
            + tile * (n_main + GLA_DK + GLA_DV) * 4 + 6 * tile * n_main * 4 + (8 << 20))
    return pl.pallas_call(
        functools.partial(_gla_kernel, tile=tile),
        grid=(b, s // tile),
        in_specs=[pl.BlockSpec((1, tile, D_MODEL), lambda i, j: (i, j, 0)),
                  _mod_spec(l),
                  _layer_param(l,(D_MODEL, n_qkv)),
                  _layer_param(l,(D_MODEL, GLA_DV)),
                  _layer_param(l,(D_MODEL, LANES)),
                  _layer_param(l,(LANES, GLA_DK)),
                  _layer_param(l,(1, GLA_DK)),
                  _layer_param(l,(1, GLA_HEAD_V))],
        out_specs=pl.BlockSpec((1, tile, GLA_DV), lambda i, j: (i, j, 0)),
        out_shape=jax.ShapeDtypeStruct((b, s, GLA_DV), F32),
        scratch_shapes=[pltpu.VMEM((GLA_HEADS // 2, GLA_HEAD_V, LANES), F32),
                        pltpu.VMEM((tile, n_main), F32),
                        pltpu.VMEM((tile, GLA_DK), F32),
                        pltpu.VMEM((tile, GLA_DV), F32)],
        compiler_params=_cparams(("parallel", "arbitrary"), vmem),
        name="gla_mixer",
    )(x, mod, wm, wog, wga, wal, bal, ng)


def _inv_unit_lower(ns, eye, m16, m32, m64):
    ds = [jnp.where(m16, n, 0.0) for n in ns]
    xs = [eye + d for d in ds]
    pws = [d.astype(BF16) for d in ds]
    pws = [_dg(p, p).astype(BF16) for p in pws]
    for level in range(2):
        prods = [_dg(p, jnp.concatenate([x.astype(BF16), p], axis=1)) for x, p in zip(xs, pws)]
        xs = [x + pr[:, :LANES] for x, pr in zip(xs, prods)]
        pws = [pr[:, LANES:].astype(BF16) for pr in prods]
    xs = [x + _dg(p, x.astype(BF16)) for x, p in zip(xs, pws)]
    for m, blk in ((m32, 16), (m64, 32)):
        n_rows = xs[0].shape[0]
        lower = [slice(s0, s0 + blk) for s0 in range(blk, n_rows, 2 * blk)]
        upper = [slice(s0, s0 + blk) for s0 in range(0, n_rows, 2 * blk)]
        xbs = [x.astype(BF16) for x in xs]
        xls = [jnp.concatenate([xb[sl] for sl in lower], axis=0) for xb in xbs]
        xos = [_dg(xl, jnp.where(m, n, 0.0).astype(BF16)) for xl, n in zip(xls, ns)]
        upd = [_dg(xo.astype(BF16), xb) for xo, xb in zip(xos, xbs)]
        new = []
        for x, up in zip(xs, upd):
            parts = []
            for i, (su, sl) in enumerate(zip(upper, lower)):
                parts += [x[su], x[sl] + up[i * blk:(i + 1) * blk]]
            new.append(jnp.concatenate(parts, axis=0))
        xs = new
    return xs


def _rwkv_kernel(x_ref, mod_ref, w_ref, mu_ref, w0_ref, wup_ref, a0_ref, aup_ref, gup_ref,
                 kk_ref, ka_ref, rk_ref, gng_ref, gnb_ref, o_ref,
                 st_ref, carry_ref, a_s, b_s, k_s, r_s, v_s, bb_s, k2_s, cum_s, y_s, bon_s, g_s,
                 *, tile):
    @pl.when(pl.program_id(1) == 0)
    def _():
        st_ref[...] = jnp.zeros_like(st_ref)
        carry_ref[...] = jnp.zeros_like(carry_ref)

    d = RWKV_DIM
    bi = lax.broadcasted_iota(jnp.int32, (LANES, LANES), 0)
    bj = lax.broadcasted_iota(jnp.int32, (LANES, LANES), 1)
    same64 = (bi >> 6) == (bj >> 6)
    seg = jnp.where(same64, 1.0, 0.0).astype(BF16)
    ltri = jnp.where(same64 & (bj <= bi), 1.0, 0.0).astype(BF16)

    def seg_sum(t):
        return jnp.concatenate(
            [_dot_exact_rhs(t[:, q * LANES:(q + 1) * LANES], seg) for q in range(d // LANES)],
            axis=1)

    n_half = tile // LANES
    sls = [slice(i * LANES, (i + 1) * LANES) for i in range(n_half)]
    projs = [_dg((x_ref[0, sl, :] * (1.0 + mod_ref[0, 1:2, :]) + mod_ref[0, 0:1, :]).astype(BF16),
                 w_ref[...]) for sl in sls]
    row = lax.broadcasted_iota(jnp.int32, (LANES, 1), 0)
    last = carry_ref[0:1, :]
    for sl, p in zip(sls, projs):
        prev = jnp.where(row == 0, last, pltpu.roll(p, 1, 0))
        last = p[LANES - 1:LANES, :]
        ps = p + (prev - p) * mu_ref[...]
        r = ps[:, 0:d]
        k = ps[:, d:2 * d]
        v = ps[:, 2 * d:3 * d]
        wa_lo = ps[:, 3 * d:3 * d + LANES]
        g_lo = ps[:, 3 * d + LANES:3 * d + 2 * LANES]
        wl = w0_ref[...] + _dot3(jnp.tanh(wa_lo), wup_ref[...])
        lw = -_sigmoid(wl) * math.exp(-0.5)
        a = _sigmoid(a0_ref[...] + _dot1(wa_lo, aup_ref[...]))
        g_s[sl, :] = _dot1(_sigmoid(g_lo), gup_ref[...])
        kk = k * kk_ref[...]
        k2 = k * (1.0 + (a - 1.0) * ka_ref[...])
        kk = kk / jnp.maximum(jnp.sqrt(seg_sum(kk * kk)), 1e-12)
        bb = kk * a
        bon_s[sl, :] = seg_sum(r * k2 * rk_ref[...]) * v
        cum = _dot_exact_lhs(ltri, lw)
        e_neg = jnp.exp(-cum)
        a_s[sl, :] = (-kk * jnp.exp(cum - lw)).astype(BF16)
        b_s[sl, :] = (bb * e_neg).astype(BF16)
        k_s[sl, :] = (k2 * e_neg).astype(BF16)
        r_s[sl, :] = (r * jnp.exp(cum)).astype(BF16)
        v_s[sl, :] = v.astype(BF16)
        bb_s[sl, :] = bb
        k2_s[sl, :] = k2
        cum_s[sl, :] = cum
    carry_ref[0:1, :] = last

    lane = lax.broadcasted_iota(jnp.int32, (CHUNK, LANES), 1)
    strict = same64 & (bj < bi)
    incl = same64 & (bj <= bi)
    m16 = (bi >> 4) == (bj >> 4)
    m32 = ((bi >> 5) == (bj >> 5)) & jnp.logical_not(m16)
    m64 = same64 & ((bi >> 5) != (bj >> 5))
    eye = jnp.where(bi == bj, 1.0, 0.0)
    h2 = 2 * CHUNK

    n_pairs = RWKV_HEADS // 2
    units = [(c, pr) for c in range(tile // CHUNK) for pr in range(n_pairs)]

    def rows_of(c):
        return slice(c * CHUNK, (c + 1) * CHUNK)

    def lanes_of(pr):
        return slice(pr * LANES, (pr + 1) * LANES)

    stacked = {}
    for name, ref in (("a", a_s), ("b", b_s), ("k", k_s), ("r", r_s), ("v", v_s)):
        stacked[name] = [_stack_heads(ref[rows_of(c), lanes_of(pr)], lane) for c, pr in units]
    scs = [_dg(jnp.concatenate([a_, r_], axis=0), jnp.concatenate([b_, k_], axis=0), NT)
           for a_, r_, b_, k_ in zip(stacked["a"], stacked["r"], stacked["b"], stacked["k"])]
    abs_ = [jnp.where(strict, sc[:h2, :h2], 0.0) for sc in scs]
    akvs = [_dg(jnp.where(strict, sc[:h2, h2:], 0.0).astype(BF16), v_)
            for sc, v_ in zip(scs, stacked["v"])]
    rbks = [jnp.concatenate([jnp.where(incl, sc[h2:, :h2], 0.0),
                             jnp.where(incl, sc[h2:, h2:], 0.0)], axis=1).astype(BF16)
            for sc in scs]
    tinvs = _inv_unit_lower(abs_, eye, m16, m32, m64)
    wus = [_dg(t.astype(BF16), jnp.concatenate([a_, akv.astype(BF16)], axis=1))
           for t, a_, akv in zip(tinvs, stacked["a"], akvs)]
    wrs = [jnp.concatenate([wu[:, :LANES].astype(BF16), r_], axis=0)
           for wu, r_ in zip(wus, stacked["r"])]

    for c in range(tile // CHUNK):
        idx = [c * n_pairs + pr for pr in range(n_pairs)]
        gts = [st_ref[pr] for pr in range(n_pairs)]
        wrgs = [_dg(wrs[i], g_.astype(BF16), NT) for i, g_ in zip(idx, gts)]
        ums = [(wrg[:h2] + wus[i][:, LANES:]).astype(BF16) for i, wrg in zip(idx, wrgs)]
        uvs = [jnp.concatenate([um, stacked["v"][i]], axis=0) for i, um in zip(idx, ums)]
        ys = [wrg[h2:] + _dg(rbks[i], uv) for i, wrg, uv in zip(idx, wrgs, uvs)]
        for pr in range(n_pairs):
            cumc = cum_s[rows_of(c), lanes_of(pr)]
            cl = cumc[CHUNK - 1:CHUNK, :]
            dec = jnp.exp(cl - cumc)
            bkd = jnp.concatenate([_stack_heads(bb_s[rows_of(c), lanes_of(pr)] * dec, lane),
                                   _stack_heads(k2_s[rows_of(c), lanes_of(pr)] * dec, lane)],
                                  axis=0).astype(BF16)
            st_ref[pr] = gts[pr] * jnp.exp(cl) + _dg(uvs[pr], bkd, TN)
            y_s[rows_of(c), lanes_of(pr)] = ys[pr][:CHUNK] + ys[pr][CHUNK:]

    y = y_s[...]
    inv_n = 1.0 / RWKV_HEAD
    mu_h = seg_sum(y) * inv_n
    yc = y - mu_h
    var = seg_sum(yc * yc) * inv_n
    yn = yc * lax.rsqrt(var + RWKV_GN_EPS) * gng_ref[...] + gnb_ref[...]
    o_ref[0] = (yn + bon_s[...]) * g_s[...]


def _rwkv_layer(l, x, mod, w, mu, w0, wup, a0, aup, gup, k_k, k_a, r_k, gn_g, gn_b, tile=256):
    b, s, _ = x.shape
    d = RWKV_DIM
    vec = lambda: _layer_param(l,(1, d))
    vmem = (4 * tile * D_MODEL * 4 + 4 * tile * d * 4 + D_MODEL * RWKV_IN * 2
            + 11 * tile * d * 4 + 8 * tile * RWKV_IN * 4 + (12 << 20))
    return pl.pallas_call(
        functools.partial(_rwkv_kernel, tile=tile),
        grid=(b, s // tile),
        in_specs=[pl.BlockSpec((1, tile, D_MODEL), lambda i, j: (i, j, 0)),
                  _mod_spec(l),
                  _layer_param(l,(D_MODEL, RWKV_IN)),
                  _layer_param(l,(1, RWKV_IN)),
                  vec(), _layer_param(l,(LANES, d)), vec(), _layer_param(l,(LANES, d)),
                  _layer_param(l,(LANES, d)), vec(), vec(), vec(), vec(), vec()],
        out_specs=pl.BlockSpec((1, tile, d), lambda i, j: (i, j, 0)),
        out_shape=jax.ShapeDtypeStruct((b, s, d), F32),
        scratch_shapes=[pltpu.VMEM((RWKV_HEADS // 2, LANES, LANES), F32),
                        pltpu.VMEM((8, RWKV_IN), F32)]
                       + [pltpu.VMEM((tile, d), BF16) for _ in range(5)]
                       + [pltpu.VMEM((tile, d), F32) for _ in range(6)],
        compiler_params=_cparams(("parallel", "arbitrary"), vmem),
        name="rwkv7_mixer",
    )(x, mod, w, mu, w0, wup, a0, aup, gup, k_k, k_a, r_k, gn_g, gn_b)


def _qkv_kernel(x_ref, mod_ref, w_ref, *refs, tile):
    out_refs, p_s = refs[:-1], refs[-1]
    n_grp = 3 * ATT_DIM // LANES
    per = ATT_DIM // LANES
    n_half = 2
    half = tile // n_half
    ps = []
    for hf in range(n_half):
        x = x_ref[0, hf * half:(hf + 1) * half, :]
        u = (x * (1.0 + mod_ref[0, 1:2, :]) + mod_ref[0, 0:1, :]).astype(BF16)
        ps.append(_dg(u, w_ref[...]))
    for hf in range(n_half):
        for g in range(n_grp):
            p_s[g, hf * half:(hf + 1) * half, :] = ps[hf][:, g * LANES:(g + 1) * LANES]
        for pi, dil in enumerate(DILATIONS):
            q_ref, k_ref, v_ref = out_refs[3 * pi:3 * pi + 3]
            n_out = half // dil
            dst = slice(hf * n_out, (hf + 1) * n_out)
            for r in range(dil):
                rows = pl.ds(hf * half + r, n_out, stride=dil) if dil > 1 else \
                    slice(hf * half, (hf + 1) * half)
                grp = lambda t: jnp.concatenate([p_s[t * per + g, rows, :] for g in range(per)],
                                                axis=1)
                q_ref[0, r, dst, :] = (grp(0) * (LOG2_E * ATT_HEAD ** -0.5)).astype(BF16)
                k_ref[0, r, dst, :] = grp(1).astype(BF16)
                v_ref[0, r, dst, :] = grp(2).astype(BF16)


def _qkv_layer(l, x, mod, w, tile=512):
    b, s, _ = x.shape
    out_shape, out_specs = [], []
    for dil in DILATIONS:
        for _ in range(3):
            out_shape.append(jax.ShapeDtypeStruct((b, dil, s // dil, ATT_DIM), BF16))
            out_specs.append(pl.BlockSpec((1, dil, tile // dil, ATT_DIM), lambda i, j: (i, 0, j, 0)))
    vmem = (4 * tile * D_MODEL * 4 + D_MODEL * 3 * ATT_DIM * 2 + 4 * tile * 3 * ATT_DIM * 4
            + 2 * 9 * tile * ATT_DIM * 2 + (4 << 20))
    outs = pl.pallas_call(
        functools.partial(_qkv_kernel, tile=tile),
        grid=(b, s // tile),
        in_specs=[pl.BlockSpec((1, tile, D_MODEL), lambda i, j: (i, j, 0)),
                  _mod_spec(l),
                  _layer_param(l,(D_MODEL, 3 * ATT_DIM))],
        out_specs=out_specs,
        out_shape=out_shape,
        scratch_shapes=[pltpu.VMEM((3 * ATT_DIM // LANES, tile, LANES), F32)],
        compiler_params=_cparams(("parallel", "parallel"), vmem),
        name="att_qkv",
    )(x, mod, w)
    return [outs[3 * pi:3 * pi + 3] for pi in range(len(DILATIONS))]


ATT_SPAN = 2048


def _att_kernel(q_ref, kp_ref, kc_ref, vp_ref, vc_ref, o_ref, l_ref, kbuf, vbuf, *, dilation):
    n_sub = ATT_SPAN // dilation
    n_blk = n_sub // ATT_BLOCK
    span = pl.program_id(1)
    kbuf[:, :ATT_BLOCK] = kp_ref[0]
    kbuf[:, ATT_BLOCK:] = kc_ref[0]
    vbuf[:, :ATT_BLOCK] = vp_ref[0]
    vbuf[:, ATT_BLOCK:] = vc_ref[0]

    qi = lax.broadcasted_iota(jnp.int32, (ATT_BLOCK, 2 * ATT_BLOCK), 0)
    kj = lax.broadcasted_iota(jnp.int32, (ATT_BLOCK, 2 * ATT_BLOCK), 1)
    steps = qi + ATT_BLOCK - kj
    window = (steps >= 0) & (steps <= ATT_BLOCK)
    dist = (steps * dilation).astype(F32)
    lane_q = lax.broadcasted_iota(jnp.int32, (ATT_BLOCK, LANES), 1)
    zero = jnp.zeros((), BF16)
    heads = range(ATT_HEADS)
    biases = [jnp.where(window, dist * -(LOG2_E * 2.0 ** (-8.0 * (h + 1) / ATT_HEADS)), -jnp.inf)
              for h in heads]

    lane_k = lax.broadcasted_iota(jnp.int32, (2 * ATT_BLOCK, LANES), 1)
    one = jnp.ones((), BF16)
    pair_of = lambda t, h: t[:, (h // 2) * LANES:(h // 2 + 1) * LANES]
    own = lambda lane, h: (lane >= ATT_HEAD) if h % 2 else (lane < ATT_HEAD)

    def scores(uidx):
        r = uidx >> (n_blk.bit_length() - 1)
        n = uidx & (n_blk - 1)
        row0 = pl.multiple_of(n * ATT_BLOCK, ATT_BLOCK)
        q = q_ref[0, r, pl.ds(row0, ATT_BLOCK), :]
        kk = kbuf[r, pl.ds(row0, 2 * ATT_BLOCK), :]
        ss = [_dg(jnp.where(own(lane_q, h), pair_of(q, h), zero), pair_of(kk, h), NT)
              for h in heads]
        return r, n, row0, ss

    def finish(first_span, r, n, row0, ss):
        vv = vbuf[r, pl.ds(row0, 2 * ATT_BLOCK), :]
        if first_span:
            first_key = jnp.where(n == 0, ATT_BLOCK, 0)
            head_mask = jnp.where(kj >= first_key, 0.0, -jnp.inf)
            ss = [s + (biases[h] + head_mask) for h, s in zip(heads, ss)]
        else:
            ss = [s + biases[h] for h, s in zip(heads, ss)]
        ms = [jnp.max(s, axis=-1, keepdims=True) for s in ss]
        es = [jnp.exp2(s - m).astype(BF16) for s, m in zip(ss, ms)]
        pvs = [_dg(e, jnp.where(own(lane_k, h), pair_of(vv, h), one)) for h, e in zip(heads, es)]
        m_blk = jnp.zeros((ATT_BLOCK, LANES), F32)
        d_blk = jnp.ones((ATT_BLOCK, LANES), F32)
        tok0 = n * (ATT_BLOCK * dilation) + r
        if dilation == 1:
            rows = pl.ds(pl.multiple_of(tok0, ATT_BLOCK), ATT_BLOCK)
        else:
            rows = pl.ds(tok0, ATT_BLOCK, stride=dilation)
        for p in range(ATT_HEADS // 2):
            even, odd = pvs[2 * p], pvs[2 * p + 1]
            num = jnp.where(lane_q < ATT_HEAD, even, odd)
            den_sw = jnp.where(lane_q < ATT_HEAD, odd, even)
            den = pltpu.roll(den_sw, ATT_HEAD, 1)
            o_ref[0, p, rows, :] = num / den
            m_blk = jnp.where(lane_q == 2 * p, ms[2 * p], m_blk)
            m_blk = jnp.where(lane_q == 2 * p + 1, ms[2 * p + 1], m_blk)
            d_blk = jnp.where(lane_q == 2 * p, den, d_blk)
            d_blk = jnp.where(lane_q == 2 * p + 1, den_sw, d_blk)
        l_ref[0, rows, :] = (m_blk + jnp.log2(d_blk)) * LN_2

    per_trip = 4

    def make_body(first_span):
        def body(i, carry):
            us = [scores(per_trip * i + t) for t in range(2)]
            for t in range(per_trip):
                if t + 2 < per_trip:
                    us.append(scores(per_trip * i + t + 2))
                finish(first_span, *us[t])
            return carry
        return body

    n_units = dilation * n_blk

    @pl.when(span == 0)
    def _():
        lax.fori_loop(0, n_units // per_trip, make_body(True), 0)

    @pl.when(span != 0)
    def _():
        lax.fori_loop(0, n_units // per_trip, make_body(False), 0)


def _att_pattern(q, k, v, dilation):
    b, _, n_res, _ = q.shape
    s = dilation * n_res
    n_sub = ATT_SPAN // dilation
    n_blk = n_sub // ATT_BLOCK
    cur = pl.BlockSpec((1, dilation, n_sub, ATT_DIM), lambda i, j: (i, 0, j, 0))
    prev = pl.BlockSpec((1, dilation, ATT_BLOCK, ATT_DIM),
                        lambda i, j: (i, 0, jnp.maximum(j * n_blk - 1, 0), 0))
    buf = pltpu.VMEM((dilation, ATT_BLOCK + n_sub, ATT_DIM), BF16)
    vmem = (2 * (3 * ATT_SPAN + 2 * dilation * ATT_BLOCK) * ATT_DIM * 2
            + 2 * dilation * (ATT_BLOCK + n_sub) * ATT_DIM * 2
            + 2 * ATT_SPAN * (ATT_DIM + LANES) * 4 + (8 << 20))
    return pl.pallas_call(
        functools.partial(_att_kernel, dilation=dilation),
        grid=(b, s // ATT_SPAN),
        in_specs=[cur, prev, cur, prev, cur],
        out_specs=[pl.BlockSpec((1, ATT_HEADS // 2, ATT_SPAN, LANES), lambda i, j: (i, 0, j, 0)),
                   pl.BlockSpec((1, ATT_SPAN, LANES), lambda i, j: (i, j, 0))],
        out_shape=[jax.ShapeDtypeStruct((b, ATT_HEADS // 2, s, LANES), F32),
                   jax.ShapeDtypeStruct((b, s, LANES), F32)],
        scratch_shapes=[buf, buf],
        compiler_params=_cparams(("parallel", "parallel"), vmem),
        name=f"dilated_att_d{dilation}",
    )(q, k, k, v, v)


def _merge_kernel(x_ref, mod_ref, oa_ref, ob_ref, o1_ref, o4_ref, o16_ref, l1_ref, l4_ref, l16_ref,
                  wg_ref, wb_ref, wo_ref, g_ref, b_ref, out_ref):
    n_half = 2
    rows = x_ref.shape[1] // n_half
    lane = lax.broadcasted_iota(jnp.int32, (rows, LANES), 1)

    def expand_heads(w):
        cols = []
        for p in range(ATT_HEADS // 2):
            even = jnp.broadcast_to(w[:, 2 * p:2 * p + 1], w.shape)
            odd = jnp.broadcast_to(w[:, 2 * p + 1:2 * p + 2], w.shape)
            cols.append(jnp.where(lane < ATT_HEAD, even, odd))
        return jnp.concatenate(cols, axis=1)

    sls = [slice(i * rows, (i + 1) * rows) for i in range(n_half)]
    lane_cat = lambda ref, sl: jnp.concatenate([ref[0, p, sl, :] for p in range(ATT_HEADS // 2)],
                                               axis=1)
    xs = [x_ref[0, sl, :] for sl in sls]
    us = [(x * (1.0 + mod_ref[0, 1:2, :]) + mod_ref[0, 0:1, :]).astype(BF16) for x in xs]
    gates = [_sigmoid(_dg(u, wg_ref[...])) for u in us]
    o_cs = []
    for sl in sls:
        l1, l4, l16 = l1_ref[0, sl, :], l4_ref[0, sl, :], l16_ref[0, sl, :]
        m = jnp.maximum(jnp.maximum(l1, l4), l16)
        e1, e4, e16 = jnp.exp(l1 - m), jnp.exp(l4 - m), jnp.exp(l16 - m)
        inv = 1.0 / (e1 + e4 + e16)
        o_cs.append(expand_heads(e1 * inv) * lane_cat(o1_ref, sl)
                    + expand_heads(e4 * inv) * lane_cat(o4_ref, sl)
                    + expand_heads(e16 * inv) * lane_cat(o16_ref, sl))
    pa = [_dot1(oa_ref[0, sl, :], wb_ref[0]) for sl in sls]
    pb = [_dot1(ob_ref[0, sl, :], wb_ref[1]) for sl in sls]
    pc = [_dot1(o_c, wb_ref[2]) for o_c in o_cs]
    merged = [g_[:, :D_MODEL] * a_ + g_[:, D_MODEL:2 * D_MODEL] * b_ + g_[:, 2 * D_MODEL:] * c_
              for g_, a_, b_, c_ in zip(gates, pa, pb, pc)]
    hs = [_dot1(mg, wo_ref[...]) for mg in merged]
    for sl, x, h in zip(sls, xs, hs):
        z = DEEPNORM_ALPHA * x + (1.0 + mod_ref[0, 2:3, :]) * h
        out_ref[0, sl, :] = _layer_norm(z, g_ref[...], b_ref[...])


def _merge_layer(l, x, mod, oa, ob, oc, lses, wg, wb, wo, g, bta, tile=512):
    b, s, _ = x.shape
    xs = pl.BlockSpec((1, tile, D_MODEL), lambda i, j: (i, j, 0))
    bs = pl.BlockSpec((1, tile, ATT_DIM), lambda i, j: (i, j, 0))
    ls = pl.BlockSpec((1, tile, LANES), lambda i, j: (i, j, 0))
    cs = pl.BlockSpec((1, ATT_HEADS // 2, tile, LANES), lambda i, j: (i, 0, j, 0))
    vmem = (4 * tile * D_MODEL * 4 + 10 * tile * ATT_DIM * 4 + 6 * tile * LANES * 4
            + (3 * D_MODEL * D_MODEL + 3 * ATT_DIM * D_MODEL + D_MODEL * D_MODEL) * 2
            + 6 * tile * 3 * D_MODEL * 4 + (4 << 20))
    return pl.pallas_call(
        _merge_kernel,
        grid=(b, s // tile),
        in_specs=[xs, _mod_spec(l),
                  bs, bs, cs, cs, cs, ls, ls, ls,
                  _layer_param(l,(D_MODEL, N_BRANCH * D_MODEL)),
                  _layer_param(l,(N_BRANCH, ATT_DIM, D_MODEL)),
                  _layer_param(l,(D_MODEL, D_MODEL)),
                  _layer_param(l,(1, D_MODEL)), _layer_param(l,(1, D_MODEL))],
        out_specs=xs,
        out_shape=jax.ShapeDtypeStruct((b, s, D_MODEL), F32),
        compiler_params=_cparams(("parallel", "parallel"), vmem),
        name="merge_ln1",
    )(x, mod, oa, ob, oc[0], oc[1], oc[2], lses[0], lses[1], lses[2], wg, wb, wo, g, bta)


def _ffn_kernel(x_ref, mod_ref, w1_ref, w2_ref, g_ref, b_ref, out_ref):
    n_half = 2
    rows = x_ref.shape[1] // n_half
    sls = [slice(i * rows, (i + 1) * rows) for i in range(n_half)]
    xs = [x_ref[0, sl, :] for sl in sls]
    us = [(x * (1.0 + mod_ref[0, 4:5, :]) + mod_ref[0, 3:4, :]).astype(BF16) for x in xs]
    hs = [_dg(u, w1_ref[...]) for u in us]
    acts = [(_silu(h[:, :FFN_HIDDEN]) * h[:, FFN_HIDDEN:]).astype(BF16) for h in hs]
    ys = [_dg(act, w2_ref[...]) for act in acts]
    for sl, x, y in zip(sls, xs, ys):
        z = DEEPNORM_ALPHA * x + (1.0 + mod_ref[0, 5:6, :]) * y
        out_ref[0, sl, :] = _layer_norm(z, g_ref[...], b_ref[...])


def _ffn_layer(l, x, mod, w1, w2, g, bta, tile=512):
    b, s, _ = x.shape
    xs = pl.BlockSpec((1, tile, D_MODEL), lambda i, j: (i, j, 0))
    vmem = (4 * tile * D_MODEL * 4 + 3 * D_MODEL * FFN_HIDDEN * 2
            + 4 * tile * 2 * FFN_HIDDEN * 4 + (4 << 20))
    return pl.pallas_call(
        _ffn_kernel,
        grid=(b, s // tile),
        in_specs=[xs, _mod_spec(l),
                  _layer_param(l,(D_MODEL, 2 * FFN_HIDDEN)),
                  _layer_param(l,(FFN_HIDDEN, D_MODEL)),
                  _layer_param(l,(1, D_MODEL)), _layer_param(l,(1, D_MODEL))],
        out_specs=xs,
        out_shape=jax.ShapeDtypeStruct((b, s, D_MODEL), F32),
        compiler_params=_cparams(("parallel", "parallel"), vmem),
        name="ffn_ln2",
    )(x, mod, w1, w2, g, bta)


def _pad_rows(m, rows, offset=0):
    out = jnp.zeros((m.shape[0], rows) + m.shape[2:], m.dtype)
    return out.at[:, offset:offset + m.shape[1]].set(m)


_W_IN_GROUPS = (
    (0, 2 * GLA_DK + GLA_DV),
    (2 * GLA_DK + GLA_DV + GLA_GATE_RANK, GLA_DV),
    (2 * GLA_DK + GLA_DV, GLA_GATE_RANK),
    (GLA_IN, RWKV_IN),
    (GLA_IN + RWKV_IN, 3 * ATT_DIM),
    (GLA_IN + RWKV_IN + 3 * ATT_DIM, N_BRANCH * D_MODEL),
)


def _w_in_split_kernel(w_ref, *out_refs):
    for (c0, width), o_ref in zip(_W_IN_GROUPS, out_refs):
        if width >= LANES:
            o_ref[0] = w_ref[0, :, c0:c0 + width].astype(BF16)
        else:
            lane = lax.broadcasted_iota(jnp.int32, (w_ref.shape[1], LANES), 1)
            o_ref[0] = jnp.where(lane < width, w_ref[0, :, c0:c0 + LANES], 0.0).astype(BF16)


def _w_in_split(w_in, rows=256):
    n_l, d, n = w_in.shape
    widths = [max(width, LANES) for _, width in _W_IN_GROUPS]
    vmem = 2 * rows * n * 4 + 2 * rows * sum(widths) * 2 + 2 * rows * n * 4 + (4 << 20)
    return pl.pallas_call(
        _w_in_split_kernel,
        grid=(n_l, d // rows),
        in_specs=[pl.BlockSpec((1, rows, n), lambda l, i: (l, i, 0))],
        out_specs=[pl.BlockSpec((1, rows, wd), lambda l, i: (l, i, 0)) for wd in widths],
        out_shape=[jax.ShapeDtypeStruct((n_l, d, wd), BF16) for wd in widths],
        compiler_params=_cparams(("parallel", "parallel"), vmem),
        name="w_in_split",
    )(w_in)


def _mixer_params(w_in, gla_w_alpha, gla_b_alpha, gla_norm_g, rwkv_mu, rwkv_w0, rwkv_w_up,
                  rwkv_a0, rwkv_a_up, rwkv_g_up, rwkv_k_k, rwkv_k_a, rwkv_r_k, rwkv_gn_g,
                  rwkv_gn_b):
    n_l = w_in.shape[0]
    row = lambda t: t.reshape(n_l, 1, -1)
    w_gla, w_og, w_ga, w_rwkv, w_att, w_gate = _w_in_split(w_in)
    gla = (w_gla, w_og, w_ga, _pad_rows(gla_w_alpha, LANES), row(gla_b_alpha), row(gla_norm_g))
    rwkv = (w_rwkv, row(rwkv_mu), row(rwkv_w0), _pad_rows(rwkv_w_up, LANES), row(rwkv_a0),
            _pad_rows(rwkv_a_up, LANES, RWKV_DECAY_RANK), rwkv_g_up, row(rwkv_k_k),
            row(rwkv_k_a), row(rwkv_r_k), row(rwkv_gn_g), row(rwkv_gn_b))
    return gla, rwkv, w_att, w_gate


def kernel(x, c, w_ada, b_ada, w_in, gla_w_alpha, gla_b_alpha, gla_norm_g, rwkv_mu, rwkv_w0,
           rwkv_w_up, rwkv_a0, rwkv_a_up, rwkv_g_up, rwkv_k_k, rwkv_k_a, rwkv_r_k, rwkv_gn_g,
           rwkv_gn_b, w_branch, w_out, ln1_g, ln1_b, ffn_w1, ffn_w2, ln2_g, ln2_b):
    n_l = w_in.shape[0]
    row = lambda t: t.reshape(n_l, 1, -1)
    mod = _modulation(c, w_ada, b_ada)
    gla, rwkv, w_att, w_gate = _mixer_params(
        w_in, gla_w_alpha, gla_b_alpha, gla_norm_g, rwkv_mu, rwkv_w0, rwkv_w_up, rwkv_a0,
        rwkv_a_up, rwkv_g_up, rwkv_k_k, rwkv_k_a, rwkv_r_k, rwkv_gn_g, rwkv_gn_b)
    w_branch, w_out = w_branch.astype(BF16), w_out.astype(BF16)
    ffn_w1, ffn_w2 = ffn_w1.astype(BF16), ffn_w2.astype(BF16)
    ln1_g, ln1_b, ln2_g, ln2_b = row(ln1_g), row(ln1_b), row(ln2_g), row(ln2_b)
    for l in range(n_l):
        o_a = _gla_layer(l, x, mod, *gla)
        o_b = _rwkv_layer(l, x, mod, *rwkv)
        qkvs = _qkv_layer(l, x, mod, w_att)
        res = [_att_pattern(*qkv, dil) for qkv, dil in zip(qkvs, DILATIONS)]
        x = _merge_layer(l, x, mod, o_a, o_b, [o for o, _ in res], [lse for _, lse in res],
                         w_gate, w_branch, w_out, ln1_g, ln1_b)
        x = _ffn_layer(l, x, mod, ffn_w1, ffn_w2, ln2_g, ln2_b)
    return x
```

```python
import functools
import math

import jax
import jax.numpy as jnp
from jax import lax
from jax.experimental import pallas as pl
from jax.experimental.pallas import tpu as pltpu

F32 = jnp.float32
BF16 = jnp.bfloat16

D_MODEL = 1024
DEPTH = 4
GLA_HEADS, GLA_HEAD_K, GLA_HEAD_V = 4, 64, 128
GLA_DK, GLA_DV = GLA_HEADS * GLA_HEAD_K, GLA_HEADS * GLA_HEAD_V
GLA_GATE_RANK = 16
GLA_GATE_TAU = 16.0
GLA_NORM_EPS = 1e-5
RWKV_HEADS, RWKV_HEAD = 8, 64
RWKV_DIM = RWKV_HEADS * RWKV_HEAD
RWKV_DECAY_RANK, RWKV_ICLR_RANK, RWKV_GATE_RANK = 64, 64, 128
RWKV_IN = 3 * RWKV_DIM + RWKV_DECAY_RANK + RWKV_ICLR_RANK + RWKV_GATE_RANK
RWKV_GN_EPS = 64e-5
ATT_HEADS, ATT_HEAD = 8, 64
ATT_DIM = ATT_HEADS * ATT_HEAD
ATT_BLOCK = 128
DILATIONS = (1, 4, 16)
N_BRANCH = 3
FFN_HIDDEN = 2816
LN_EPS = 1e-5
DEEPNORM_ALPHA = (2 * DEPTH) ** 0.25
GLA_IN = 2 * GLA_DK + GLA_DV + GLA_GATE_RANK + GLA_DV

LOG2_E = math.log2(math.e)
LN_2 = math.log(2.0)
LANES = 128
CHUNK = 64
VMEM_LIMIT_CAP = 60000 * 1024

NN = ((1,), (0,))
NT = ((1,), (1,))
TN = ((0,), (0,))


def _dg(a, b, dims=NN):
    return lax.dot_general(a, b, (dims, ((), ())), preferred_element_type=F32)


def _dot1(a, b, dims=NN):
    return _dg(a.astype(BF16), b.astype(BF16), dims)


def _split2(a):
    hi = a.astype(BF16)
    lo = (a - hi.astype(F32)).astype(BF16)
    return hi, lo


def _dot3(a, b, dims=NN):
    ah, al = _split2(a)
    bh, bl = _split2(b)
    return _dg(ah, bh, dims) + (_dg(ah, bl, dims) + _dg(al, bh, dims))


def _dot_exact_rhs(a, m_bf16, dims=NN, parts=2):
    acc = None
    rem = a
    for _ in range(parts):
        hi = rem.astype(BF16)
        term = _dg(hi, m_bf16, dims)
        acc = term if acc is None else acc + term
        rem = rem - hi.astype(F32)
    return acc


def _dot_exact_lhs(m_bf16, a, parts=2):
    acc = None
    rem = a
    for _ in range(parts):
        hi = rem.astype(BF16)
        term = _dg(m_bf16, hi)
        acc = term if acc is None else acc + term
        rem = rem - hi.astype(F32)
    return acc


def _sigmoid(x):
    return 1.0 / (1.0 + jnp.exp(-x))


def _silu(x):
    return x * _sigmoid(x)


def _log_sigmoid(x):
    return jnp.minimum(x, 0.0) - jnp.log(1.0 + jnp.exp(-jnp.abs(x)))


def _layer_norm(z, g, b):
    mu = jnp.mean(z, axis=-1, keepdims=True)
    zc = z - mu
    var = jnp.mean(zc * zc, axis=-1, keepdims=True)
    return zc * lax.rsqrt(var + LN_EPS) * g + b


def _stack_heads(x, lane):
    lo = jnp.where(lane < RWKV_HEAD, x, 0.0)
    hi = jnp.where(lane >= RWKV_HEAD, x, 0.0)
    return jnp.concatenate([lo, hi], axis=0)


def _cparams(sem, vmem_bytes):
    return pltpu.CompilerParams(dimension_semantics=sem,
                                vmem_limit_bytes=int(min(vmem_bytes, VMEM_LIMIT_CAP)))


def _layer_param(l, shape):
    zeros = (0,) * len(shape)
    return pl.BlockSpec((None,) + tuple(shape), lambda *_: (l,) + zeros,
                        pipeline_mode=pl.Buffered(1))


def _mod_spec(l):
    return pl.BlockSpec((None, 1, 6, D_MODEL), lambda i, j: (l, i, 0, 0))


def _mod_kernel(ct_ref, w_ref, b_ref, o_ref, *, batch):
    s = _silu(ct_ref[...])
    tn = w_ref.shape[2]
    o_ref[0] = jnp.zeros(o_ref.shape[1:], F32)
    for bi in range(batch):
        sb = jnp.broadcast_to(s[:, bi:bi + 1], (D_MODEL, LANES))
        for j in range(tn // LANES):
            sl = slice(j * LANES, (j + 1) * LANES)
            acc = jnp.sum(w_ref[0, :, sl] * sb, axis=0, keepdims=True)
            o_ref[0, bi:bi + 1, sl] = acc + b_ref[0, :, sl]


def _modulation(c, w_ada, b_ada):
    n_l = w_ada.shape[0]
    b = c.shape[0]
    rows = 8
    assert b <= rows
    c_t = jnp.zeros((D_MODEL, rows), F32).at[:, :b].set(c.T)
    tn = 1536
    out = pl.pallas_call(
        functools.partial(_mod_kernel, batch=b),
        grid=(n_l, 6 * D_MODEL // tn),
        in_specs=[pl.BlockSpec((D_MODEL, rows), lambda l, j: (0, 0)),
                  pl.BlockSpec((1, D_MODEL, tn), lambda l, j: (l, 0, j)),
                  pl.BlockSpec((1, 1, tn), lambda l, j: (l, 0, j))],
        out_specs=pl.BlockSpec((1, rows, tn), lambda l, j: (l, 0, j)),
        out_shape=jax.ShapeDtypeStruct((n_l, rows, 6 * D_MODEL), F32),
        compiler_params=_cparams(("parallel", "parallel"), 4 * D_MODEL * tn * 4),
        name="adaln_mod",
    )(c_t, w_ada, b_ada.reshape(n_l, 1, 6 * D_MODEL))
    return out[:, :b].reshape(n_l, b, 6, D_MODEL)


def _gla_kernel(x_ref, mod_ref, wm_ref, wog_ref, wga_ref, wal_ref, bal_ref, ng_ref, o_ref,
                st_ref, p_s, cum_s, o_s, *, tile):
    @pl.when(pl.program_id(1) == 0)
    def _():
        st_ref[...] = jnp.zeros_like(st_ref)

    n_qkv = 2 * GLA_DK + GLA_DV
    n_half = 2
    hrows = tile // n_half
    sls = [slice(i * hrows, (i + 1) * hrows) for i in range(n_half)]
    us = [(x_ref[0, sl, :] * (1.0 + mod_ref[0, 1:2, :]) + mod_ref[0, 0:1, :]).astype(BF16)
          for sl in sls]
    a_los = []
    for sl, u in zip(sls, us):
        p_s[sl, :n_qkv] = _dg(u, wm_ref[...])
        p_s[sl, n_qkv:] = _dg(u, wog_ref[...])
        a_los.append(_dg(u, wga_ref[...]))
    ri = lax.broadcasted_iota(jnp.int32, (hrows, hrows), 0)
    ci = lax.broadcasted_iota(jnp.int32, (hrows, hrows), 1)
    ltri = jnp.where(((ri >> 6) == (ci >> 6)) & (ci <= ri), 1.0, 0.0).astype(BF16)
    for sl, a_lo in zip(sls, a_los):
        z = _dot3(a_lo, wal_ref[...]) + bal_ref[...]
        log_a = _log_sigmoid(z) * (1.0 / GLA_GATE_TAU)
        cum_s[sl, :] = _dot_exact_lhs(ltri, log_a)

    lane = lax.broadcasted_iota(jnp.int32, (CHUNK, LANES), 1)
    i2 = lax.broadcasted_iota(jnp.int32, (2 * CHUNK, 2 * CHUNK), 0)
    j2 = lax.broadcasted_iota(jnp.int32, (2 * CHUNK, 2 * CHUNK), 1)
    causal = ((i2 >> 6) == (j2 >> 6)) & (j2 <= i2)
    scale = GLA_HEAD_K ** -0.5

    n_chunks = tile // CHUNK
    n_pairs = GLA_HEADS // 2
    units = [(c, pr) for c in range(n_chunks) for pr in range(n_pairs)]

    qsms, ksms, kdsms, vsts, decs = [], [], [], [], []
    for c, pr in units:
        rows = slice(c * CHUNK, (c + 1) * CHUNK)
        lo = pr * LANES
        cumc = cum_s[rows, lo:lo + LANES]
        qc = p_s[rows, lo:lo + LANES]
        kc = p_s[rows, GLA_DK + lo:GLA_DK + lo + LANES]
        vbase = 2 * GLA_DK + 2 * pr * GLA_HEAD_V
        vsts.append(jnp.concatenate([p_s[rows, vbase:vbase + GLA_HEAD_V],
                                     p_s[rows, vbase + GLA_HEAD_V:vbase + 2 * GLA_HEAD_V]],
                                    axis=0).astype(BF16))
        cl = cumc[CHUNK - 1:CHUNK, :]
        qsms.append(_stack_heads(qc * scale * jnp.exp(cumc), lane).astype(BF16))
        ksms.append(_stack_heads(kc * jnp.exp(-cumc), lane).astype(BF16))
        kdsms.append(_stack_heads(kc * jnp.exp(cl - cumc), lane).astype(BF16))
        decs.append(jnp.exp(cl))
    atts = [jnp.where(causal, _dg(q_, k_, NT), 0.0).astype(BF16) for q_, k_ in zip(qsms, ksms)]
    o_intra = [_dg(a_, v_) for a_, v_ in zip(atts, vsts)]
    d_states = [_dg(v_, kd_, TN) for v_, kd_ in zip(vsts, kdsms)]

    gts = [None] * len(units)
    for pr in range(n_pairs):
        g = st_ref[pr]
        for c in range(n_chunks):
            i = c * n_pairs + pr
            gts[i] = g.astype(BF16)
            g = g * decs[i] + d_states[i]
        st_ref[pr] = g

    for i, (c, pr) in enumerate(units):
        o_st = o_intra[i] + _dg(qsms[i], gts[i], NT)
        rows = slice(c * CHUNK, (c + 1) * CHUNK)
        ob = 2 * pr * GLA_HEAD_V
        o_s[rows, ob:ob + GLA_HEAD_V] = o_st[:CHUNK]
        o_s[rows, ob + GLA_HEAD_V:ob + 2 * GLA_HEAD_V] = o_st[CHUNK:]

    og_base = 2 * GLA_DK + GLA_DV
    for h in range(GLA_HEADS):
        sl = slice(h * GLA_HEAD_V, (h + 1) * GLA_HEAD_V)
        oh = o_s[:, sl]
        on = oh * lax.rsqrt(jnp.mean(oh * oh, axis=-1, keepdims=True) + GLA_NORM_EPS) * ng_ref[...]
        og = p_s[:, og_base + h * GLA_HEAD_V:og_base + (h + 1) * GLA_HEAD_V]
        o_ref[0, :, sl] = on * _silu(og)


def _gla_layer(l, x, mod, wm, wog, wga, wal, bal, ng, tile=512):
    b, s, _ = x.shape
    n_qkv = 2 * GLA_DK + GLA_DV
    n_main = n_qkv + GLA_DV
    vmem = (4 * tile * D_MODEL * 4 + 4 * tile * GLA_DV * 4 + D_MODEL * (n_main + LANES) * 2
            + tile * (n_main + GLA_DK + GLA_DV) * 4 + 6 * tile * n_main * 4 + (8 << 20))
    return pl.pallas_call(
        functools.partial(_gla_kernel, tile=tile),
        grid=(b, s // tile),
        in_specs=[pl.BlockSpec((1, tile, D_MODEL), lambda i, j: (i, j, 0)),
                  _mod_spec(l),
                  _layer_param(l,(D_MODEL, n_qkv)),
                  _layer_param(l,(D_MODEL, GLA_DV)),
                  _layer_param(l,(D_MODEL, LANES)),
                  _layer_param(l,(LANES, GLA_DK)),
                  _layer_param(l,(1, GLA_DK)),
                  _layer_param(l,(1, GLA_HEAD_V))],
        out_specs=pl.BlockSpec((1, tile, GLA_DV), lambda i, j: (i, j, 0)),
        out_shape=jax.ShapeDtypeStruct((b, s, GLA_DV), F32),
        scratch_shapes=[pltpu.VMEM((GLA_HEADS // 2, GLA_HEAD_V, LANES), F32),
                        pltpu.VMEM((tile, n_main), F32),
                        pltpu.VMEM((tile, GLA_DK), F32),
                        pltpu.VMEM((tile, GLA_DV), F32)],
        compiler_params=_cparams(("parallel", "arbitrary"), vmem),
        name="gla_mixer",
    )(x, mod, wm, wog, wga, wal, bal, ng)


def _inv_unit_lower(ns, eye, m16, m32, m64):
    ds = [jnp.where(m16, n, 0.0) for n in ns]
    xs = [eye + d for d in ds]
    pws = [d.astype(BF16) for d in ds]
    pws = [_dg(p, p).astype(BF16) for p in pws]
    for level in range(2):
        prods = [_dg(p, jnp.concatenate([x.astype(BF16), p], axis=1)) for x, p in zip(xs, pws)]
        xs = [x + pr[:, :LANES] for x, pr in zip(xs, prods)]
        pws = [pr[:, LANES:].astype(BF16) for pr in prods]
    xs = [x + _dg(p, x.astype(BF16)) for x, p in zip(xs, pws)]
    for m, blk in ((m32, 16), (m64, 32)):
        n_rows = xs[0].shape[0]
        lower = [slice(s0, s0 + blk) for s0 in range(blk, n_rows, 2 * blk)]
        upper = [slice(s0, s0 + blk) for s0 in range(0, n_rows, 2 * blk)]
        xbs = [x.astype(BF16) for x in xs]
        xls = [jnp.concatenate([xb[sl] for sl in lower], axis=0) for xb in xbs]
        xos = [_dg(xl, jnp.where(m, n, 0.0).astype(BF16)) for xl, n in zip(xls, ns)]
        upd = [_dg(xo.astype(BF16), xb) for xo, xb in zip(xos, xbs)]
        new = []
        for x, up in zip(xs, upd):
            parts = []
            for i, (su, sl) in enumerate(zip(upper, lower)):
                parts += [x[su], x[sl] + up[i * blk:(i + 1) * blk]]
            new.append(jnp.concatenate(parts, axis=0))
        xs = new
    return xs


def _rwkv_kernel(x_ref, mod_ref, w_ref, mu_ref, w0_ref, wup_ref, a0_ref, aup_ref, gup_ref,
                 kk_ref, ka_ref, rk_ref, gng_ref, gnb_ref, o_ref,
                 st_ref, carry_ref, a_s, b_s, k_s, r_s, v_s, bb_s, k2_s, cum_s, y_s, bon_s, g_s,
                 *, tile):
    @pl.when(pl.program_id(1) == 0)
    def _():
        st_ref[...] = jnp.zeros_like(st_ref)
        carry_ref[...] = jnp.zeros_like(carry_ref)

    d = RWKV_DIM
    bi = lax.broadcasted_iota(jnp.int32, (LANES, LANES), 0)
    bj = lax.broadcasted_iota(jnp.int32, (LANES, LANES), 1)
    same64 = (bi >> 6) == (bj >> 6)
    seg = jnp.where(same64, 1.0, 0.0).astype(BF16)
    ltri = jnp.where(same64 & (bj <= bi), 1.0, 0.0).astype(BF16)

    def seg_sum(t):
        return jnp.concatenate(
            [_dot_exact_rhs(t[:, q * LANES:(q + 1) * LANES], seg) for q in range(d // LANES)],
            axis=1)

    n_half = tile // LANES
    sls = [slice(i * LANES, (i + 1) * LANES) for i in range(n_half)]
    projs = [_dg((x_ref[0, sl, :] * (1.0 + mod_ref[0, 1:2, :]) + mod_ref[0, 0:1, :]).astype(BF16),
                 w_ref[...]) for sl in sls]
    row = lax.broadcasted_iota(jnp.int32, (LANES, 1), 0)
    last = carry_ref[0:1, :]
    for sl, p in zip(sls, projs):
        prev = jnp.where(row == 0, last, pltpu.roll(p, 1, 0))
        last = p[LANES - 1:LANES, :]
        ps = p + (prev - p) * mu_ref[...]
        r = ps[:, 0:d]
        k = ps[:, d:2 * d]
        v = ps[:, 2 * d:3 * d]
        wa_lo = ps[:, 3 * d:3 * d + LANES]
        g_lo = ps[:, 3 * d + LANES:3 * d + 2 * LANES]
        wl = w0_ref[...] + _dot3(jnp.tanh(wa_lo), wup_ref[...])
        lw = -_sigmoid(wl) * math.exp(-0.5)
        a = _sigmoid(a0_ref[...] + _dot1(wa_lo, aup_ref[...]))
        g_s[sl, :] = _dot1(_sigmoid(g_lo), gup_ref[...])
        kk = k * kk_ref[...]
        k2 = k * (1.0 + (a - 1.0) * ka_ref[...])
        kk = kk / jnp.maximum(jnp.sqrt(seg_sum(kk * kk)), 1e-12)
        bb = kk * a
        bon_s[sl, :] = seg_sum(r * k2 * rk_ref[...]) * v
        cum = _dot_exact_lhs(ltri, lw)
        e_neg = jnp.exp(-cum)
        a_s[sl, :] = (-kk * jnp.exp(cum - lw)).astype(BF16)
        b_s[sl, :] = (bb * e_neg).astype(BF16)
        k_s[sl, :] = (k2 * e_neg).astype(BF16)
        r_s[sl, :] = (r * jnp.exp(cum)).astype(BF16)
        v_s[sl, :] = v.astype(BF16)
        bb_s[sl, :] = bb
        k2_s[sl, :] = k2
        cum_s[sl, :] = cum
    carry_ref[0:1, :] = last

    lane = lax.broadcasted_iota(jnp.int32, (CHUNK, LANES), 1)
    strict = same64 & (bj < bi)
    incl = same64 & (bj <= bi)
    m16 = (bi >> 4) == (bj >> 4)
    m32 = ((bi >> 5) == (bj >> 5)) & jnp.logical_not(m16)
    m64 = same64 & ((bi >> 5) != (bj >> 5))
    eye = jnp.where(bi == bj, 1.0, 0.0)
    h2 = 2 * CHUNK

    n_pairs = RWKV_HEADS // 2
    units = [(c, pr) for c in range(tile // CHUNK) for pr in range(n_pairs)]

    def rows_of(c):
        return slice(c * CHUNK, (c + 1) * CHUNK)

    def lanes_of(pr):
        return slice(pr * LANES, (pr + 1) * LANES)

    stacked = {}
    for name, ref in (("a", a_s), ("b", b_s), ("k", k_s), ("r", r_s), ("v", v_s)):
        stacked[name] = [_stack_heads(ref[rows_of(c), lanes_of(pr)], lane) for c, pr in units]
    scs = [_dg(jnp.concatenate([a_, r_], axis=0), jnp.concatenate([b_, k_], axis=0), NT)
           for a_, r_, b_, k_ in zip(stacked["a"], stacked["r"], stacked["b"], stacked["k"])]
    abs_ = [jnp.where(strict, sc[:h2, :h2], 0.0) for sc in scs]
    akvs = [_dg(jnp.where(strict, sc[:h2, h2:], 0.0).astype(BF16), v_)
            for sc, v_ in zip(scs, stacked["v"])]
    rbks = [jnp.concatenate([jnp.where(incl, sc[h2:, :h2], 0.0),
                             jnp.where(incl, sc[h2:, h2:], 0.0)], axis=1).astype(BF16)
            for sc in scs]
    tinvs = _inv_unit_lower(abs_, eye, m16, m32, m64)
    wus = [_dg(t.astype(BF16), jnp.concatenate([a_, akv.astype(BF16)], axis=1))
           for t, a_, akv in zip(tinvs, stacked["a"], akvs)]
    wrs = [jnp.concatenate([wu[:, :LANES].astype(BF16), r_], axis=0)
           for wu, r_ in zip(wus, stacked["r"])]

    for c in range(tile // CHUNK):
        idx = [c * n_pairs + pr for pr in range(n_pairs)]
        gts = [st_ref[pr] for pr in range(n_pairs)]
        wrgs = [_dg(wrs[i], g_.astype(BF16), NT) for i, g_ in zip(idx, gts)]
        ums = [(wrg[:h2] + wus[i][:, LANES:]).astype(BF16) for i, wrg in zip(idx, wrgs)]
        uvs = [jnp.concatenate([um, stacked["v"][i]], axis=0) for i, um in zip(idx, ums)]
        ys = [wrg[h2:] + _dg(rbks[i], uv) for i, wrg, uv in zip(idx, wrgs, uvs)]
        for pr in range(n_pairs):
            cumc = cum_s[rows_of(c), lanes_of(pr)]
            cl = cumc[CHUNK - 1:CHUNK, :]
            dec = jnp.exp(cl - cumc)
            bkd = jnp.concatenate([_stack_heads(bb_s[rows_of(c), lanes_of(pr)] * dec, lane),
                                   _stack_heads(k2_s[rows_of(c), lanes_of(pr)] * dec, lane)],
                                  axis=0).astype(BF16)
            st_ref[pr] = gts[pr] * jnp.exp(cl) + _dg(uvs[pr], bkd, TN)
            y_s[rows_of(c), lanes_of(pr)] = ys[pr][:CHUNK] + ys[pr][CHUNK:]

    y = y_s[...]
    inv_n = 1.0 / RWKV_HEAD
    mu_h = seg_sum(y) * inv_n
    yc = y - mu_h
    var = seg_sum(yc * yc) * inv_n
    yn = yc * lax.rsqrt(var + RWKV_GN_EPS) * gng_ref[...] + gnb_ref[...]
    o_ref[0] = (yn + bon_s[...]) * g_s[...]


def _rwkv_layer(l, x, mod, w, mu, w0, wup, a0, aup, gup, k_k, k_a, r_k, gn_g, gn_b, tile=256):
    b, s, _ = x.shape
    d = RWKV_DIM
    vec = lambda: _layer_param(l,(1, d))
    vmem = (4 * tile * D_MODEL * 4 + 4 * tile * d * 4 + D_MODEL * RWKV_IN * 2
            + 11 * tile * d * 4 + 8 * tile * RWKV_IN * 4 + (12 << 20))
    return pl.pallas_call(
        functools.partial(_rwkv_kernel, tile=tile),
        grid=(b, s // tile),
        in_specs=[pl.BlockSpec((1, tile, D_MODEL), lambda i, j: (i, j, 0)),
                  _mod_spec(l),
                  _layer_param(l,(D_MODEL, RWKV_IN)),
                  _layer_param(l,(1, RWKV_IN)),
                  vec(), _layer_param(l,(LANES, d)), vec(), _layer_param(l,(LANES, d)),
                  _layer_param(l,(LANES, d)), vec(), vec(), vec(), vec(), vec()],
        out_specs=pl.BlockSpec((1, tile, d), lambda i, j: (i, j, 0)),
        out_shape=jax.ShapeDtypeStruct((b, s, d), F32),
        scratch_shapes=[pltpu.VMEM((RWKV_HEADS // 2, LANES, LANES), F32),
                        pltpu.VMEM((8, RWKV_IN), F32)]
                       + [pltpu.VMEM((tile, d), BF16) for _ in range(5)]
                       + [pltpu.VMEM((tile, d), F32) for _ in range(6)],
        compiler_params=_cparams(("parallel", "arbitrary"), vmem),
        name="rwkv7_mixer",
    )(x, mod, w, mu, w0, wup, a0, aup, gup, k_k, k_a, r_k, gn_g, gn_b)


def _qkv_kernel(x_ref, mod_ref, w_ref, *refs, tile):
    out_refs, p_s = refs[:-1], refs[-1]
    n_grp = 3 * ATT_DIM // LANES
    per = ATT_DIM // LANES
    n_half = 2
    half = tile // n_half
    ps = []
    for hf in range(n_half):
        x = x_ref[0, hf * half:(hf + 1) * half, :]
        u = (x * (1.0 + mod_ref[0, 1:2, :]) + mod_ref[0, 0:1, :]).astype(BF16)
        ps.append(_dg(u, w_ref[...]))
    for hf in range(n_half):
        for g in range(n_grp):
            p_s[g, hf * half:(hf + 1) * half, :] = ps[hf][:, g * LANES:(g + 1) * LANES]
        for pi, dil in enumerate(DILATIONS):
            q_ref, k_ref, v_ref = out_refs[3 * pi:3 * pi + 3]
            n_out = half // dil
            dst = slice(hf * n_out, (hf + 1) * n_out)
            for r in range(dil):
                rows = pl.ds(hf * half + r, n_out, stride=dil) if dil > 1 else \
                    slice(hf * half, (hf + 1) * half)
                grp = lambda t: jnp.concatenate([p_s[t * per + g, rows, :] for g in range(per)],
                                                axis=1)
                q_ref[0, r, dst, :] = (grp(0) * (LOG2_E * ATT_HEAD ** -0.5)).astype(BF16)
                k_ref[0, r, dst, :] = grp(1).astype(BF16)
                v_ref[0, r, dst, :] = grp(2).astype(BF16)


def _qkv_layer(l, x, mod, w, tile=512):
    b, s, _ = x.shape
    out_shape, out_specs = [], []
    for dil in DILATIONS:
        for _ in range(3):
            out_shape.append(jax.ShapeDtypeStruct((b, dil, s // dil, ATT_DIM), BF16))
            out_specs.append(pl.BlockSpec((1, dil, tile // dil, ATT_DIM), lambda i, j: (i, 0, j, 0)))
    vmem = (4 * tile * D_MODEL * 4 + D_MODEL * 3 * ATT_DIM * 2 + 4 * tile * 3 * ATT_DIM * 4
            + 2 * 9 * tile * ATT_DIM * 2 + (4 << 20))
    outs = pl.pallas_call(
        functools.partial(_qkv_kernel, tile=tile),
        grid=(b, s // tile),
        in_specs=[pl.BlockSpec((1, tile, D_MODEL), lambda i, j: (i, j, 0)),
                  _mod_spec(l),
                  _layer_param(l,(D_MODEL, 3 * ATT_DIM))],
        out_specs=out_specs,
        out_shape=out_shape,
        scratch_shapes=[pltpu.VMEM((3 * ATT_DIM // LANES, tile, LANES), F32)],
        compiler_params=_cparams(("parallel", "parallel"), vmem),
        name="att_qkv",
    )(x, mod, w)
    return [outs[3 * pi:3 * pi + 3] for pi in range(len(DILATIONS))]


ATT_SPAN = 2048


def _att_kernel(q_ref, kp_ref, kc_ref, vp_ref, vc_ref, o_ref, l_ref, kbuf, vbuf, *, dilation):
    n_sub = ATT_SPAN // dilation
    n_blk = n_sub // ATT_BLOCK
    span = pl.program_id(1)
    kbuf[:, :ATT_BLOCK] = kp_ref[0]
    kbuf[:, ATT_BLOCK:] = kc_ref[0]
    vbuf[:, :ATT_BLOCK] = vp_ref[0]
    vbuf[:, ATT_BLOCK:] = vc_ref[0]

    qi = lax.broadcasted_iota(jnp.int32, (ATT_BLOCK, 2 * ATT_BLOCK), 0)
    kj = lax.broadcasted_iota(jnp.int32, (ATT_BLOCK, 2 * ATT_BLOCK), 1)
    steps = qi + ATT_BLOCK - kj
    window = (steps >= 0) & (steps <= ATT_BLOCK)
    dist = (steps * dilation).astype(F32)
    lane_q = lax.broadcasted_iota(jnp.int32, (ATT_BLOCK, LANES), 1)
    zero = jnp.zeros((), BF16)
    heads = range(ATT_HEADS)
    biases = [jnp.where(window, dist * -(LOG2_E * 2.0 ** (-8.0 * (h + 1) / ATT_HEADS)), -jnp.inf)
              for h in heads]

    lane_k = lax.broadcasted_iota(jnp.int32, (2 * ATT_BLOCK, LANES), 1)
    one = jnp.ones((), BF16)
    pair_of = lambda t, h: t[:, (h // 2) * LANES:(h // 2 + 1) * LANES]
    own = lambda lane, h: (lane >= ATT_HEAD) if h % 2 else (lane < ATT_HEAD)

    def scores(uidx):
        r = uidx >> (n_blk.bit_length() - 1)
        n = uidx & (n_blk - 1)
        row0 = pl.multiple_of(n * ATT_BLOCK, ATT_BLOCK)
        q = q_ref[0, r, pl.ds(row0, ATT_BLOCK), :]
        kk = kbuf[r, pl.ds(row0, 2 * ATT_BLOCK), :]
        ss = [_dg(jnp.where(own(lane_q, h), pair_of(q, h), zero), pair_of(kk, h), NT)
              for h in heads]
        return r, n, row0, ss

    def finish(first_span, r, n, row0, ss):
        vv = vbuf[r, pl.ds(row0, 2 * ATT_BLOCK), :]
        if first_span:
            first_key = jnp.where(n == 0, ATT_BLOCK, 0)
            head_mask = jnp.where(kj >= first_key, 0.0, -jnp.inf)
            ss = [s + (biases[h] + head_mask) for h, s in zip(heads, ss)]
        else:
            ss = [s + biases[h] for h, s in zip(heads, ss)]
        ms = [jnp.max(s, axis=-1, keepdims=True) for s in ss]
        es = [jnp.exp2(s - m).astype(BF16) for s, m in zip(ss, ms)]
        pvs = [_dg(e, jnp.where(own(lane_k, h), pair_of(vv, h), one)) for h, e in zip(heads, es)]
        m_blk = jnp.zeros((ATT_BLOCK, LANES), F32)
        d_blk = jnp.ones((ATT_BLOCK, LANES), F32)
        tok0 = n * (ATT_BLOCK * dilation) + r
        if dilation == 1:
            rows = pl.ds(pl.multiple_of(tok0, ATT_BLOCK), ATT_BLOCK)
        else:
            rows = pl.ds(tok0, ATT_BLOCK, stride=dilation)
        for p in range(ATT_HEADS // 2):
            even, odd = pvs[2 * p], pvs[2 * p + 1]
            num = jnp.where(lane_q < ATT_HEAD, even, odd)
            den_sw = jnp.where(lane_q < ATT_HEAD, odd, even)
            den = pltpu.roll(den_sw, ATT_HEAD, 1)
            o_ref[0, p, rows, :] = num / den
            m_blk = jnp.where(lane_q == 2 * p, ms[2 * p], m_blk)
            m_blk = jnp.where(lane_q == 2 * p + 1, ms[2 * p + 1], m_blk)
            d_blk = jnp.where(lane_q == 2 * p, den, d_blk)
            d_blk = jnp.where(lane_q == 2 * p + 1, den_sw, d_blk)
        l_ref[0, rows, :] = (m_blk + jnp.log2(d_blk)) * LN_2

    per_trip = 4

    def make_body(first_span):
        def body(i, carry):
            us = [scores(per_trip * i + t) for t in range(2)]
            for t in range(per_trip):
                if t + 2 < per_trip:
                    us.append(scores(per_trip * i + t + 2))
                finish(first_span, *us[t])
            return carry
        return body

    n_units = dilation * n_blk

    @pl.when(span == 0)
    def _():
        lax.fori_loop(0, n_units // per_trip, make_body(True), 0)

    @pl.when(span != 0)
    def _():
        lax.fori_loop(0, n_units // per_trip, make_body(False), 0)


def _att_pattern(q, k, v, dilation):
    b, _, n_res, _ = q.shape
    s = dilation * n_res
    n_sub = ATT_SPAN // dilation
    n_blk = n_sub // ATT_BLOCK
    cur = pl.BlockSpec((1, dilation, n_sub, ATT_DIM), lambda i, j: (i, 0, j, 0))
    prev = pl.BlockSpec((1, dilation, ATT_BLOCK, ATT_DIM),
                        lambda i, j: (i, 0, jnp.maximum(j * n_blk - 1, 0), 0))
    buf = pltpu.VMEM((dilation, ATT_BLOCK + n_sub, ATT_DIM), BF16)
    vmem = (2 * (3 * ATT_SPAN + 2 * dilation * ATT_BLOCK) * ATT_DIM * 2
            + 2 * dilation * (ATT_BLOCK + n_sub) * ATT_DIM * 2
            + 2 * ATT_SPAN * (ATT_DIM + LANES) * 4 + (8 << 20))
    return pl.pallas_call(
        functools.partial(_att_kernel, dilation=dilation),
        grid=(b, s // ATT_SPAN),
        in_specs=[cur, prev, cur, prev, cur],
        out_specs=[pl.BlockSpec((1, ATT_HEADS // 2, ATT_SPAN, LANES), lambda i, j: (i, 0, j, 0)),
                   pl.BlockSpec((1, ATT_SPAN, LANES), lambda i, j: (i, j, 0))],
        out_shape=[jax.ShapeDtypeStruct((b, ATT_HEADS // 2, s, LANES), F32),
                   jax.ShapeDtypeStruct((b, s, LANES), F32)],
        scratch_shapes=[buf, buf],
        compiler_params=_cparams(("parallel", "parallel"), vmem),
        name=f"dilated_att_d{dilation}",
    )(q, k, k, v, v)


def _merge_kernel(x_ref, mod_ref, oa_ref, ob_ref, o1_ref, o4_ref, o16_ref, l1_ref, l4_ref, l16_ref,
                  wg_ref, wb_ref, wo_ref, g_ref, b_ref, out_ref):
    n_half = 2
    rows = x_ref.shape[1] // n_half
    lane = lax.broadcasted_iota(jnp.int32, (rows, LANES), 1)

    def expand_heads(w):
        cols = []
        for p in range(ATT_HEADS // 2):
            even = jnp.broadcast_to(w[:, 2 * p:2 * p + 1], w.shape)
            odd = jnp.broadcast_to(w[:, 2 * p + 1:2 * p + 2], w.shape)
            cols.append(jnp.where(lane < ATT_HEAD, even, odd))
        return jnp.concatenate(cols, axis=1)

    sls = [slice(i * rows, (i + 1) * rows) for i in range(n_half)]
    lane_cat = lambda ref, sl: jnp.concatenate([ref[0, p, sl, :] for p in range(ATT_HEADS // 2)],
                                               axis=1)
    xs = [x_ref[0, sl, :] for sl in sls]
    us = [(x * (1.0 + mod_ref[0, 1:2, :]) + mod_ref[0, 0:1, :]).astype(BF16) for x in xs]
    gates = [_sigmoid(_dg(u, wg_ref[...])) for u in us]
    o_cs = []
    for sl in sls:
        l1, l4, l16 = l1_ref[0, sl, :], l4_ref[0, sl, :], l16_ref[0, sl, :]
        m = jnp.maximum(jnp.maximum(l1, l4), l16)
        e1, e4, e16 = jnp.exp(l1 - m), jnp.exp(l4 - m), jnp.exp(l16 - m)
        inv = 1.0 / (e1 + e4 + e16)
        o_cs.append(expand_heads(e1 * inv) * lane_cat(o1_ref, sl)
                    + expand_heads(e4 * inv) * lane_cat(o4_ref, sl)
                    + expand_heads(e16 * inv) * lane_cat(o16_ref, sl))
    pa = [_dot1(oa_ref[0, sl, :], wb_ref[0]) for sl in sls]
    pb = [_dot1(ob_ref[0, sl, :], wb_ref[1]) for sl in sls]
    pc = [_dot1(o_c, wb_ref[2]) for o_c in o_cs]
    merged = [g_[:, :D_MODEL] * a_ + g_[:, D_MODEL:2 * D_MODEL] * b_ + g_[:, 2 * D_MODEL:] * c_
              for g_, a_, b_, c_ in zip(gates, pa, pb, pc)]
    hs = [_dot1(mg, wo_ref[...]) for mg in merged]
    for sl, x, h in zip(sls, xs, hs):
        z = DEEPNORM_ALPHA * x + (1.0 + mod_ref[0, 2:3, :]) * h
        out_ref[0, sl, :] = _layer_norm(z, g_ref[...], b_ref[...])


def _merge_layer(l, x, mod, oa, ob, oc, lses, wg, wb, wo, g, bta, tile=512):
    b, s, _ = x.shape
    xs = pl.BlockSpec((1, tile, D_MODEL), lambda i, j: (i, j, 0))
    bs = pl.BlockSpec((1, tile, ATT_DIM), lambda i, j: (i, j, 0))
    ls = pl.BlockSpec((1, tile, LANES), lambda i, j: (i, j, 0))
    cs = pl.BlockSpec((1, ATT_HEADS // 2, tile, LANES), lambda i, j: (i, 0, j, 0))
    vmem = (4 * tile * D_MODEL * 4 + 10 * tile * ATT_DIM * 4 + 6 * tile * LANES * 4
            + (3 * D_MODEL * D_MODEL + 3 * ATT_DIM * D_MODEL + D_MODEL * D_MODEL) * 2
            + 6 * tile * 3 * D_MODEL * 4 + (4 << 20))
    return pl.pallas_call(
        _merge_kernel,
        grid=(b, s // tile),
        in_specs=[xs, _mod_spec(l),
                  bs, bs, cs, cs, cs, ls, ls, ls,
                  _layer_param(l,(D_MODEL, N_BRANCH * D_MODEL)),
                  _layer_param(l,(N_BRANCH, ATT_DIM, D_MODEL)),
                  _layer_param(l,(D_MODEL, D_MODEL)),
                  _layer_param(l,(1, D_MODEL)), _layer_param(l,(1, D_MODEL))],
        out_specs=xs,
        out_shape=jax.ShapeDtypeStruct((b, s, D_MODEL), F32),
        compiler_params=_cparams(("parallel", "parallel"), vmem),
        name="merge_ln1",
    )(x, mod, oa, ob, oc[0], oc[1], oc[2], lses[0], lses[1], lses[2], wg, wb, wo, g, bta)


def _ffn_kernel(x_ref, mod_ref, w1_ref, w2_ref, g_ref, b_ref, out_ref):
    n_half = 2
    rows = x_ref.shape[1] // n_half
    sls = [slice(i * rows, (i + 1) * rows) for i in range(n_half)]
    xs = [x_ref[0, sl, :] for sl in sls]
    us = [(x * (1.0 + mod_ref[0, 4:5, :]) + mod_ref[0, 3:4, :]).astype(BF16) for x in xs]
    hs = [_dg(u, w1_ref[...]) for u in us]
    acts = [(_silu(h[:, :FFN_HIDDEN]) * h[:, FFN_HIDDEN:]).astype(BF16) for h in hs]
    ys = [_dg(act, w2_ref[...]) for act in acts]
    for sl, x, y in zip(sls, xs, ys):
        z = DEEPNORM_ALPHA * x + (1.0 + mod_ref[0, 5:6, :]) * y
        out_ref[0, sl, :] = _layer_norm(z, g_ref[...], b_ref[...])


def _ffn_layer(l, x, mod, w1, w2, g, bta, tile=512):
    b, s, _ = x.shape
    xs = pl.BlockSpec((1, tile, D_MODEL), lambda i, j: (i, j, 0))
    vmem = (4 * tile * D_MODEL * 4 + 3 * D_MODEL * FFN_HIDDEN * 2
            + 4 * tile * 2 * FFN_HIDDEN * 4 + (4 << 20))
    return pl.pallas_call(
        _ffn_kernel,
        grid=(b, s // tile),
        in_specs=[xs, _mod_spec(l),
                  _layer_param(l,(D_MODEL, 2 * FFN_HIDDEN)),
                  _layer_param(l,(FFN_HIDDEN, D_MODEL)),
                  _layer_param(l,(1, D_MODEL)), _layer_param(l,(1, D_MODEL))],
        out_specs=xs,
        out_shape=jax.ShapeDtypeStruct((b, s, D_MODEL), F32),
        compiler_params=_cparams(("parallel", "parallel"), vmem),
        name="ffn_ln2",
    )(x, mod, w1, w2, g, bta)


def _pad_rows(m, rows, offset=0):
    out = jnp.zeros((m.shape[0], rows) + m.shape[2:], m.dtype)
    return out.at[:, offset:offset + m.shape[1]].set(m)


_W_IN_GROUPS = (
    (0, 2 * GLA_DK + GLA_DV),
    (2 * GLA_DK + GLA_DV + GLA_GATE_RANK, GLA_DV),
    (2 * GLA_DK + GLA_DV, GLA_GATE_RANK),
    (GLA_IN, RWKV_IN),
    (GLA_IN + RWKV_IN, 3 * ATT_DIM),
    (GLA_IN + RWKV_IN + 3 * ATT_DIM, N_BRANCH * D_MODEL),
)


def _w_in_split_kernel(wt_ref, *out_refs):
    for (c0, width), o_ref in zip(_W_IN_GROUPS, out_refs):
        if width >= LANES:
            o_ref[0] = wt_ref[0, c0:c0 + width, :].T.astype(BF16)
        else:
            lane = lax.broadcasted_iota(jnp.int32, (LANES, LANES), 1)
            o_ref[0] = jnp.where(lane < width, wt_ref[0, c0:c0 + LANES, :].T, 0.0).astype(BF16)


def _w_in_split(w_in):
    n_l, d, n = w_in.shape
    widths = [max(width, LANES) for _, width in _W_IN_GROUPS]
    vmem = 4 * LANES * n * 4 + 4 * LANES * sum(widths) * 2 + (4 << 20)
    return pl.pallas_call(
        _w_in_split_kernel,
        grid=(n_l, d // LANES),
        in_specs=[pl.BlockSpec((1, n, LANES), lambda l, i: (l, 0, i))],
        out_specs=[pl.BlockSpec((1, LANES, wd), lambda l, i: (l, i, 0)) for wd in widths],
        out_shape=[jax.ShapeDtypeStruct((n_l, d, wd), BF16) for wd in widths],
        compiler_params=_cparams(("parallel", "parallel"), vmem),
        name="w_in_split",
    )(jnp.swapaxes(w_in, 1, 2))


def _mixer_params(w_in, gla_w_alpha, gla_b_alpha, gla_norm_g, rwkv_mu, rwkv_w0, rwkv_w_up,
                  rwkv_a0, rwkv_a_up, rwkv_g_up, rwkv_k_k, rwkv_k_a, rwkv_r_k, rwkv_gn_g,
                  rwkv_gn_b):
    n_l = w_in.shape[0]
    row = lambda t: t.reshape(n_l, 1, -1)
    w_gla, w_og, w_ga, w_rwkv, w_att, w_gate = _w_in_split(w_in)
    gla = (w_gla, w_og, w_ga, _pad_rows(gla_w_alpha, LANES), row(gla_b_alpha), row(gla_norm_g))
    rwkv = (w_rwkv, row(rwkv_mu), row(rwkv_w0), _pad_rows(rwkv_w_up, LANES), row(rwkv_a0),
            _pad_rows(rwkv_a_up, LANES, RWKV_DECAY_RANK), rwkv_g_up, row(rwkv_k_k),
            row(rwkv_k_a), row(rwkv_r_k), row(rwkv_gn_g), row(rwkv_gn_b))
    return gla, rwkv, w_att, w_gate


def kernel(x, c, w_ada, b_ada, w_in, gla_w_alpha, gla_b_alpha, gla_norm_g, rwkv_mu, rwkv_w0,
           rwkv_w_up, rwkv_a0, rwkv_a_up, rwkv_g_up, rwkv_k_k, rwkv_k_a, rwkv_r_k, rwkv_gn_g,
           rwkv_gn_b, w_branch, w_out, ln1_g, ln1_b, ffn_w1, ffn_w2, ln2_g, ln2_b):
    n_l = w_in.shape[0]
    row = lambda t: t.reshape(n_l, 1, -1)
    mod = _modulation(c, w_ada, b_ada)
    gla, rwkv, w_att, w_gate = _mixer_params(
        w_in, gla_w_alpha, gla_b_alpha, gla_norm_g, rwkv_mu, rwkv_w0, rwkv_w_up, rwkv_a0,
        rwkv_a_up, rwkv_g_up, rwkv_k_k, rwkv_k_a, rwkv_r_k, rwkv_gn_g, rwkv_gn_b)
    w_branch, w_out = w_branch.astype(BF16), w_out.astype(BF16)
    ffn_w1, ffn_w2 = ffn_w1.astype(BF16), ffn_w2.astype(BF16)
    ln1_g, ln1_b, ln2_g, ln2_b = row(ln1_g), row(ln1_b), row(ln2_g), row(ln2_b)
    for l in range(n_l):
        o_a = _gla_layer(l, x, mod, *gla)
        o_b = _rwkv_layer(l, x, mod, *rwkv)
        qkvs = _qkv_layer(l, x, mod, w_att)
        res = [_att_pattern(*qkv, dil) for qkv, dil in zip(qkvs, DILATIONS)]
        x = _merge_layer(l, x, mod, o_a, o_b, [o for o, _ in res], [lse for _, lse in res],
                         w_gate, w_branch, w_out, ln1_g, ln1_b)
        x = _ffn_layer(l, x, mod, ffn_w1, ffn_w2, ln2_g, ln2_b)
    return x
```

```python
import functools
import math

import jax
import jax.numpy as jnp
from jax import lax
from jax.experimental import pallas as pl
from jax.experimental.pallas import tpu as pltpu

F32 = jnp.float32
BF16 = jnp.bfloat16

D_MODEL = 1024
DEPTH = 4
GLA_HEADS, GLA_HEAD_K, GLA_HEAD_V = 4, 64, 128
GLA_DK, GLA_DV = GLA_HEADS * GLA_HEAD_K, GLA_HEADS * GLA_HEAD_V
GLA_GATE_RANK = 16
GLA_GATE_TAU = 16.0
GLA_NORM_EPS = 1e-5
RWKV_HEADS, RWKV_HEAD = 8, 64
RWKV_DIM = RWKV_HEADS * RWKV_HEAD
RWKV_DECAY_RANK, RWKV_ICLR_RANK, RWKV_GATE_RANK = 64, 64, 128
RWKV_IN = 3 * RWKV_DIM + RWKV_DECAY_RANK + RWKV_ICLR_RANK + RWKV_GATE_RANK
RWKV_GN_EPS = 64e-5
ATT_HEADS, ATT_HEAD = 8, 64
ATT_DIM = ATT_HEADS * ATT_HEAD
ATT_BLOCK = 128
DILATIONS = (1, 4, 16)
N_BRANCH = 3
FFN_HIDDEN = 2816
LN_EPS = 1e-5
DEEPNORM_ALPHA = (2 * DEPTH) ** 0.25
GLA_IN = 2 * GLA_DK + GLA_DV + GLA_GATE_RANK + GLA_DV

LOG2_E = math.log2(math.e)
LN_2 = math.log(2.0)
LANES = 128
CHUNK = 64
VMEM_LIMIT_CAP = 60000 * 1024

NN = ((1,), (0,))
NT = ((1,), (1,))
TN = ((0,), (0,))


def _dg(a, b, dims=NN):
    return lax.dot_general(a, b, (dims, ((), ())), preferred_element_type=F32)


def _dot1(a, b, dims=NN):
    return _dg(a.astype(BF16), b.astype(BF16), dims)


def _split2(a):
    hi = a.astype(BF16)
    lo = (a - hi.astype(F32)).astype(BF16)
    return hi, lo


def _dot3(a, b, dims=NN):
    ah, al = _split2(a)
    bh, bl = _split2(b)
    return _dg(ah, bh, dims) + (_dg(ah, bl, dims) + _dg(al, bh, dims))


def _dot_exact_lhs(m_bf16, a, parts=2):
    acc = None
    rem = a
    for _ in range(parts):
        hi = rem.astype(BF16)
        term = _dg(m_bf16, hi)
        acc = term if acc is None else acc + term
        rem = rem - hi.astype(F32)
    return acc


def _sigmoid(x):
    return 1.0 / (1.0 + jnp.exp(-x))


def _silu(x):
    return x * _sigmoid(x)


def _log_sigmoid(x):
    return jnp.minimum(x, 0.0) - jnp.log(1.0 + jnp.exp(-jnp.abs(x)))


def _layer_norm(z, g, b):
    mu = jnp.mean(z, axis=-1, keepdims=True)
    zc = z - mu
    var = jnp.mean(zc * zc, axis=-1, keepdims=True)
    return zc * lax.rsqrt(var + LN_EPS) * g + b


def _stack_heads(x, lane):
    lo = jnp.where(lane < RWKV_HEAD, x, 0.0)
    hi = jnp.where(lane >= RWKV_HEAD, x, 0.0)
    return jnp.concatenate([lo, hi], axis=0)


def _cparams(sem, vmem_bytes):
    return pltpu.CompilerParams(dimension_semantics=sem,
                                vmem_limit_bytes=int(min(vmem_bytes, VMEM_LIMIT_CAP)))


def _layer_param(l, shape):
    zeros = (0,) * len(shape)
    return pl.BlockSpec((None,) + tuple(shape), lambda *_: (l,) + zeros,
                        pipeline_mode=pl.Buffered(1))


def _mod_spec(l):
    return pl.BlockSpec((None, 1, 6, D_MODEL), lambda i, j: (l, i, 0, 0))


def _mod_kernel(ct_ref, w_ref, b_ref, o_ref, *, batch):
    s = _silu(ct_ref[...])
    tn = w_ref.shape[2]
    o_ref[0] = jnp.zeros(o_ref.shape[1:], F32)
    for bi in range(batch):
        sb = jnp.broadcast_to(s[:, bi:bi + 1], (D_MODEL, LANES))
        for j in range(tn // LANES):
            sl = slice(j * LANES, (j + 1) * LANES)
            acc = jnp.sum(w_ref[0, :, sl] * sb, axis=0, keepdims=True)
            o_ref[0, bi:bi + 1, sl] = acc + b_ref[0, :, sl]


def _modulation(c, w_ada, b_ada):
    n_l = w_ada.shape[0]
    b = c.shape[0]
    rows = 8
    assert b <= rows
    c_t = jnp.zeros((D_MODEL, rows), F32).at[:, :b].set(c.T)
    tn = 1536
    out = pl.pallas_call(
        functools.partial(_mod_kernel, batch=b),
        grid=(n_l, 6 * D_MODEL // tn),
        in_specs=[pl.BlockSpec((D_MODEL, rows), lambda l, j: (0, 0)),
                  pl.BlockSpec((1, D_MODEL, tn), lambda l, j: (l, 0, j)),
                  pl.BlockSpec((1, 1, tn), lambda l, j: (l, 0, j))],
        out_specs=pl.BlockSpec((1, rows, tn), lambda l, j: (l, 0, j)),
        out_shape=jax.ShapeDtypeStruct((n_l, rows, 6 * D_MODEL), F32),
        compiler_params=_cparams(("parallel", "parallel"), 4 * D_MODEL * tn * 4),
        name="adaln_mod",
    )(c_t, w_ada, b_ada.reshape(n_l, 1, 6 * D_MODEL))
    return out[:, :b].reshape(n_l, b, 6, D_MODEL)


def _gla_kernel(x_ref, mod_ref, wm_ref, wog_ref, wga_ref, wal_ref, bal_ref, ng_ref, o_ref,
                st_ref, p_s, cum_s, o_s, *, tile):
    @pl.when(pl.program_id(1) == 0)
    def _():
        st_ref[...] = jnp.zeros_like(st_ref)

    n_qkv = 2 * GLA_DK + GLA_DV
    n_half = 2
    hrows = tile // n_half
    sls = [slice(i * hrows, (i + 1) * hrows) for i in range(n_half)]
    us = [(x_ref[0, sl, :] * (1.0 + mod_ref[0, 1:2, :]) + mod_ref[0, 0:1, :]).astype(BF16)
          for sl in sls]
    a_los = []
    for sl, u in zip(sls, us):
        p_s[sl, :n_qkv] = _dg(u, wm_ref[...])
        p_s[sl, n_qkv:] = _dg(u, wog_ref[...])
        a_los.append(_dg(u, wga_ref[...]))
    ri = lax.broadcasted_iota(jnp.int32, (hrows, hrows), 0)
    ci = lax.broadcasted_iota(jnp.int32, (hrows, hrows), 1)
    ltri = jnp.where(((ri >> 6) == (ci >> 6)) & (ci <= ri), 1.0, 0.0).astype(BF16)
    for sl, a_lo in zip(sls, a_los):
        z = _dot3(a_lo, wal_ref[...]) + bal_ref[...]
        log_a = _log_sigmoid(z) * (1.0 / GLA_GATE_TAU)
        cum_s[sl, :] = _dot_exact_lhs(ltri, log_a)

    lane = lax.broadcasted_iota(jnp.int32, (CHUNK, LANES), 1)
    i2 = lax.broadcasted_iota(jnp.int32, (2 * CHUNK, 2 * CHUNK), 0)
    j2 = lax.broadcasted_iota(jnp.int32, (2 * CHUNK, 2 * CHUNK), 1)
    causal = ((i2 >> 6) == (j2 >> 6)) & (j2 <= i2)
    scale = GLA_HEAD_K ** -0.5

    n_chunks = tile // CHUNK
    n_pairs = GLA_HEADS // 2
    units = [(c, pr) for c in range(n_chunks) for pr in range(n_pairs)]

    qsms, ksms, kdsms, vsts, decs = [], [], [], [], []
    for c, pr in units:
        rows = slice(c * CHUNK, (c + 1) * CHUNK)
        lo = pr * LANES
        cumc = cum_s[rows, lo:lo + LANES]
        qc = p_s[rows, lo:lo + LANES]
        kc = p_s[rows, GLA_DK + lo:GLA_DK + lo + LANES]
        vbase = 2 * GLA_DK + 2 * pr * GLA_HEAD_V
        vsts.append(jnp.concatenate([p_s[rows, vbase:vbase + GLA_HEAD_V],
                                     p_s[rows, vbase + GLA_HEAD_V:vbase + 2 * GLA_HEAD_V]],
                                    axis=0).astype(BF16))
        cl = cumc[CHUNK - 1:CHUNK, :]
        qsms.append(_stack_heads(qc * scale * jnp.exp(cumc), lane).astype(BF16))
        ksms.append(_stack_heads(kc * jnp.exp(-cumc), lane).astype(BF16))
        kdsms.append(_stack_heads(kc * jnp.exp(cl - cumc), lane).astype(BF16))
        decs.append(jnp.exp(cl))
    atts = [jnp.where(causal, _dg(q_, k_, NT), 0.0).astype(BF16) for q_, k_ in zip(qsms, ksms)]
    o_intra = [_dg(a_, v_) for a_, v_ in zip(atts, vsts)]
    d_states = [_dg(v_, kd_, TN) for v_, kd_ in zip(vsts, kdsms)]

    gts = [None] * len(units)
    for pr in range(n_pairs):
        g = st_ref[pr]
        for c in range(n_chunks):
            i = c * n_pairs + pr
            gts[i] = g.astype(BF16)
            g = g * decs[i] + d_states[i]
        st_ref[pr] = g

    for i, (c, pr) in enumerate(units):
        o_st = o_intra[i] + _dg(qsms[i], gts[i], NT)
        rows = slice(c * CHUNK, (c + 1) * CHUNK)
        ob = 2 * pr * GLA_HEAD_V
        o_s[rows, ob:ob + GLA_HEAD_V] = o_st[:CHUNK]
        o_s[rows, ob + GLA_HEAD_V:ob + 2 * GLA_HEAD_V] = o_st[CHUNK:]

    og_base = 2 * GLA_DK + GLA_DV
    for h in range(GLA_HEADS):
        sl = slice(h * GLA_HEAD_V, (h + 1) * GLA_HEAD_V)
        oh = o_s[:, sl]
        on = oh * lax.rsqrt(jnp.mean(oh * oh, axis=-1, keepdims=True) + GLA_NORM_EPS) * ng_ref[...]
        og = p_s[:, og_base + h * GLA_HEAD_V:og_base + (h + 1) * GLA_HEAD_V]
        o_ref[0, :, sl] = on * _silu(og)


def _gla_layer(l, x, mod, wm, wog, wga, wal, bal, ng, tile=512):
    b, s, _ = x.shape
    n_qkv = 2 * GLA_DK + GLA_DV
    n_main = n_qkv + GLA_DV
    vmem = (4 * tile * D_MODEL * 4 + 4 * tile * GLA_DV * 4 + D_MODEL * (n_main + LANES) * 2
            + tile * (n_main + GLA_DK + GLA_DV) * 4 + 6 * tile * n_main * 4 + (8 << 20))
    return pl.pallas_call(
        functools.partial(_gla_kernel, tile=tile),
        grid=(b, s // tile),
        in_specs=[pl.BlockSpec((1, tile, D_MODEL), lambda i, j: (i, j, 0)),
                  _mod_spec(l),
                  _layer_param(l,(D_MODEL, n_qkv)),
                  _layer_param(l,(D_MODEL, GLA_DV)),
                  _layer_param(l,(D_MODEL, LANES)),
                  _layer_param(l,(LANES, GLA_DK)),
                  _layer_param(l,(1, GLA_DK)),
                  _layer_param(l,(1, GLA_HEAD_V))],
        out_specs=pl.BlockSpec((1, tile, GLA_DV), lambda i, j: (i, j, 0)),
        out_shape=jax.ShapeDtypeStruct((b, s, GLA_DV), F32),
        scratch_shapes=[pltpu.VMEM((GLA_HEADS // 2, GLA_HEAD_V, LANES), F32),
                        pltpu.VMEM((tile, n_main), F32),
                        pltpu.VMEM((tile, GLA_DK), F32),
                        pltpu.VMEM((tile, GLA_DV), F32)],
        compiler_params=_cparams(("parallel", "arbitrary"), vmem),
        name="gla_mixer",
    )(x, mod, wm, wog, wga, wal, bal, ng)


def _inv_unit_lower(ns, eye, m16, m32, m64):
    ds = [jnp.where(m16, n, 0.0) for n in ns]
    xs = [eye + d for d in ds]
    pws = [d.astype(BF16) for d in ds]
    pws = [_dg(p, p).astype(BF16) for p in pws]
    for level in range(2):
        prods = [_dg(p, jnp.concatenate([x.astype(BF16), p], axis=1)) for x, p in zip(xs, pws)]
        xs = [x + pr[:, :LANES] for x, pr in zip(xs, prods)]
        pws = [pr[:, LANES:].astype(BF16) for pr in prods]
    xs = [x + _dg(p, x.astype(BF16)) for x, p in zip(xs, pws)]
    for m, blk in ((m32, 16), (m64, 32)):
        n_rows = xs[0].shape[0]
        lower = [slice(s0, s0 + blk) for s0 in range(blk, n_rows, 2 * blk)]
        upper = [slice(s0, s0 + blk) for s0 in range(0, n_rows, 2 * blk)]
        xbs = [x.astype(BF16) for x in xs]
        xls = [jnp.concatenate([xb[sl] for sl in lower], axis=0) for xb in xbs]
        xos = [_dg(xl, jnp.where(m, n, 0.0).astype(BF16)) for xl, n in zip(xls, ns)]
        upd = [_dg(xo.astype(BF16), xb) for xo, xb in zip(xos, xbs)]
        new = []
        for x, up in zip(xs, upd):
            parts = []
            for i, (su, sl) in enumerate(zip(upper, lower)):
                parts += [x[su], x[sl] + up[i * blk:(i + 1) * blk]]
            new.append(jnp.concatenate(parts, axis=0))
        xs = new
    return xs


def _rwkv_kernel(x_ref, mod_ref, w_ref, mu_ref, w0_ref, wup_ref, a0_ref, aup_ref, gup_ref,
                 kk_ref, ka_ref, rk_ref, gng_ref, gnb_ref, o_ref,
                 st_ref, carry_ref, a_s, b_s, k_s, r_s, v_s, bb_s, k2_s, cum_s, y_s, bon_s, g_s,
                 *, tile):
    @pl.when(pl.program_id(1) == 0)
    def _():
        st_ref[...] = jnp.zeros_like(st_ref)
        carry_ref[...] = jnp.zeros_like(carry_ref)

    d = RWKV_DIM
    bi = lax.broadcasted_iota(jnp.int32, (LANES, LANES), 0)
    bj = lax.broadcasted_iota(jnp.int32, (LANES, LANES), 1)
    same64 = (bi >> 6) == (bj >> 6)
    seg = jnp.where(same64, 1.0, 0.0).astype(BF16)
    ltri = jnp.where(same64 & (bj <= bi), 1.0, 0.0).astype(BF16)

    def seg_sum(t):
        return jnp.concatenate(
            [_dot1(t[:, q * LANES:(q + 1) * LANES], seg) for q in range(d // LANES)],
            axis=1)

    n_half = tile // LANES
    sls = [slice(i * LANES, (i + 1) * LANES) for i in range(n_half)]
    projs = [_dg((x_ref[0, sl, :] * (1.0 + mod_ref[0, 1:2, :]) + mod_ref[0, 0:1, :]).astype(BF16),
                 w_ref[...]) for sl in sls]
    row = lax.broadcasted_iota(jnp.int32, (LANES, 1), 0)
    last = carry_ref[0:1, :]
    for sl, p in zip(sls, projs):
        prev = jnp.where(row == 0, last, pltpu.roll(p, 1, 0))
        last = p[LANES - 1:LANES, :]
        ps = p + (prev - p) * mu_ref[...]
        r = ps[:, 0:d]
        k = ps[:, d:2 * d]
        v = ps[:, 2 * d:3 * d]
        wa_lo = ps[:, 3 * d:3 * d + LANES]
        g_lo = ps[:, 3 * d + LANES:3 * d + 2 * LANES]
        wl = w0_ref[...] + _dot1(jnp.tanh(wa_lo), wup_ref[...])
        lw = -_sigmoid(wl) * math.exp(-0.5)
        a = _sigmoid(a0_ref[...] + _dot1(wa_lo, aup_ref[...]))
        g_s[sl, :] = _dot1(_sigmoid(g_lo), gup_ref[...])
        kk = k * kk_ref[...]
        k2 = k * (1.0 + (a - 1.0) * ka_ref[...])
        kk = kk / jnp.maximum(jnp.sqrt(seg_sum(kk * kk)), 1e-12)
        bb = kk * a
        bon_s[sl, :] = seg_sum(r * k2 * rk_ref[...]) * v
        cum = _dot_exact_lhs(ltri, lw)
        e_neg = jnp.exp(-cum)
        a_s[sl, :] = (-kk * jnp.exp(cum - lw)).astype(BF16)
        b_s[sl, :] = (bb * e_neg).astype(BF16)
        k_s[sl, :] = (k2 * e_neg).astype(BF16)
        r_s[sl, :] = (r * jnp.exp(cum)).astype(BF16)
        v_s[sl, :] = v.astype(BF16)
        bb_s[sl, :] = bb
        k2_s[sl, :] = k2
        cum_s[sl, :] = cum
    carry_ref[0:1, :] = last

    lane = lax.broadcasted_iota(jnp.int32, (CHUNK, LANES), 1)
    strict = same64 & (bj < bi)
    incl = same64 & (bj <= bi)
    m16 = (bi >> 4) == (bj >> 4)
    m32 = ((bi >> 5) == (bj >> 5)) & jnp.logical_not(m16)
    m64 = same64 & ((bi >> 5) != (bj >> 5))
    eye = jnp.where(bi == bj, 1.0, 0.0)
    h2 = 2 * CHUNK

    n_pairs = RWKV_HEADS // 2
    units = [(c, pr) for c in range(tile // CHUNK) for pr in range(n_pairs)]

    def rows_of(c):
        return slice(c * CHUNK, (c + 1) * CHUNK)

    def lanes_of(pr):
        return slice(pr * LANES, (pr + 1) * LANES)

    stacked = {}
    for name, ref in (("a", a_s), ("b", b_s), ("k", k_s), ("r", r_s), ("v", v_s)):
        stacked[name] = [_stack_heads(ref[rows_of(c), lanes_of(pr)], lane) for c, pr in units]
    scs = [_dg(jnp.concatenate([a_, r_], axis=0), jnp.concatenate([b_, k_], axis=0), NT)
           for a_, r_, b_, k_ in zip(stacked["a"], stacked["r"], stacked["b"], stacked["k"])]
    abs_ = [jnp.where(strict, sc[:h2, :h2], 0.0) for sc in scs]
    akvs = [_dg(jnp.where(strict, sc[:h2, h2:], 0.0).astype(BF16), v_)
            for sc, v_ in zip(scs, stacked["v"])]
    rbks = [jnp.concatenate([jnp.where(incl, sc[h2:, :h2], 0.0),
                             jnp.where(incl, sc[h2:, h2:], 0.0)], axis=1).astype(BF16)
            for sc in scs]
    tinvs = _inv_unit_lower(abs_, eye, m16, m32, m64)
    wus = [_dg(t.astype(BF16), jnp.concatenate([a_, akv.astype(BF16)], axis=1))
           for t, a_, akv in zip(tinvs, stacked["a"], akvs)]
    wrs = [jnp.concatenate([wu[:, :LANES].astype(BF16), r_], axis=0)
           for wu, r_ in zip(wus, stacked["r"])]

    for c in range(tile // CHUNK):
        idx = [c * n_pairs + pr for pr in range(n_pairs)]
        gts = [st_ref[pr] for pr in range(n_pairs)]
        wrgs = [_dg(wrs[i], g_.astype(BF16), NT) for i, g_ in zip(idx, gts)]
        ums = [(wrg[:h2] + wus[i][:, LANES:]).astype(BF16) for i, wrg in zip(idx, wrgs)]
        uvs = [jnp.concatenate([um, stacked["v"][i]], axis=0) for i, um in zip(idx, ums)]
        ys = [wrg[h2:] + _dg(rbks[i], uv) for i, wrg, uv in zip(idx, wrgs, uvs)]
        for pr in range(n_pairs):
            cumc = cum_s[rows_of(c), lanes_of(pr)]
            cl = cumc[CHUNK - 1:CHUNK, :]
            dec = jnp.exp(cl - cumc)
            bkd = jnp.concatenate([_stack_heads(bb_s[rows_of(c), lanes_of(pr)] * dec, lane),
                                   _stack_heads(k2_s[rows_of(c), lanes_of(pr)] * dec, lane)],
                                  axis=0).astype(BF16)
            st_ref[pr] = gts[pr] * jnp.exp(cl) + _dg(uvs[pr], bkd, TN)
            y_s[rows_of(c), lanes_of(pr)] = ys[pr][:CHUNK] + ys[pr][CHUNK:]

    y = y_s[...]
    inv_n = 1.0 / RWKV_HEAD
    mu_h = seg_sum(y) * inv_n
    yc = y - mu_h
    var = seg_sum(yc * yc) * inv_n
    yn = yc * lax.rsqrt(var + RWKV_GN_EPS) * gng_ref[...] + gnb_ref[...]
    o_ref[0] = (yn + bon_s[...]) * g_s[...]


def _rwkv_layer(l, x, mod, w, mu, w0, wup, a0, aup, gup, k_k, k_a, r_k, gn_g, gn_b, tile=256):
    b, s, _ = x.shape
    d = RWKV_DIM
    vec = lambda: _layer_param(l,(1, d))
    vmem = (4 * tile * D_MODEL * 4 + 4 * tile * d * 4 + D_MODEL * RWKV_IN * 2
            + 11 * tile * d * 4 + 8 * tile * RWKV_IN * 4 + (12 << 20))
    return pl.pallas_call(
        functools.partial(_rwkv_kernel, tile=tile),
        grid=(b, s // tile),
        in_specs=[pl.BlockSpec((1, tile, D_MODEL), lambda i, j: (i, j, 0)),
                  _mod_spec(l),
                  _layer_param(l,(D_MODEL, RWKV_IN)),
                  _layer_param(l,(1, RWKV_IN)),
                  vec(), _layer_param(l,(LANES, d)), vec(), _layer_param(l,(LANES, d)),
                  _layer_param(l,(LANES, d)), vec(), vec(), vec(), vec(), vec()],
        out_specs=pl.BlockSpec((1, tile, d), lambda i, j: (i, j, 0)),
        out_shape=jax.ShapeDtypeStruct((b, s, d), F32),
        scratch_shapes=[pltpu.VMEM((RWKV_HEADS // 2, LANES, LANES), F32),
                        pltpu.VMEM((8, RWKV_IN), F32)]
                       + [pltpu.VMEM((tile, d), BF16) for _ in range(5)]
                       + [pltpu.VMEM((tile, d), F32) for _ in range(6)],
        compiler_params=_cparams(("parallel", "arbitrary"), vmem),
        name="rwkv7_mixer",
    )(x, mod, w, mu, w0, wup, a0, aup, gup, k_k, k_a, r_k, gn_g, gn_b)


def _qkv_kernel(x_ref, mod_ref, w_ref, *refs, tile):
    out_refs, p_s = refs[:-1], refs[-1]
    n_grp = 3 * ATT_DIM // LANES
    per = ATT_DIM // LANES
    n_half = 2
    half = tile // n_half
    ps = []
    for hf in range(n_half):
        x = x_ref[0, hf * half:(hf + 1) * half, :]
        u = (x * (1.0 + mod_ref[0, 1:2, :]) + mod_ref[0, 0:1, :]).astype(BF16)
        ps.append(_dg(u, w_ref[...]))
    for hf in range(n_half):
        for g in range(n_grp):
            p_s[g, hf * half:(hf + 1) * half, :] = ps[hf][:, g * LANES:(g + 1) * LANES]
        for pi, dil in enumerate(DILATIONS):
            q_ref, k_ref, v_ref = out_refs[3 * pi:3 * pi + 3]
            n_out = half // dil
            dst = slice(hf * n_out, (hf + 1) * n_out)
            for r in range(dil):
                rows = pl.ds(hf * half + r, n_out, stride=dil) if dil > 1 else \
                    slice(hf * half, (hf + 1) * half)
                grp = lambda t: jnp.concatenate([p_s[t * per + g, rows, :] for g in range(per)],
                                                axis=1)
                q_ref[0, r, dst, :] = (grp(0) * (LOG2_E * ATT_HEAD ** -0.5)).astype(BF16)
                k_ref[0, r, dst, :] = grp(1).astype(BF16)
                v_ref[0, r, dst, :] = grp(2).astype(BF16)


def _qkv_layer(l, x, mod, w, tile=512):
    b, s, _ = x.shape
    out_shape, out_specs = [], []
    for dil in DILATIONS:
        for _ in range(3):
            out_shape.append(jax.ShapeDtypeStruct((b, dil, s // dil, ATT_DIM), BF16))
            out_specs.append(pl.BlockSpec((1, dil, tile // dil, ATT_DIM), lambda i, j: (i, 0, j, 0)))
    vmem = (4 * tile * D_MODEL * 4 + D_MODEL * 3 * ATT_DIM * 2 + 4 * tile * 3 * ATT_DIM * 4
            + 2 * 9 * tile * ATT_DIM * 2 + (4 << 20))
    outs = pl.pallas_call(
        functools.partial(_qkv_kernel, tile=tile),
        grid=(b, s // tile),
        in_specs=[pl.BlockSpec((1, tile, D_MODEL), lambda i, j: (i, j, 0)),
                  _mod_spec(l),
                  _layer_param(l,(D_MODEL, 3 * ATT_DIM))],
        out_specs=out_specs,
        out_shape=out_shape,
        scratch_shapes=[pltpu.VMEM((3 * ATT_DIM // LANES, tile, LANES), F32)],
        compiler_params=_cparams(("parallel", "parallel"), vmem),
        name="att_qkv",
    )(x, mod, w)
    return [outs[3 * pi:3 * pi + 3] for pi in range(len(DILATIONS))]


ATT_SPAN = 2048


def _att_kernel(q_ref, kp_ref, kc_ref, vp_ref, vc_ref, o_ref, l_ref, kbuf, vbuf, *, dilation):
    n_sub = ATT_SPAN // dilation
    n_blk = n_sub // ATT_BLOCK
    span = pl.program_id(1)
    kbuf[:, :ATT_BLOCK] = kp_ref[0]
    kbuf[:, ATT_BLOCK:] = kc_ref[0]
    vbuf[:, :ATT_BLOCK] = vp_ref[0]
    vbuf[:, ATT_BLOCK:] = vc_ref[0]

    qi = lax.broadcasted_iota(jnp.int32, (ATT_BLOCK, 2 * ATT_BLOCK), 0)
    kj = lax.broadcasted_iota(jnp.int32, (ATT_BLOCK, 2 * ATT_BLOCK), 1)
    steps = qi + ATT_BLOCK - kj
    window = (steps >= 0) & (steps <= ATT_BLOCK)
    dist = (steps * dilation).astype(F32)
    lane_q = lax.broadcasted_iota(jnp.int32, (ATT_BLOCK, LANES), 1)
    zero = jnp.zeros((), BF16)
    heads = range(ATT_HEADS)
    biases = [jnp.where(window, dist * -(LOG2_E * 2.0 ** (-8.0 * (h + 1) / ATT_HEADS)), -jnp.inf)
              for h in heads]

    lane_k = lax.broadcasted_iota(jnp.int32, (2 * ATT_BLOCK, LANES), 1)
    one = jnp.ones((), BF16)
    pair_of = lambda t, h: t[:, (h // 2) * LANES:(h // 2 + 1) * LANES]
    own = lambda lane, h: (lane >= ATT_HEAD) if h % 2 else (lane < ATT_HEAD)

    def scores(uidx):
        r = uidx >> (n_blk.bit_length() - 1)
        n = uidx & (n_blk - 1)
        row0 = pl.multiple_of(n * ATT_BLOCK, ATT_BLOCK)
        q = q_ref[0, r, pl.ds(row0, ATT_BLOCK), :]
        kk = kbuf[r, pl.ds(row0, 2 * ATT_BLOCK), :]
        ss = [_dg(jnp.where(own(lane_q, h), pair_of(q, h), zero), pair_of(kk, h), NT)
              for h in heads]
        return r, n, row0, ss

    def finish(first_span, r, n, row0, ss):
        vv = vbuf[r, pl.ds(row0, 2 * ATT_BLOCK), :]
        if first_span:
            first_key = jnp.where(n == 0, ATT_BLOCK, 0)
            head_mask = jnp.where(kj >= first_key, 0.0, -jnp.inf)
            ss = [s + (biases[h] + head_mask) for h, s in zip(heads, ss)]
        else:
            ss = [s + biases[h] for h, s in zip(heads, ss)]
        ms = [jnp.max(s, axis=-1, keepdims=True) for s in ss]
        es = [jnp.exp2(s - m).astype(BF16) for s, m in zip(ss, ms)]
        pvs = [_dg(e, jnp.where(own(lane_k, h), pair_of(vv, h), one)) for h, e in zip(heads, es)]
        m_blk = jnp.zeros((ATT_BLOCK, LANES), F32)
        d_blk = jnp.ones((ATT_BLOCK, LANES), F32)
        tok0 = n * (ATT_BLOCK * dilation) + r
        if dilation == 1:
            rows = pl.ds(pl.multiple_of(tok0, ATT_BLOCK), ATT_BLOCK)
        else:
            rows = pl.ds(tok0, ATT_BLOCK, stride=dilation)
        for p in range(ATT_HEADS // 2):
            even, odd = pvs[2 * p], pvs[2 * p + 1]
            num = jnp.where(lane_q < ATT_HEAD, even, odd)
            den_sw = jnp.where(lane_q < ATT_HEAD, odd, even)
            den = pltpu.roll(den_sw, ATT_HEAD, 1)
            o_ref[0, p, rows, :] = num / den
            m_blk = jnp.where(lane_q == 2 * p, ms[2 * p], m_blk)
            m_blk = jnp.where(lane_q == 2 * p + 1, ms[2 * p + 1], m_blk)
            d_blk = jnp.where(lane_q == 2 * p, den, d_blk)
            d_blk = jnp.where(lane_q == 2 * p + 1, den_sw, d_blk)
        l_ref[0, rows, :] = (m_blk + jnp.log2(d_blk)) * LN_2

    per_trip = 4

    def make_body(first_span):
        def body(i, carry):
            us = [scores(per_trip * i + t) for t in range(2)]
            for t in range(per_trip):
                if t + 2 < per_trip:
                    us.append(scores(per_trip * i + t + 2))
                finish(first_span, *us[t])
            return carry
        return body

    n_units = dilation * n_blk

    @pl.when(span == 0)
    def _():
        lax.fori_loop(0, n_units // per_trip, make_body(True), 0)

    @pl.when(span != 0)
    def _():
        lax.fori_loop(0, n_units // per_trip, make_body(False), 0)


def _att_pattern(q, k, v, dilation):
    b, _, n_res, _ = q.shape
    s = dilation * n_res
    n_sub = ATT_SPAN // dilation
    n_blk = n_sub // ATT_BLOCK
    cur = pl.BlockSpec((1, dilation, n_sub, ATT_DIM), lambda i, j: (i, 0, j, 0))
    prev = pl.BlockSpec((1, dilation, ATT_BLOCK, ATT_DIM),
                        lambda i, j: (i, 0, jnp.maximum(j * n_blk - 1, 0), 0))
    buf = pltpu.VMEM((dilation, ATT_BLOCK + n_sub, ATT_DIM), BF16)
    vmem = (2 * (3 * ATT_SPAN + 2 * dilation * ATT_BLOCK) * ATT_DIM * 2
            + 2 * dilation * (ATT_BLOCK + n_sub) * ATT_DIM * 2
            + 2 * ATT_SPAN * (ATT_DIM + LANES) * 4 + (8 << 20))
    return pl.pallas_call(
        functools.partial(_att_kernel, dilation=dilation),
        grid=(b, s // ATT_SPAN),
        in_specs=[cur, prev, cur, prev, cur],
        out_specs=[pl.BlockSpec((1, ATT_HEADS // 2, ATT_SPAN, LANES), lambda i, j: (i, 0, j, 0)),
                   pl.BlockSpec((1, ATT_SPAN, LANES), lambda i, j: (i, j, 0))],
        out_shape=[jax.ShapeDtypeStruct((b, ATT_HEADS // 2, s, LANES), F32),
                   jax.ShapeDtypeStruct((b, s, LANES), F32)],
        scratch_shapes=[buf, buf],
        compiler_params=_cparams(("parallel", "parallel"), vmem),
        name=f"dilated_att_d{dilation}",
    )(q, k, k, v, v)


def _merge_kernel(x_ref, mod_ref, oa_ref, ob_ref, o1_ref, o4_ref, o16_ref, l1_ref, l4_ref, l16_ref,
                  wg_ref, wb_ref, wo_ref, g_ref, b_ref, out_ref):
    n_half = 2
    rows = x_ref.shape[1] // n_half
    lane = lax.broadcasted_iota(jnp.int32, (rows, LANES), 1)

    def expand_heads(w):
        cols = []
        for p in range(ATT_HEADS // 2):
            even = jnp.broadcast_to(w[:, 2 * p:2 * p + 1], w.shape)
            odd = jnp.broadcast_to(w[:, 2 * p + 1:2 * p + 2], w.shape)
            cols.append(jnp.where(lane < ATT_HEAD, even, odd))
        return jnp.concatenate(cols, axis=1)

    sls = [slice(i * rows, (i + 1) * rows) for i in range(n_half)]
    lane_cat = lambda ref, sl: jnp.concatenate([ref[0, p, sl, :] for p in range(ATT_HEADS // 2)],
                                               axis=1)
    xs = [x_ref[0, sl, :] for sl in sls]
    us = [(x * (1.0 + mod_ref[0, 1:2, :]) + mod_ref[0, 0:1, :]).astype(BF16) for x in xs]
    gates = [_sigmoid(_dg(u, wg_ref[...])) for u in us]
    o_cs = []
    for sl in sls:
        l1, l4, l16 = l1_ref[0, sl, :], l4_ref[0, sl, :], l16_ref[0, sl, :]
        m = jnp.maximum(jnp.maximum(l1, l4), l16)
        e1, e4, e16 = jnp.exp(l1 - m), jnp.exp(l4 - m), jnp.exp(l16 - m)
        inv = 1.0 / (e1 + e4 + e16)
        o_cs.append(expand_heads(e1 * inv) * lane_cat(o1_ref, sl)
                    + expand_heads(e4 * inv) * lane_cat(o4_ref, sl)
                    + expand_heads(e16 * inv) * lane_cat(o16_ref, sl))
    pa = [_dot1(oa_ref[0, sl, :], wb_ref[0]) for sl in sls]
    pb = [_dot1(ob_ref[0, sl, :], wb_ref[1]) for sl in sls]
    pc = [_dot1(o_c, wb_ref[2]) for o_c in o_cs]
    merged = [g_[:, :D_MODEL] * a_ + g_[:, D_MODEL:2 * D_MODEL] * b_ + g_[:, 2 * D_MODEL:] * c_
              for g_, a_, b_, c_ in zip(gates, pa, pb, pc)]
    hs = [_dot1(mg, wo_ref[...]) for mg in merged]
    for sl, x, h in zip(sls, xs, hs):
        z = DEEPNORM_ALPHA * x + (1.0 + mod_ref[0, 2:3, :]) * h
        out_ref[0, sl, :] = _layer_norm(z, g_ref[...], b_ref[...])


def _merge_layer(l, x, mod, oa, ob, oc, lses, wg, wb, wo, g, bta, tile=512):
    b, s, _ = x.shape
    xs = pl.BlockSpec((1, tile, D_MODEL), lambda i, j: (i, j, 0))
    bs = pl.BlockSpec((1, tile, ATT_DIM), lambda i, j: (i, j, 0))
    ls = pl.BlockSpec((1, tile, LANES), lambda i, j: (i, j, 0))
    cs = pl.BlockSpec((1, ATT_HEADS // 2, tile, LANES), lambda i, j: (i, 0, j, 0))
    vmem = (4 * tile * D_MODEL * 4 + 10 * tile * ATT_DIM * 4 + 6 * tile * LANES * 4
            + (3 * D_MODEL * D_MODEL + 3 * ATT_DIM * D_MODEL + D_MODEL * D_MODEL) * 2
            + 6 * tile * 3 * D_MODEL * 4 + (4 << 20))
    return pl.pallas_call(
        _merge_kernel,
        grid=(b, s // tile),
        in_specs=[xs, _mod_spec(l),
                  bs, bs, cs, cs, cs, ls, ls, ls,
                  _layer_param(l,(D_MODEL, N_BRANCH * D_MODEL)),
                  _layer_param(l,(N_BRANCH, ATT_DIM, D_MODEL)),
                  _layer_param(l,(D_MODEL, D_MODEL)),
                  _layer_param(l,(1, D_MODEL)), _layer_param(l,(1, D_MODEL))],
        out_specs=xs,
        out_shape=jax.ShapeDtypeStruct((b, s, D_MODEL), F32),
        compiler_params=_cparams(("parallel", "parallel"), vmem),
        name="merge_ln1",
    )(x, mod, oa, ob, oc[0], oc[1], oc[2], lses[0], lses[1], lses[2], wg, wb, wo, g, bta)


def _ffn_kernel(x_ref, mod_ref, w1_ref, w2_ref, g_ref, b_ref, out_ref):
    n_half = 2
    rows = x_ref.shape[1] // n_half
    sls = [slice(i * rows, (i + 1) * rows) for i in range(n_half)]
    xs = [x_ref[0, sl, :] for sl in sls]
    us = [(x * (1.0 + mod_ref[0, 4:5, :]) + mod_ref[0, 3:4, :]).astype(BF16) for x in xs]
    hs = [_dg(u, w1_ref[...]) for u in us]
    acts = [(_silu(h[:, :FFN_HIDDEN]) * h[:, FFN_HIDDEN:]).astype(BF16) for h in hs]
    ys = [_dg(act, w2_ref[...]) for act in acts]
    for sl, x, y in zip(sls, xs, ys):
        z = DEEPNORM_ALPHA * x + (1.0 + mod_ref[0, 5:6, :]) * y
        out_ref[0, sl, :] = _layer_norm(z, g_ref[...], b_ref[...])


def _ffn_layer(l, x, mod, w1, w2, g, bta, tile=512):
    b, s, _ = x.shape
    xs = pl.BlockSpec((1, tile, D_MODEL), lambda i, j: (i, j, 0))
    vmem = (4 * tile * D_MODEL * 4 + 3 * D_MODEL * FFN_HIDDEN * 2
            + 4 * tile * 2 * FFN_HIDDEN * 4 + (4 << 20))
    return pl.pallas_call(
        _ffn_kernel,
        grid=(b, s // tile),
        in_specs=[xs, _mod_spec(l),
                  _layer_param(l,(D_MODEL, 2 * FFN_HIDDEN)),
                  _layer_param(l,(FFN_HIDDEN, D_MODEL)),
                  _layer_param(l,(1, D_MODEL)), _layer_param(l,(1, D_MODEL))],
        out_specs=xs,
        out_shape=jax.ShapeDtypeStruct((b, s, D_MODEL), F32),
        compiler_params=_cparams(("parallel", "parallel"), vmem),
        name="ffn_ln2",
    )(x, mod, w1, w2, g, bta)


def _pad_rows(m, rows, offset=0):
    out = jnp.zeros((m.shape[0], rows) + m.shape[2:], m.dtype)
    return out.at[:, offset:offset + m.shape[1]].set(m)


_W_IN_GROUPS = (
    (0, 2 * GLA_DK + GLA_DV),
    (2 * GLA_DK + GLA_DV + GLA_GATE_RANK, GLA_DV),
    (2 * GLA_DK + GLA_DV, GLA_GATE_RANK),
    (GLA_IN, RWKV_IN),
    (GLA_IN + RWKV_IN, 3 * ATT_DIM),
    (GLA_IN + RWKV_IN + 3 * ATT_DIM, N_BRANCH * D_MODEL),
)


def _w_in_split_kernel(wt_ref, *out_refs):
    for (c0, width), o_ref in zip(_W_IN_GROUPS, out_refs):
        if width >= LANES:
            o_ref[0] = wt_ref[0, c0:c0 + width, :].T.astype(BF16)
        else:
            lane = lax.broadcasted_iota(jnp.int32, (LANES, LANES), 1)
            o_ref[0] = jnp.where(lane < width, wt_ref[0, c0:c0 + LANES, :].T, 0.0).astype(BF16)


def _w_in_split(w_in):
    n_l, d, n = w_in.shape
    widths = [max(width, LANES) for _, width in _W_IN_GROUPS]
    vmem = 4 * LANES * n * 4 + 4 * LANES * sum(widths) * 2 + (4 << 20)
    return pl.pallas_call(
        _w_in_split_kernel,
        grid=(n_l, d // LANES),
        in_specs=[pl.BlockSpec((1, n, LANES), lambda l, i: (l, 0, i))],
        out_specs=[pl.BlockSpec((1, LANES, wd), lambda l, i: (l, i, 0)) for wd in widths],
        out_shape=[jax.ShapeDtypeStruct((n_l, d, wd), BF16) for wd in widths],
        compiler_params=_cparams(("parallel", "parallel"), vmem),
        name="w_in_split",
    )(jnp.swapaxes(w_in, 1, 2))


def _mixer_params(w_in, gla_w_alpha, gla_b_alpha, gla_norm_g, rwkv_mu, rwkv_w0, rwkv_w_up,
                  rwkv_a0, rwkv_a_up, rwkv_g_up, rwkv_k_k, rwkv_k_a, rwkv_r_k, rwkv_gn_g,
                  rwkv_gn_b):
    n_l = w_in.shape[0]
    row = lambda t: t.reshape(n_l, 1, -1)
    w_gla, w_og, w_ga, w_rwkv, w_att, w_gate = _w_in_split(w_in)
    gla = (w_gla, w_og, w_ga, _pad_rows(gla_w_alpha, LANES), row(gla_b_alpha), row(gla_norm_g))
    rwkv = (w_rwkv, row(rwkv_mu), row(rwkv_w0), _pad_rows(rwkv_w_up, LANES), row(rwkv_a0),
            _pad_rows(rwkv_a_up, LANES, RWKV_DECAY_RANK), rwkv_g_up, row(rwkv_k_k),
            row(rwkv_k_a), row(rwkv_r_k), row(rwkv_gn_g), row(rwkv_gn_b))
    return gla, rwkv, w_att, w_gate


def kernel(x, c, w_ada, b_ada, w_in, gla_w_alpha, gla_b_alpha, gla_norm_g, rwkv_mu, rwkv_w0,
           rwkv_w_up, rwkv_a0, rwkv_a_up, rwkv_g_up, rwkv_k_k, rwkv_k_a, rwkv_r_k, rwkv_gn_g,
           rwkv_gn_b, w_branch, w_out, ln1_g, ln1_b, ffn_w1, ffn_w2, ln2_g, ln2_b):
    n_l = w_in.shape[0]
    row = lambda t: t.reshape(n_l, 1, -1)
    mod = _modulation(c, w_ada, b_ada)
    gla, rwkv, w_att, w_gate = _mixer_params(
        w_in, gla_w_alpha, gla_b_alpha, gla_norm_g, rwkv_mu, rwkv_w0, rwkv_w_up, rwkv_a0,
        rwkv_a_up, rwkv_g_up, rwkv_k_k, rwkv_k_a, rwkv_r_k, rwkv_gn_g, rwkv_gn_b)
    w_branch, w_out = w_branch.astype(BF16), w_out.astype(BF16)
    ffn_w1, ffn_w2 = ffn_w1.astype(BF16), ffn_w2.astype(BF16)
    ln1_g, ln1_b, ln2_g, ln2_b = row(ln1_g), row(ln1_b), row(ln2_g), row(ln2_b)
    for l in range(n_l):
        o_a = _gla_layer(l, x, mod, *gla)
        o_b = _rwkv_layer(l, x, mod, *rwkv)
        qkvs = _qkv_layer(l, x, mod, w_att)
        res = [_att_pattern(*qkv, dil) for qkv, dil in zip(qkvs, DILATIONS)]
        x = _merge_layer(l, x, mod, o_a, o_b, [o for o, _ in res], [lse for _, lse in res],
                         w_gate, w_branch, w_out, ln1_g, ln1_b)
        x = _ffn_layer(l, x, mod, ffn_w1, ffn_w2, ln2_g, ln2_b)
    return x
```

```python
import functools
import math

import jax
import jax.numpy as jnp
from jax import lax
from jax.experimental import pallas as pl
from jax.experimental.pallas import tpu as pltpu

F32 = jnp.float32
BF16 = jnp.bfloat16

D_MODEL = 1024
DEPTH = 4
GLA_HEADS, GLA_HEAD_K, GLA_HEAD_V = 4, 64, 128
GLA_DK, GLA_DV = GLA_HEADS * GLA_HEAD_K, GLA_HEADS * GLA_HEAD_V
GLA_GATE_RANK = 16
GLA_GATE_TAU = 16.0
GLA_NORM_EPS = 1e-5
RWKV_HEADS, RWKV_HEAD = 8, 64
RWKV_DIM = RWKV_HEADS * RWKV_HEAD
RWKV_DECAY_RANK, RWKV_ICLR_RANK, RWKV_GATE_RANK = 64, 64, 128
RWKV_IN = 3 * RWKV_DIM + RWKV_DECAY_RANK + RWKV_ICLR_RANK + RWKV_GATE_RANK
RWKV_GN_EPS = 64e-5
ATT_HEADS, ATT_HEAD = 8, 64
ATT_DIM = ATT_HEADS * ATT_HEAD
ATT_BLOCK = 128
DILATIONS = (1, 4, 16)
N_BRANCH = 3
FFN_HIDDEN = 2816
LN_EPS = 1e-5
DEEPNORM_ALPHA = (2 * DEPTH) ** 0.25
GLA_IN = 2 * GLA_DK + GLA_DV + GLA_GATE_RANK + GLA_DV

LOG2_E = math.log2(math.e)
LN_2 = math.log(2.0)
LANES = 128
CHUNK = 64
VMEM_LIMIT_CAP = 60000 * 1024

NN = ((1,), (0,))
NT = ((1,), (1,))
TN = ((0,), (0,))


def _dg(a, b, dims=NN):
    return lax.dot_general(a, b, (dims, ((), ())), preferred_element_type=F32)


def _dot1(a, b, dims=NN):
    return _dg(a.astype(BF16), b.astype(BF16), dims)


def _split2(a):
    hi = a.astype(BF16)
    lo = (a - hi.astype(F32)).astype(BF16)
    return hi, lo


def _dot3(a, b, dims=NN):
    ah, al = _split2(a)
    bh, bl = _split2(b)
    return _dg(ah, bh, dims) + (_dg(ah, bl, dims) + _dg(al, bh, dims))


def _dot_exact_lhs(m_bf16, a, parts=2):
    acc = None
    rem = a
    for _ in range(parts):
        hi = rem.astype(BF16)
        term = _dg(m_bf16, hi)
        acc = term if acc is None else acc + term
        rem = rem - hi.astype(F32)
    return acc


def _sigmoid(x):
    return 1.0 / (1.0 + jnp.exp(-x))


def _silu(x):
    return x * _sigmoid(x)


def _log_sigmoid(x):
    return jnp.minimum(x, 0.0) - jnp.log(1.0 + jnp.exp(-jnp.abs(x)))


def _layer_norm(z, g, b):
    mu = jnp.mean(z, axis=-1, keepdims=True)
    zc = z - mu
    var = jnp.mean(zc * zc, axis=-1, keepdims=True)
    return zc * lax.rsqrt(var + LN_EPS) * g + b


def _stack_heads(x, lane):
    lo = jnp.where(lane < RWKV_HEAD, x, 0.0)
    hi = jnp.where(lane >= RWKV_HEAD, x, 0.0)
    return jnp.concatenate([lo, hi], axis=0)


def _cparams(sem, vmem_bytes):
    return pltpu.CompilerParams(dimension_semantics=sem,
                                vmem_limit_bytes=int(min(vmem_bytes, VMEM_LIMIT_CAP)))


def _layer_param(l, shape):
    zeros = (0,) * len(shape)
    return pl.BlockSpec((None,) + tuple(shape), lambda *_: (l,) + zeros,
                        pipeline_mode=pl.Buffered(1))


def _mod_spec(l):
    return pl.BlockSpec((None, 1, 6, D_MODEL), lambda i, j: (l, i, 0, 0))


def _mod_kernel(ct_ref, w_ref, b_ref, o_ref, *, batch):
    s = _silu(ct_ref[...])
    tn = w_ref.shape[2]
    o_ref[0] = jnp.zeros(o_ref.shape[1:], F32)
    for bi in range(batch):
        sb = jnp.broadcast_to(s[:, bi:bi + 1], (D_MODEL, LANES))
        for j in range(tn // LANES):
            sl = slice(j * LANES, (j + 1) * LANES)
            acc = jnp.sum(w_ref[0, :, sl] * sb, axis=0, keepdims=True)
            o_ref[0, bi:bi + 1, sl] = acc + b_ref[0, :, sl]


def _modulation(c, w_ada, b_ada):
    n_l = w_ada.shape[0]
    b = c.shape[0]
    rows = 8
    assert b <= rows
    c_t = jnp.zeros((D_MODEL, rows), F32).at[:, :b].set(c.T)
    tn = 1536
    out = pl.pallas_call(
        functools.partial(_mod_kernel, batch=b),
        grid=(n_l, 6 * D_MODEL // tn),
        in_specs=[pl.BlockSpec((D_MODEL, rows), lambda l, j: (0, 0)),
                  pl.BlockSpec((1, D_MODEL, tn), lambda l, j: (l, 0, j)),
                  pl.BlockSpec((1, 1, tn), lambda l, j: (l, 0, j))],
        out_specs=pl.BlockSpec((1, rows, tn), lambda l, j: (l, 0, j)),
        out_shape=jax.ShapeDtypeStruct((n_l, rows, 6 * D_MODEL), F32),
        compiler_params=_cparams(("parallel", "parallel"), 4 * D_MODEL * tn * 4),
        name="adaln_mod",
    )(c_t, w_ada, b_ada.reshape(n_l, 1, 6 * D_MODEL))
    return out[:, :b].reshape(n_l, b, 6, D_MODEL)


def _gla_kernel(x_ref, mod_ref, wm_ref, wog_ref, wga_ref, wal_ref, bal_ref, ng_ref, o_ref,
                st_ref, p_s, cum_s, o_s, *, tile):
    @pl.when(pl.program_id(1) == 0)
    def _():
        st_ref[...] = jnp.zeros_like(st_ref)

    n_qkv = 2 * GLA_DK + GLA_DV
    n_half = 2
    hrows = tile // n_half
    sls = [slice(i * hrows, (i + 1) * hrows) for i in range(n_half)]
    us = [(x_ref[0, sl, :] * (1.0 + mod_ref[0, 1:2, :]) + mod_ref[0, 0:1, :]).astype(BF16)
          for sl in sls]
    a_los = []
    for sl, u in zip(sls, us):
        p_s[sl, :n_qkv] = _dg(u, wm_ref[...])
        p_s[sl, n_qkv:] = _dg(u, wog_ref[...])
        a_los.append(_dg(u, wga_ref[...]))
    ri = lax.broadcasted_iota(jnp.int32, (hrows, hrows), 0)
    ci = lax.broadcasted_iota(jnp.int32, (hrows, hrows), 1)
    ltri = jnp.where(((ri >> 6) == (ci >> 6)) & (ci <= ri), 1.0, 0.0).astype(BF16)
    for sl, a_lo in zip(sls, a_los):
        z = _dot3(a_lo, wal_ref[...]) + bal_ref[...]
        log_a = _log_sigmoid(z) * (1.0 / GLA_GATE_TAU)
        cum_s[sl, :] = _dot_exact_lhs(ltri, log_a)

    lane = lax.broadcasted_iota(jnp.int32, (CHUNK, LANES), 1)
    i2 = lax.broadcasted_iota(jnp.int32, (2 * CHUNK, 2 * CHUNK), 0)
    j2 = lax.broadcasted_iota(jnp.int32, (2 * CHUNK, 2 * CHUNK), 1)
    causal = ((i2 >> 6) == (j2 >> 6)) & (j2 <= i2)
    scale = GLA_HEAD_K ** -0.5

    n_chunks = tile // CHUNK
    n_pairs = GLA_HEADS // 2
    units = [(c, pr) for c in range(n_chunks) for pr in range(n_pairs)]

    qsms, ksms, kdsms, vsts, decs = [], [], [], [], []
    for c, pr in units:
        rows = slice(c * CHUNK, (c + 1) * CHUNK)
        lo = pr * LANES
        cumc = cum_s[rows, lo:lo + LANES]
        qc = p_s[rows, lo:lo + LANES]
        kc = p_s[rows, GLA_DK + lo:GLA_DK + lo + LANES]
        vbase = 2 * GLA_DK + 2 * pr * GLA_HEAD_V
        vsts.append(jnp.concatenate([p_s[rows, vbase:vbase + GLA_HEAD_V],
                                     p_s[rows, vbase + GLA_HEAD_V:vbase + 2 * GLA_HEAD_V]],
                                    axis=0).astype(BF16))
        cl = cumc[CHUNK - 1:CHUNK, :]
        qsms.append(_stack_heads(qc * scale * jnp.exp(cumc), lane).astype(BF16))
        ksms.append(_stack_heads(kc * jnp.exp(-cumc), lane).astype(BF16))
        kdsms.append(_stack_heads(kc * jnp.exp(cl - cumc), lane).astype(BF16))
        decs.append(jnp.exp(cl))
    atts = [jnp.where(causal, _dg(q_, k_, NT), 0.0).astype(BF16) for q_, k_ in zip(qsms, ksms)]
    o_intra = [_dg(a_, v_) for a_, v_ in zip(atts, vsts)]
    d_states = [_dg(v_, kd_, TN) for v_, kd_ in zip(vsts, kdsms)]

    gts = [None] * len(units)
    for pr in range(n_pairs):
        g = st_ref[pr]
        for c in range(n_chunks):
            i = c * n_pairs + pr
            gts[i] = g.astype(BF16)
            g = g * decs[i] + d_states[i]
        st_ref[pr] = g

    for i, (c, pr) in enumerate(units):
        o_st = o_intra[i] + _dg(qsms[i], gts[i], NT)
        rows = slice(c * CHUNK, (c + 1) * CHUNK)
        ob = 2 * pr * GLA_HEAD_V
        o_s[rows, ob:ob + GLA_HEAD_V] = o_st[:CHUNK]
        o_s[rows, ob + GLA_HEAD_V:ob + 2 * GLA_HEAD_V] = o_st[CHUNK:]

    og_base = 2 * GLA_DK + GLA_DV
    for h in range(GLA_HEADS):
        sl = slice(h * GLA_HEAD_V, (h + 1) * GLA_HEAD_V)
        oh = o_s[:, sl]
        on = oh * lax.rsqrt(jnp.mean(oh * oh, axis=-1, keepdims=True) + GLA_NORM_EPS) * ng_ref[...]
        og = p_s[:, og_base + h * GLA_HEAD_V:og_base + (h + 1) * GLA_HEAD_V]
        o_ref[0, :, sl] = on * _silu(og)


def _gla_layer(l, x, mod, wm, wog, wga, wal, bal, ng, tile=512):
    b, s, _ = x.shape
    n_qkv = 2 * GLA_DK + GLA_DV
    n_main = n_qkv + GLA_DV
    vmem = (4 * tile * D_MODEL * 4 + 4 * tile * GLA_DV * 4 + D_MODEL * (n_main + LANES) * 2
            + tile * (n_main + GLA_DK + GLA_DV) * 4 + 6 * tile * n_main * 4 + (8 << 20))
    return pl.pallas_call(
        functools.partial(_gla_kernel, tile=tile),
        grid=(b, s // tile),
        in_specs=[pl.BlockSpec((1, tile, D_MODEL), lambda i, j: (i, j, 0)),
                  _mod_spec(l),
                  _layer_param(l,(D_MODEL, n_qkv)),
                  _layer_param(l,(D_MODEL, GLA_DV)),
                  _layer_param(l,(D_MODEL, LANES)),
                  _layer_param(l,(LANES, GLA_DK)),
                  _layer_param(l,(1, GLA_DK)),
                  _layer_param(l,(1, GLA_HEAD_V))],
        out_specs=pl.BlockSpec((1, tile, GLA_DV), lambda i, j: (i, j, 0)),
        out_shape=jax.ShapeDtypeStruct((b, s, GLA_DV), F32),
        scratch_shapes=[pltpu.VMEM((GLA_HEADS // 2, GLA_HEAD_V, LANES), F32),
                        pltpu.VMEM((tile, n_main), F32),
                        pltpu.VMEM((tile, GLA_DK), F32),
                        pltpu.VMEM((tile, GLA_DV), F32)],
        compiler_params=_cparams(("parallel", "arbitrary"), vmem),
        name="gla_mixer",
    )(x, mod, wm, wog, wga, wal, bal, ng)


def _inv_unit_lower(ns, eye, m16, m32, m64):
    ds = [jnp.where(m16, n, 0.0) for n in ns]
    xs = [eye + d for d in ds]
    pws = [d.astype(BF16) for d in ds]
    pws = [_dg(p, p).astype(BF16) for p in pws]
    for level in range(2):
        prods = [_dg(p, jnp.concatenate([x.astype(BF16), p], axis=1)) for x, p in zip(xs, pws)]
        xs = [x + pr[:, :LANES] for x, pr in zip(xs, prods)]
        pws = [pr[:, LANES:].astype(BF16) for pr in prods]
    xs = [x + _dg(p, x.astype(BF16)) for x, p in zip(xs, pws)]
    for m, blk in ((m32, 16), (m64, 32)):
        n_rows = xs[0].shape[0]
        lower = [slice(s0, s0 + blk) for s0 in range(blk, n_rows, 2 * blk)]
        upper = [slice(s0, s0 + blk) for s0 in range(0, n_rows, 2 * blk)]
        xbs = [x.astype(BF16) for x in xs]
        xls = [jnp.concatenate([xb[sl] for sl in lower], axis=0) for xb in xbs]
        xos = [_dg(xl, jnp.where(m, n, 0.0).astype(BF16)) for xl, n in zip(xls, ns)]
        upd = [_dg(xo.astype(BF16), xb) for xo, xb in zip(xos, xbs)]
        new = []
        for x, up in zip(xs, upd):
            parts = []
            for i, (su, sl) in enumerate(zip(upper, lower)):
                parts += [x[su], x[sl] + up[i * blk:(i + 1) * blk]]
            new.append(jnp.concatenate(parts, axis=0))
        xs = new
    return xs


def _rwkv_kernel(x_ref, mod_ref, w_ref, mu_ref, w0_ref, wup_ref, a0_ref, aup_ref, gup_ref,
                 kk_ref, ka_ref, rk_ref, gng_ref, gnb_ref, o_ref,
                 st_ref, carry_ref, a_s, b_s, k_s, r_s, v_s, bb_s, k2_s, cum_s, y_s, bon_s, g_s,
                 *, tile):
    @pl.when(pl.program_id(1) == 0)
    def _():
        st_ref[...] = jnp.zeros_like(st_ref)
        carry_ref[...] = jnp.zeros_like(carry_ref)

    d = RWKV_DIM
    bi = lax.broadcasted_iota(jnp.int32, (LANES, LANES), 0)
    bj = lax.broadcasted_iota(jnp.int32, (LANES, LANES), 1)
    same64 = (bi >> 6) == (bj >> 6)
    seg = jnp.where(same64, 1.0, 0.0).astype(BF16)
    ltri = jnp.where(same64 & (bj <= bi), 1.0, 0.0).astype(BF16)

    def seg_sum(t):
        return jnp.concatenate(
            [_dot1(t[:, q * LANES:(q + 1) * LANES], seg) for q in range(d // LANES)],
            axis=1)

    n_half = tile // LANES
    sls = [slice(i * LANES, (i + 1) * LANES) for i in range(n_half)]
    projs = [_dg((x_ref[0, sl, :] * (1.0 + mod_ref[0, 1:2, :]) + mod_ref[0, 0:1, :]).astype(BF16),
                 w_ref[...]) for sl in sls]
    row = lax.broadcasted_iota(jnp.int32, (LANES, 1), 0)
    last = carry_ref[0:1, :]
    for sl, p in zip(sls, projs):
        prev = jnp.where(row == 0, last, pltpu.roll(p, 1, 0))
        last = p[LANES - 1:LANES, :]
        ps = p + (prev - p) * mu_ref[...]
        r = ps[:, 0:d]
        k = ps[:, d:2 * d]
        v = ps[:, 2 * d:3 * d]
        wa_lo = ps[:, 3 * d:3 * d + LANES]
        g_lo = ps[:, 3 * d + LANES:3 * d + 2 * LANES]
        wl = w0_ref[...] + _dot1(jnp.tanh(wa_lo), wup_ref[...])
        lw = -_sigmoid(wl) * math.exp(-0.5)
        a = _sigmoid(a0_ref[...] + _dot1(wa_lo, aup_ref[...]))
        g_s[sl, :] = _dot1(_sigmoid(g_lo), gup_ref[...])
        kk = k * kk_ref[...]
        k2 = k * (1.0 + (a - 1.0) * ka_ref[...])
        kk = kk / jnp.maximum(jnp.sqrt(seg_sum(kk * kk)), 1e-12)
        bb = kk * a
        bon_s[sl, :] = seg_sum(r * k2 * rk_ref[...]) * v
        cum = _dot_exact_lhs(ltri, lw)
        e_neg = jnp.exp(-cum)
        a_s[sl, :] = (-kk * jnp.exp(cum - lw)).astype(BF16)
        b_s[sl, :] = (bb * e_neg).astype(BF16)
        k_s[sl, :] = (k2 * e_neg).astype(BF16)
        r_s[sl, :] = (r * jnp.exp(cum)).astype(BF16)
        v_s[sl, :] = v.astype(BF16)
        bb_s[sl, :] = bb
        k2_s[sl, :] = k2
        cum_s[sl, :] = cum
    carry_ref[0:1, :] = last

    lane = lax.broadcasted_iota(jnp.int32, (CHUNK, LANES), 1)
    strict = same64 & (bj < bi)
    incl = same64 & (bj <= bi)
    m16 = (bi >> 4) == (bj >> 4)
    m32 = ((bi >> 5) == (bj >> 5)) & jnp.logical_not(m16)
    m64 = same64 & ((bi >> 5) != (bj >> 5))
    eye = jnp.where(bi == bj, 1.0, 0.0)
    h2 = 2 * CHUNK

    n_pairs = RWKV_HEADS // 2
    units = [(c, pr) for c in range(tile // CHUNK) for pr in range(n_pairs)]

    def rows_of(c):
        return slice(c * CHUNK, (c + 1) * CHUNK)

    def lanes_of(pr):
        return slice(pr * LANES, (pr + 1) * LANES)

    stacked = {}
    for name, ref in (("a", a_s), ("b", b_s), ("k", k_s), ("r", r_s), ("v", v_s)):
        stacked[name] = [_stack_heads(ref[rows_of(c), lanes_of(pr)], lane) for c, pr in units]
    scs = [_dg(jnp.concatenate([a_, r_], axis=0), jnp.concatenate([b_, k_], axis=0), NT)
           for a_, r_, b_, k_ in zip(stacked["a"], stacked["r"], stacked["b"], stacked["k"])]
    abs_ = [jnp.where(strict, sc[:h2, :h2], 0.0) for sc in scs]
    akvs = [_dg(jnp.where(strict, sc[:h2, h2:], 0.0).astype(BF16), v_)
            for sc, v_ in zip(scs, stacked["v"])]
    rbks = [jnp.concatenate([jnp.where(incl, sc[h2:, :h2], 0.0),
                             jnp.where(incl, sc[h2:, h2:], 0.0)], axis=1).astype(BF16)
            for sc in scs]
    tinvs = _inv_unit_lower(abs_, eye, m16, m32, m64)
    wus = [_dg(t.astype(BF16), jnp.concatenate([a_, akv.astype(BF16)], axis=1))
           for t, a_, akv in zip(tinvs, stacked["a"], akvs)]
    wrs = [jnp.concatenate([wu[:, :LANES].astype(BF16), r_], axis=0)
           for wu, r_ in zip(wus, stacked["r"])]

    for c in range(tile // CHUNK):
        idx = [c * n_pairs + pr for pr in range(n_pairs)]
        gts = [st_ref[pr] for pr in range(n_pairs)]
        wrgs = [_dg(wrs[i], g_.astype(BF16), NT) for i, g_ in zip(idx, gts)]
        ums = [(wrg[:h2] + wus[i][:, LANES:]).astype(BF16) for i, wrg in zip(idx, wrgs)]
        uvs = [jnp.concatenate([um, stacked["v"][i]], axis=0) for i, um in zip(idx, ums)]
        ys = [wrg[h2:] + _dg(rbks[i], uv) for i, wrg, uv in zip(idx, wrgs, uvs)]
        for pr in range(n_pairs):
            cumc = cum_s[rows_of(c), lanes_of(pr)]
            cl = cumc[CHUNK - 1:CHUNK, :]
            dec = jnp.exp(cl - cumc)
            bkd = jnp.concatenate([_stack_heads(bb_s[rows_of(c), lanes_of(pr)] * dec, lane),
                                   _stack_heads(k2_s[rows_of(c), lanes_of(pr)] * dec, lane)],
                                  axis=0).astype(BF16)
            st_ref[pr] = gts[pr] * jnp.exp(cl) + _dg(uvs[pr], bkd, TN)
            y_s[rows_of(c), lanes_of(pr)] = ys[pr][:CHUNK] + ys[pr][CHUNK:]

    y = y_s[...]
    inv_n = 1.0 / RWKV_HEAD
    mu_h = seg_sum(y) * inv_n
    yc = y - mu_h
    var = seg_sum(yc * yc) * inv_n
    yn = yc * lax.rsqrt(var + RWKV_GN_EPS) * gng_ref[...] + gnb_ref[...]
    o_ref[0] = (yn + bon_s[...]) * g_s[...]


def _rwkv_layer(l, x, mod, w, mu, w0, wup, a0, aup, gup, k_k, k_a, r_k, gn_g, gn_b, tile=256):
    b, s, _ = x.shape
    d = RWKV_DIM
    vec = lambda: _layer_param(l,(1, d))
    vmem = (4 * tile * D_MODEL * 4 + 4 * tile * d * 4 + D_MODEL * RWKV_IN * 2
            + 11 * tile * d * 4 + 8 * tile * RWKV_IN * 4 + (12 << 20))
    return pl.pallas_call(
        functools.partial(_rwkv_kernel, tile=tile),
        grid=(b, s // tile),
        in_specs=[pl.BlockSpec((1, tile, D_MODEL), lambda i, j: (i, j, 0)),
                  _mod_spec(l),
                  _layer_param(l,(D_MODEL, RWKV_IN)),
                  _layer_param(l,(1, RWKV_IN)),
                  vec(), _layer_param(l,(LANES, d)), vec(), _layer_param(l,(LANES, d)),
                  _layer_param(l,(LANES, d)), vec(), vec(), vec(), vec(), vec()],
        out_specs=pl.BlockSpec((1, tile, d), lambda i, j: (i, j, 0)),
        out_shape=jax.ShapeDtypeStruct((b, s, d), F32),
        scratch_shapes=[pltpu.VMEM((RWKV_HEADS // 2, LANES, LANES), F32),
                        pltpu.VMEM((8, RWKV_IN), F32)]
                       + [pltpu.VMEM((tile, d), BF16) for _ in range(5)]
                       + [pltpu.VMEM((tile, d), F32) for _ in range(6)],
        compiler_params=_cparams(("parallel", "arbitrary"), vmem),
        name="rwkv7_mixer",
    )(x, mod, w, mu, w0, wup, a0, aup, gup, k_k, k_a, r_k, gn_g, gn_b)


def _qkv_kernel(x_ref, mod_ref, w_ref, *refs, tile):
    out_refs, p_s = refs[:-1], refs[-1]
    n_grp = 3 * ATT_DIM // LANES
    per = ATT_DIM // LANES
    n_half = 2
    half = tile // n_half
    ps = []
    for hf in range(n_half):
        x = x_ref[0, hf * half:(hf + 1) * half, :]
        u = (x * (1.0 + mod_ref[0, 1:2, :]) + mod_ref[0, 0:1, :]).astype(BF16)
        ps.append(_dg(u, w_ref[...]))
    for hf in range(n_half):
        for g in range(n_grp):
            p_s[g, hf * half:(hf + 1) * half, :] = ps[hf][:, g * LANES:(g + 1) * LANES]
        for pi, dil in enumerate(DILATIONS):
            q_ref, k_ref, v_ref = out_refs[3 * pi:3 * pi + 3]
            n_out = half // dil
            dst = slice(hf * n_out, (hf + 1) * n_out)
            for r in range(dil):
                rows = pl.ds(hf * half + r, n_out, stride=dil) if dil > 1 else \
                    slice(hf * half, (hf + 1) * half)
                grp = lambda t: jnp.concatenate([p_s[t * per + g, rows, :] for g in range(per)],
                                                axis=1)
                q_ref[0, r, dst, :] = (grp(0) * (LOG2_E * ATT_HEAD ** -0.5)).astype(BF16)
                k_ref[0, r, dst, :] = grp(1).astype(BF16)
                v_ref[0, r, dst, :] = grp(2).astype(BF16)


def _qkv_layer(l, x, mod, w, tile=512):
    b, s, _ = x.shape
    out_shape, out_specs = [], []
    for dil in DILATIONS:
        for _ in range(3):
            out_shape.append(jax.ShapeDtypeStruct((b, dil, s // dil, ATT_DIM), BF16))
            out_specs.append(pl.BlockSpec((1, dil, tile // dil, ATT_DIM), lambda i, j: (i, 0, j, 0)))
    vmem = (4 * tile * D_MODEL * 4 + D_MODEL * 3 * ATT_DIM * 2 + 4 * tile * 3 * ATT_DIM * 4
            + 2 * 9 * tile * ATT_DIM * 2 + (4 << 20))
    outs = pl.pallas_call(
        functools.partial(_qkv_kernel, tile=tile),
        grid=(b, s // tile),
        in_specs=[pl.BlockSpec((1, tile, D_MODEL), lambda i, j: (i, j, 0)),
                  _mod_spec(l),
                  _layer_param(l,(D_MODEL, 3 * ATT_DIM))],
        out_specs=out_specs,
        out_shape=out_shape,
        scratch_shapes=[pltpu.VMEM((3 * ATT_DIM // LANES, tile, LANES), F32)],
        compiler_params=_cparams(("parallel", "parallel"), vmem),
        name="att_qkv",
    )(x, mod, w)
    return [outs[3 * pi:3 * pi + 3] for pi in range(len(DILATIONS))]


ATT_SPAN = 2048


def _att_kernel(q_ref, kp_ref, kc_ref, vp_ref, vc_ref, o_ref, l_ref, kbuf, vbuf, *, dilation):
    n_sub = ATT_SPAN // dilation
    n_blk = n_sub // ATT_BLOCK
    span = pl.program_id(1)
    kbuf[:, :ATT_BLOCK] = kp_ref[0]
    kbuf[:, ATT_BLOCK:] = kc_ref[0]
    vbuf[:, :ATT_BLOCK] = vp_ref[0]
    vbuf[:, ATT_BLOCK:] = vc_ref[0]

    qi = lax.broadcasted_iota(jnp.int32, (ATT_BLOCK, 2 * ATT_BLOCK), 0)
    kj = lax.broadcasted_iota(jnp.int32, (ATT_BLOCK, 2 * ATT_BLOCK), 1)
    steps = qi + ATT_BLOCK - kj
    window = (steps >= 0) & (steps <= ATT_BLOCK)
    dist = (steps * dilation).astype(F32)
    lane_q = lax.broadcasted_iota(jnp.int32, (ATT_BLOCK, LANES), 1)
    zero = jnp.zeros((), BF16)
    heads = range(ATT_HEADS)
    biases = [jnp.where(window, dist * -(LOG2_E * 2.0 ** (-8.0 * (h + 1) / ATT_HEADS)), -jnp.inf)
              for h in heads]

    lane_k = lax.broadcasted_iota(jnp.int32, (2 * ATT_BLOCK, LANES), 1)
    one = jnp.ones((), BF16)
    pair_of = lambda t, h: t[:, (h // 2) * LANES:(h // 2 + 1) * LANES]
    own = lambda lane, h: (lane >= ATT_HEAD) if h % 2 else (lane < ATT_HEAD)

    def scores(uidx):
        r = uidx >> (n_blk.bit_length() - 1)
        n = uidx & (n_blk - 1)
        row0 = pl.multiple_of(n * ATT_BLOCK, ATT_BLOCK)
        q = q_ref[0, r, pl.ds(row0, ATT_BLOCK), :]
        kk = kbuf[r, pl.ds(row0, 2 * ATT_BLOCK), :]
        ss = [_dg(jnp.where(own(lane_q, h), pair_of(q, h), zero), pair_of(kk, h), NT)
              for h in heads]
        return r, n, row0, ss

    def finish(first_span, r, n, row0, ss):
        vv = vbuf[r, pl.ds(row0, 2 * ATT_BLOCK), :]
        if first_span:
            first_key = jnp.where(n == 0, ATT_BLOCK, 0)
            head_mask = jnp.where(kj >= first_key, 0.0, -jnp.inf)
            ss = [s + (biases[h] + head_mask) for h, s in zip(heads, ss)]
        else:
            ss = [s + biases[h] for h, s in zip(heads, ss)]
        ms = [jnp.max(s, axis=-1, keepdims=True) for s in ss]
        es = [jnp.exp2(s - m).astype(BF16) for s, m in zip(ss, ms)]
        pvs = [_dg(e, jnp.where(own(lane_k, h), pair_of(vv, h), one)) for h, e in zip(heads, es)]
        m_blk = jnp.zeros((ATT_BLOCK, LANES), F32)
        d_blk = jnp.ones((ATT_BLOCK, LANES), F32)
        tok0 = n * (ATT_BLOCK * dilation) + r
        if dilation == 1:
            rows = pl.ds(pl.multiple_of(tok0, ATT_BLOCK), ATT_BLOCK)
        else:
            rows = pl.ds(tok0, ATT_BLOCK, stride=dilation)
        for p in range(ATT_HEADS // 2):
            even, odd = pvs[2 * p], pvs[2 * p + 1]
            num = jnp.where(lane_q < ATT_HEAD, even, odd)
            den_sw = jnp.where(lane_q < ATT_HEAD, odd, even)
            den = pltpu.roll(den_sw, ATT_HEAD, 1)
            o_ref[0, p, rows, :] = num / den
            m_blk = jnp.where(lane_q == 2 * p, ms[2 * p], m_blk)
            m_blk = jnp.where(lane_q == 2 * p + 1, ms[2 * p + 1], m_blk)
            d_blk = jnp.where(lane_q == 2 * p, den, d_blk)
            d_blk = jnp.where(lane_q == 2 * p + 1, den_sw, d_blk)
        l_ref[0, rows, :] = (m_blk + jnp.log2(d_blk)) * LN_2

    per_trip = 8

    def make_body(first_span):
        def body(i, carry):
            us = [scores(per_trip * i + t) for t in range(2)]
            for t in range(per_trip):
                if t + 2 < per_trip:
                    us.append(scores(per_trip * i + t + 2))
                finish(first_span, *us[t])
            return carry
        return body

    n_units = dilation * n_blk

    @pl.when(span == 0)
    def _():
        lax.fori_loop(0, n_units // per_trip, make_body(True), 0)

    @pl.when(span != 0)
    def _():
        lax.fori_loop(0, n_units // per_trip, make_body(False), 0)


def _att_pattern(q, k, v, dilation):
    b, _, n_res, _ = q.shape
    s = dilation * n_res
    n_sub = ATT_SPAN // dilation
    n_blk = n_sub // ATT_BLOCK
    cur = pl.BlockSpec((1, dilation, n_sub, ATT_DIM), lambda i, j: (i, 0, j, 0))
    prev = pl.BlockSpec((1, dilation, ATT_BLOCK, ATT_DIM),
                        lambda i, j: (i, 0, jnp.maximum(j * n_blk - 1, 0), 0))
    buf = pltpu.VMEM((dilation, ATT_BLOCK + n_sub, ATT_DIM), BF16)
    vmem = (2 * (3 * ATT_SPAN + 2 * dilation * ATT_BLOCK) * ATT_DIM * 2
            + 2 * dilation * (ATT_BLOCK + n_sub) * ATT_DIM * 2
            + 2 * ATT_SPAN * (ATT_DIM + LANES) * 4 + (8 << 20))
    return pl.pallas_call(
        functools.partial(_att_kernel, dilation=dilation),
        grid=(b, s // ATT_SPAN),
        in_specs=[cur, prev, cur, prev, cur],
        out_specs=[pl.BlockSpec((1, ATT_HEADS // 2, ATT_SPAN, LANES), lambda i, j: (i, 0, j, 0)),
                   pl.BlockSpec((1, ATT_SPAN, LANES), lambda i, j: (i, j, 0))],
        out_shape=[jax.ShapeDtypeStruct((b, ATT_HEADS // 2, s, LANES), F32),
                   jax.ShapeDtypeStruct((b, s, LANES), F32)],
        scratch_shapes=[buf, buf],
        compiler_params=_cparams(("parallel", "parallel"), vmem),
        name=f"dilated_att_d{dilation}",
    )(q, k, k, v, v)


def _merge_kernel(x_ref, mod_ref, oa_ref, ob_ref, o1_ref, o4_ref, o16_ref, l1_ref, l4_ref, l16_ref,
                  wg_ref, wb_ref, wo_ref, g_ref, b_ref, out_ref):
    n_half = 2
    rows = x_ref.shape[1] // n_half
    lane = lax.broadcasted_iota(jnp.int32, (rows, LANES), 1)

    def expand_heads(w):
        cols = []
        for p in range(ATT_HEADS // 2):
            even = jnp.broadcast_to(w[:, 2 * p:2 * p + 1], w.shape)
            odd = jnp.broadcast_to(w[:, 2 * p + 1:2 * p + 2], w.shape)
            cols.append(jnp.where(lane < ATT_HEAD, even, odd))
        return jnp.concatenate(cols, axis=1)

    sls = [slice(i * rows, (i + 1) * rows) for i in range(n_half)]
    lane_cat = lambda ref, sl: jnp.concatenate([ref[0, p, sl, :] for p in range(ATT_HEADS // 2)],
                                               axis=1)
    xs = [x_ref[0, sl, :] for sl in sls]
    us = [(x * (1.0 + mod_ref[0, 1:2, :]) + mod_ref[0, 0:1, :]).astype(BF16) for x in xs]
    gates = [_sigmoid(_dg(u, wg_ref[...])) for u in us]
    o_cs = []
    for sl in sls:
        l1, l4, l16 = l1_ref[0, sl, :], l4_ref[0, sl, :], l16_ref[0, sl, :]
        m = jnp.maximum(jnp.maximum(l1, l4), l16)
        e1, e4, e16 = jnp.exp(l1 - m), jnp.exp(l4 - m), jnp.exp(l16 - m)
        inv = 1.0 / (e1 + e4 + e16)
        o_cs.append(expand_heads(e1 * inv) * lane_cat(o1_ref, sl)
                    + expand_heads(e4 * inv) * lane_cat(o4_ref, sl)
                    + expand_heads(e16 * inv) * lane_cat(o16_ref, sl))
    pa = [_dot1(oa_ref[0, sl, :], wb_ref[0]) for sl in sls]
    pb = [_dot1(ob_ref[0, sl, :], wb_ref[1]) for sl in sls]
    pc = [_dot1(o_c, wb_ref[2]) for o_c in o_cs]
    merged = [g_[:, :D_MODEL] * a_ + g_[:, D_MODEL:2 * D_MODEL] * b_ + g_[:, 2 * D_MODEL:] * c_
              for g_, a_, b_, c_ in zip(gates, pa, pb, pc)]
    hs = [_dot1(mg, wo_ref[...]) for mg in merged]
    for sl, x, h in zip(sls, xs, hs):
        z = DEEPNORM_ALPHA * x + (1.0 + mod_ref[0, 2:3, :]) * h
        out_ref[0, sl, :] = _layer_norm(z, g_ref[...], b_ref[...])


def _merge_layer(l, x, mod, oa, ob, oc, lses, wg, wb, wo, g, bta, tile=512):
    b, s, _ = x.shape
    xs = pl.BlockSpec((1, tile, D_MODEL), lambda i, j: (i, j, 0))
    bs = pl.BlockSpec((1, tile, ATT_DIM), lambda i, j: (i, j, 0))
    ls = pl.BlockSpec((1, tile, LANES), lambda i, j: (i, j, 0))
    cs = pl.BlockSpec((1, ATT_HEADS // 2, tile, LANES), lambda i, j: (i, 0, j, 0))
    vmem = (4 * tile * D_MODEL * 4 + 10 * tile * ATT_DIM * 4 + 6 * tile * LANES * 4
            + (3 * D_MODEL * D_MODEL + 3 * ATT_DIM * D_MODEL + D_MODEL * D_MODEL) * 2
            + 6 * tile * 3 * D_MODEL * 4 + (4 << 20))
    return pl.pallas_call(
        _merge_kernel,
        grid=(b, s // tile),
        in_specs=[xs, _mod_spec(l),
                  bs, bs, cs, cs, cs, ls, ls, ls,
                  _layer_param(l,(D_MODEL, N_BRANCH * D_MODEL)),
                  _layer_param(l,(N_BRANCH, ATT_DIM, D_MODEL)),
                  _layer_param(l,(D_MODEL, D_MODEL)),
                  _layer_param(l,(1, D_MODEL)), _layer_param(l,(1, D_MODEL))],
        out_specs=xs,
        out_shape=jax.ShapeDtypeStruct((b, s, D_MODEL), F32),
        compiler_params=_cparams(("parallel", "parallel"), vmem),
        name="merge_ln1",
    )(x, mod, oa, ob, oc[0], oc[1], oc[2], lses[0], lses[1], lses[2], wg, wb, wo, g, bta)


def _ffn_kernel(x_ref, mod_ref, w1_ref, w2_ref, g_ref, b_ref, out_ref):
    n_half = 2
    rows = x_ref.shape[1] // n_half
    sls = [slice(i * rows, (i + 1) * rows) for i in range(n_half)]
    xs = [x_ref[0, sl, :] for sl in sls]
    us = [(x * (1.0 + mod_ref[0, 4:5, :]) + mod_ref[0, 3:4, :]).astype(BF16) for x in xs]
    hs = [_dg(u, w1_ref[...]) for u in us]
    acts = [(_silu(h[:, :FFN_HIDDEN]) * h[:, FFN_HIDDEN:]).astype(BF16) for h in hs]
    ys = [_dg(act, w2_ref[...]) for act in acts]
    for sl, x, y in zip(sls, xs, ys):
        z = DEEPNORM_ALPHA * x + (1.0 + mod_ref[0, 5:6, :]) * y
        out_ref[0, sl, :] = _layer_norm(z, g_ref[...], b_ref[...])


def _ffn_layer(l, x, mod, w1, w2, g, bta, tile=512):
    b, s, _ = x.shape
    xs = pl.BlockSpec((1, tile, D_MODEL), lambda i, j: (i, j, 0))
    vmem = (4 * tile * D_MODEL * 4 + 3 * D_MODEL * FFN_HIDDEN * 2
            + 4 * tile * 2 * FFN_HIDDEN * 4 + (4 << 20))
    return pl.pallas_call(
        _ffn_kernel,
        grid=(b, s // tile),
        in_specs=[xs, _mod_spec(l),
                  _layer_param(l,(D_MODEL, 2 * FFN_HIDDEN)),
                  _layer_param(l,(FFN_HIDDEN, D_MODEL)),
                  _layer_param(l,(1, D_MODEL)), _layer_param(l,(1, D_MODEL))],
        out_specs=xs,
        out_shape=jax.ShapeDtypeStruct((b, s, D_MODEL), F32),
        compiler_params=_cparams(("parallel", "parallel"), vmem),
        name="ffn_ln2",
    )(x, mod, w1, w2, g, bta)


def _pad_rows(m, rows, offset=0):
    out = jnp.zeros((m.shape[0], rows) + m.shape[2:], m.dtype)
    return out.at[:, offset:offset + m.shape[1]].set(m)


_W_IN_GROUPS = (
    (0, 2 * GLA_DK + GLA_DV),
    (2 * GLA_DK + GLA_DV + GLA_GATE_RANK, GLA_DV),
    (2 * GLA_DK + GLA_DV, GLA_GATE_RANK),
    (GLA_IN, RWKV_IN),
    (GLA_IN + RWKV_IN, 3 * ATT_DIM),
    (GLA_IN + RWKV_IN + 3 * ATT_DIM, N_BRANCH * D_MODEL),
)


def _w_in_split_kernel(wt_ref, *out_refs):
    for (c0, width), o_ref in zip(_W_IN_GROUPS, out_refs):
        if width >= LANES:
            o_ref[0] = wt_ref[0, c0:c0 + width, :].T.astype(BF16)
        else:
            lane = lax.broadcasted_iota(jnp.int32, (LANES, LANES), 1)
            o_ref[0] = jnp.where(lane < width, wt_ref[0, c0:c0 + LANES, :].T, 0.0).astype(BF16)


def _w_in_split(w_in):
    n_l, d, n = w_in.shape
    widths = [max(width, LANES) for _, width in _W_IN_GROUPS]
    vmem = 4 * LANES * n * 4 + 4 * LANES * sum(widths) * 2 + (4 << 20)
    return pl.pallas_call(
        _w_in_split_kernel,
        grid=(n_l, d // LANES),
        in_specs=[pl.BlockSpec((1, n, LANES), lambda l, i: (l, 0, i))],
        out_specs=[pl.BlockSpec((1, LANES, wd), lambda l, i: (l, i, 0)) for wd in widths],
        out_shape=[jax.ShapeDtypeStruct((n_l, d, wd), BF16) for wd in widths],
        compiler_params=_cparams(("parallel", "parallel"), vmem),
        name="w_in_split",
    )(jnp.swapaxes(w_in, 1, 2))


def _mixer_params(w_in, gla_w_alpha, gla_b_alpha, gla_norm_g, rwkv_mu, rwkv_w0, rwkv_w_up,
                  rwkv_a0, rwkv_a_up, rwkv_g_up, rwkv_k_k, rwkv_k_a, rwkv_r_k, rwkv_gn_g,
                  rwkv_gn_b):
    n_l = w_in.shape[0]
    row = lambda t: t.reshape(n_l, 1, -1)
    w_gla, w_og, w_ga, w_rwkv, w_att, w_gate = _w_in_split(w_in)
    gla = (w_gla, w_og, w_ga, _pad_rows(gla_w_alpha, LANES), row(gla_b_alpha), row(gla_norm_g))
    rwkv = (w_rwkv, row(rwkv_mu), row(rwkv_w0), _pad_rows(rwkv_w_up, LANES), row(rwkv_a0),
            _pad_rows(rwkv_a_up, LANES, RWKV_DECAY_RANK), rwkv_g_up, row(rwkv_k_k),
            row(rwkv_k_a), row(rwkv_r_k), row(rwkv_gn_g), row(rwkv_gn_b))
    return gla, rwkv, w_att, w_gate


def kernel(x, c, w_ada, b_ada, w_in, gla_w_alpha, gla_b_alpha, gla_norm_g, rwkv_mu, rwkv_w0,
           rwkv_w_up, rwkv_a0, rwkv_a_up, rwkv_g_up, rwkv_k_k, rwkv_k_a, rwkv_r_k, rwkv_gn_g,
           rwkv_gn_b, w_branch, w_out, ln1_g, ln1_b, ffn_w1, ffn_w2, ln2_g, ln2_b):
    n_l = w_in.shape[0]
    row = lambda t: t.reshape(n_l, 1, -1)
    mod = _modulation(c, w_ada, b_ada)
    gla, rwkv, w_att, w_gate = _mixer_params(
        w_in, gla_w_alpha, gla_b_alpha, gla_norm_g, rwkv_mu, rwkv_w0, rwkv_w_up, rwkv_a0,
        rwkv_a_up, rwkv_g_up, rwkv_k_k, rwkv_k_a, rwkv_r_k, rwkv_gn_g, rwkv_gn_b)
    w_branch, w_out = w_branch.astype(BF16), w_out.astype(BF16)
    ffn_w1, ffn_w2 = ffn_w1.astype(BF16), ffn_w2.astype(BF16)
    ln1_g, ln1_b, ln2_g, ln2_b = row(ln1_g), row(ln1_b), row(ln2_g), row(ln2_b)
    for l in range(n_l):
        o_a = _gla_layer(l, x, mod, *gla)
        o_b = _rwkv_layer(l, x, mod, *rwkv)
        qkvs = _qkv_layer(l, x, mod, w_att)
        res = [_att_pattern(*qkv, dil) for qkv, dil in zip(qkvs, DILATIONS)]
        x = _merge_layer(l, x, mod, o_a, o_b, [o for o, _ in res], [lse for _, lse in res],
                         w_gate, w_branch, w_out, ln1_g, ln1_b)
        x = _ffn_layer(l, x, mod, ffn_w1, ffn_w2, ln2_g, ln2_b)
    return x
```
